```python
import math
import jax, jax.numpy as jnp
from jax import lax
import numpy as np


D_MODEL = 4096
BATCH = 2
SEQ = 4096
DEPTH = 2

N_MIXERS = 4
GROUP_WIDTH = D_MODEL // N_MIXERS
RMS_EPS = 1e-6

LRU_WIDTH = GROUP_WIDTH
LRU_BLOCKS = 8
LRU_BLOCK = LRU_WIDTH // LRU_BLOCKS
LRU_CONV = 4
LRU_C = 8.0

GLA_HEADS = 4
GLA_DV = GROUP_WIDTH // GLA_HEADS
GLA_DK = GLA_DV // 2
GLA_GATE_RANK = 16
GLA_TAU = 16.0
GLA_CHUNK = 64

S5_WIDTH = GROUP_WIDTH
S5_GROUP = 16
S5_GROUPS = S5_WIDTH // S5_GROUP
S5_STATE = 64
S5_DT_MIN = 0.001
S5_DT_MAX = 0.1

MOBA_HEADS = 8
MOBA_HD = GROUP_WIDTH // MOBA_HEADS
MOBA_WIDTH = MOBA_HEADS * MOBA_HD
MOBA_BLOCK = 256
MOBA_TOPK = 3
MOBA_Q_CHUNK = 32
ROPE_DIMS = MOBA_HD // 4
ROPE_THETA = 500000.0

PEER_HEADS = 8
PEER_NKEYS = 128
PEER_EXPERTS = PEER_NKEYS * PEER_NKEYS
PEER_DKEY = 256
PEER_TOPK = 16
PEER_TOKEN_CHUNK = 128

IN_SPLITS = (LRU_WIDTH, LRU_WIDTH,
             GLA_HEADS * GLA_DK, GLA_HEADS * GLA_DK, GLA_HEADS * GLA_DV, GLA_GATE_RANK, GLA_HEADS * GLA_DV,
             S5_WIDTH,
             MOBA_WIDTH, MOBA_WIDTH, MOBA_WIDTH)
IN_WIDTH = sum(IN_SPLITS)

kernel_name = 'hybrid_parallel_heads_peer'


def rms_norm(x, g):
    xf = x.astype(jnp.float32)
    xf = xf * lax.rsqrt(jnp.mean(xf * xf, axis=-1, keepdims=True) + RMS_EPS)
    return xf.astype(x.dtype) * g


def linear_scan(a, b):
    def comb(l, r):
        al, bl = l
        ar, br = r
        return al * ar, ar * bl + br
    return lax.associative_scan(comb, (a, b), axis=1)[1]


def complex_linear_scan(a_re, a_im, b_re, b_im):
    def comb(l, r):
        alr, ali, blr, bli = l
        arr, ari, brr, bri = r
        return (arr * alr - ari * ali, arr * ali + ari * alr,
                arr * blr - ari * bli + brr, arr * bli + ari * blr + bri)
    _, _, xr, xi = lax.associative_scan(comb, (a_re, a_im, b_re, b_im), axis=1)
    return xr, xi


def partial_rope(t, pos):
    half = ROPE_DIMS // 2
    inv = jnp.power(ROPE_THETA, -jnp.arange(half, dtype=jnp.float32) / half)
    ang = pos.astype(jnp.float32)[:, None] * inv[None, :]
    cos, sin = jnp.cos(ang), jnp.sin(ang)
    tf = t[..., :ROPE_DIMS].astype(jnp.float32)
    t1, t2 = tf[..., :half], tf[..., half:]
    rot = jnp.concatenate([t1 * cos - t2 * sin, t2 * cos + t1 * sin], axis=-1).astype(t.dtype)
    return jnp.concatenate([rot, t[..., ROPE_DIMS:]], axis=-1)


def rglru_mixer(xb, gb, conv_w, conv_b, wa, ba, wx, bx, lam):
    B, S, _ = xb.shape
    xc = lax.conv_general_dilated(xb, conv_w[:, None, :], window_strides=(1,),
                                  padding=[(LRU_CONV - 1, 0)],
                                  dimension_numbers=('NWC', 'WIO', 'NWC'),
                                  feature_group_count=LRU_WIDTH) + conv_b
    xblk = xc.reshape(B, S, LRU_BLOCKS, LRU_BLOCK)
    r = jax.nn.sigmoid(jnp.einsum('bsni,nij->bsnj', xblk, wa).reshape(B, S, LRU_WIDTH) + ba)
    i = jax.nn.sigmoid(jnp.einsum('bsni,nij->bsnj', xblk, wx).reshape(B, S, LRU_WIDTH) + bx)
    log_a = -LRU_C * r.astype(jnp.float32) * jax.nn.softplus(-lam.astype(jnp.float32))
    a = jnp.exp(log_a)
    mult = jnp.sqrt(-jnp.expm1(2.0 * log_a))
    h = linear_scan(a, mult * (i * xc).astype(jnp.float32))
    return h.astype(xb.dtype) * jax.nn.gelu(gb)


def gla_mixer(q, k, v, glr, og, wg2, bg, norm_g):
    B, S, _ = q.shape
    N, C = S // GLA_CHUNK, GLA_CHUNK
    f32 = jnp.float32

    def heads(t, d):
        return t.reshape(B, N, C, GLA_HEADS, d).transpose(0, 3, 1, 2, 4).astype(f32)

    g = jax.nn.log_sigmoid((glr @ wg2 + bg).astype(f32)) / GLA_TAU
    qh = heads(q, GLA_DK) * (GLA_DK ** -0.5)
    kh = heads(k, GLA_DK)
    vh = heads(v, GLA_DV)
    bc = jnp.cumsum(heads(g, GLA_DK), axis=3)
    qe = qh * jnp.exp(bc)
    ke = kh * jnp.exp(-bc)
    causal = jnp.tril(jnp.ones((C, C), dtype=bool))
    att = jnp.where(causal, jnp.einsum('bhncd,bhnjd->bhncj', qe, ke), 0.0)
    o = jnp.einsum('bhncj,bhnje->bhnce', att, vh)
    b_last = bc[:, :, :, -1:, :]
    kd = kh * jnp.exp(b_last - bc)
    U = jnp.einsum('bhncd,bhnce->nbhde', kd, vh)
    decay = jnp.exp(b_last[:, :, :, 0]).transpose(2, 0, 1, 3)

    def step(state, inp):
        dec, u = inp
        return dec[..., None] * state + u, state

    _, s_prev = lax.scan(step, jnp.zeros((B, GLA_HEADS, GLA_DK, GLA_DV), f32), (decay, U))
    o = o + jnp.einsum('bhncd,nbhde->bhnce', qe, s_prev)
    o = o.transpose(0, 2, 3, 1, 4).reshape(B, S, GLA_HEADS, GLA_DV)
    o = o * lax.rsqrt(jnp.mean(o * o, axis=-1, keepdims=True) + RMS_EPS)
    o = o.reshape(B, S, GLA_HEADS * GLA_DV).astype(q.dtype) * norm_g
    return o * jax.nn.silu(og)


def s5_mixer(u, a_re, a_im, log_step, b_re, b_im, c_re, c_im, d, w_glu, b_glu):
    B, S, _ = u.shape
    f32 = jnp.float32
    uf = u.reshape(B, S, S5_GROUPS, S5_GROUP).astype(f32)
    step = jnp.exp(log_step.astype(f32))[:, None]
    lr, li = a_re.astype(f32), a_im.astype(f32)
    mag = jnp.exp(lr * step)
    ang = li * step
    abar_re, abar_im = mag * jnp.cos(ang), mag * jnp.sin(ang)
    den = lr * lr + li * li
    f_re = ((abar_re - 1.0) * lr + abar_im * li) / den
    f_im = (abar_im * lr - (abar_re - 1.0) * li) / den
    br, bi = b_re.astype(f32), b_im.astype(f32)
    bb_re = f_re[..., None] * br - f_im[..., None] * bi
    bb_im = f_re[..., None] * bi + f_im[..., None] * br
    bu_re = jnp.einsum('bsgh,gph->bsgp', uf, bb_re)
    bu_im = jnp.einsum('bsgh,gph->bsgp', uf, bb_im)
    shape = bu_re.shape
    xr, xi = complex_linear_scan(jnp.broadcast_to(abar_re, shape), jnp.broadcast_to(abar_im, shape), bu_re, bu_im)
    y = jnp.einsum('bsgp,ghp->bsgh', xr, c_re.astype(f32)) - jnp.einsum('bsgp,ghp->bsgh', xi, c_im.astype(f32))
    y = y.reshape(B, S, S5_WIDTH) + d.astype(f32) * u.astype(f32)
    z = jax.nn.gelu(y).astype(u.dtype)
    return z * jax.nn.sigmoid(z @ w_glu + b_glu)


def moba_mixer(q, k, v):
    B, S, _ = q.shape
    f32 = jnp.float32

    def heads(t):
        return t.reshape(B, S, MOBA_HEADS, MOBA_HD).transpose(0, 2, 1, 3)

    pos = jnp.arange(S)
    qh = partial_rope(heads(q), pos)
    kh = partial_rope(heads(k), pos)
    vh = heads(v)
    nb = -(-S // MOBA_BLOCK)
    pad = nb * MOBA_BLOCK - S
    kb = jnp.pad(kh, ((0, 0), (0, 0), (0, pad), (0, 0))).reshape(B, MOBA_HEADS, nb, MOBA_BLOCK, MOBA_HD)
    vb = jnp.pad(vh, ((0, 0), (0, 0), (0, pad), (0, 0))).reshape(B, MOBA_HEADS, nb, MOBA_BLOCK, MOBA_HD)
    kmean = jnp.mean(kb.astype(f32), axis=3)
    n_sel = min(MOBA_TOPK, nb)
    nqc = S // MOBA_Q_CHUNK
    qc = qh.reshape(B, MOBA_HEADS, nqc, MOBA_Q_CHUNK, MOBA_HD).transpose(2, 0, 1, 3, 4)
    scale = MOBA_HD ** -0.5
    gather = jax.vmap(jax.vmap(lambda blocks, ix: blocks[ix]))

    def chunk(args):
        qi, ci = args
        q_pos = ci * MOBA_Q_CHUNK + jnp.arange(MOBA_Q_CHUNK)
        own = (ci * MOBA_Q_CHUNK) // MOBA_BLOCK
        qf = qi.astype(f32)
        gate = jnp.einsum('bhqd,bhnd->bhqn', qf, kmean)
        gate = jnp.where(jnp.arange(nb) < own, gate, -jnp.inf)
        _, idx = lax.top_k(gate, n_sel)
        valid = jnp.arange(n_sel) < own
        kg = gather(kb, idx).astype(f32)
        vg = gather(vb, idx).astype(f32)
        s_sel = jnp.einsum('bhqd,bhqmkd->bhqmk', qf, kg) * scale
        s_sel = jnp.where(valid[:, None], s_sel, -jnp.inf).reshape(B, MOBA_HEADS, MOBA_Q_CHUNK, n_sel * MOBA_BLOCK)
        k_own = lax.dynamic_index_in_dim(kb, own, axis=2, keepdims=False).astype(f32)
        v_own = lax.dynamic_index_in_dim(vb, own, axis=2, keepdims=False).astype(f32)
        s_own = jnp.einsum('bhqd,bhkd->bhqk', qf, k_own) * scale
        k_pos = own * MOBA_BLOCK + jnp.arange(MOBA_BLOCK)
        s_own = jnp.where(k_pos[None, :] <= q_pos[:, None], s_own, -jnp.inf)
        p = jax.nn.softmax(jnp.concatenate([s_sel, s_own], axis=-1), axis=-1)
        p_sel = p[..., :n_sel * MOBA_BLOCK].reshape(B, MOBA_HEADS, MOBA_Q_CHUNK, n_sel, MOBA_BLOCK)
        p_own = p[..., n_sel * MOBA_BLOCK:]
        o = jnp.einsum('bhqmk,bhqmkd->bhqd', p_sel, vg) + jnp.einsum('bhqk,bhkd->bhqd', p_own, v_own)
        return o.astype(q.dtype)

    out = lax.map(chunk, (qc, jnp.arange(nqc)))
    return out.transpose(1, 0, 3, 2, 4).reshape(B, S, MOBA_WIDTH)


def peer_ffn(h, wq, subkeys, u_tab, v_tab):
    B, S, D = h.shape
    T = B * S
    ht = h.reshape(T, D)
    q = (ht @ wq).reshape(T, PEER_HEADS, 2, PEER_DKEY // 2).astype(jnp.float32)
    s = jnp.einsum('thcd,hckd->thck', q, subkeys.astype(jnp.float32))
    v1, i1 = lax.top_k(s[:, :, 0], PEER_TOPK)
    v2, i2 = lax.top_k(s[:, :, 1], PEER_TOPK)
    cand = (v1[..., :, None] + v2[..., None, :]).reshape(T, PEER_HEADS, PEER_TOPK * PEER_TOPK)
    sc, ci = lax.top_k(cand, PEER_TOPK)
    e1 = jnp.take_along_axis(i1, ci // PEER_TOPK, axis=-1)
    e2 = jnp.take_along_axis(i2, ci % PEER_TOPK, axis=-1)
    experts = (e1 * PEER_NKEYS + e2).reshape(T, PEER_HEADS * PEER_TOPK)
    gates = jax.nn.softmax(sc, axis=-1).reshape(T, PEER_HEADS * PEER_TOPK)
    nc = T // PEER_TOKEN_CHUNK

    def chunk(args):
        xc, ec, gc = args
        act = jax.nn.gelu(jnp.einsum('td,tkd->tk', xc, u_tab[ec]).astype(jnp.float32), approximate=False)
        w = (gc * act).astype(xc.dtype)
        return jnp.einsum('tk,tkd->td', w, v_tab[ec])

    out = lax.map(chunk, (ht.reshape(nc, PEER_TOKEN_CHUNK, D),
                          experts.reshape(nc, PEER_TOKEN_CHUNK, -1),
                          gates.reshape(nc, PEER_TOKEN_CHUNK, -1)))
    return out.reshape(B, S, D)


def setup_inputs(seed: int = 0) -> dict:
    key = jax.random.key(seed)
    ks = jax.random.split(key, 32)
    f32 = jnp.float32
    L = DEPTH

    def nrm(k, shape, scale):
        return jax.random.normal(k, shape, f32) * scale

    a0 = jax.random.uniform(ks[9], (L, LRU_WIDTH), f32, 0.9, 0.999)
    s0 = a0 ** (1.0 / LRU_C)
    return {
        'x': nrm(ks[0], (BATCH, SEQ, D_MODEL), 1.0),
        'norm1_g': 1.0 + nrm(ks[1], (L, D_MODEL), 0.02),
        'w_in': nrm(ks[2], (L, D_MODEL, IN_WIDTH), D_MODEL ** -0.5),
        'lru_conv_w': nrm(ks[3], (L, LRU_CONV, LRU_WIDTH), LRU_CONV ** -0.5),
        'lru_conv_b': nrm(ks[4], (L, LRU_WIDTH), 0.01),
        'lru_wa': nrm(ks[5], (L, LRU_BLOCKS, LRU_BLOCK, LRU_BLOCK), LRU_BLOCK ** -0.5),
        'lru_ba': nrm(ks[6], (L, LRU_WIDTH), 0.01),
        'lru_wx': nrm(ks[7], (L, LRU_BLOCKS, LRU_BLOCK, LRU_BLOCK), LRU_BLOCK ** -0.5),
        'lru_bx': nrm(ks[8], (L, LRU_WIDTH), 0.01),
        'lru_lambda': jnp.log(s0) - jnp.log1p(-s0),
        'gla_wg2': nrm(ks[10], (L, GLA_GATE_RANK, GLA_HEADS * GLA_DK), GLA_GATE_RANK ** -0.5),
        'gla_bg': nrm(ks[11], (L, GLA_HEADS * GLA_DK), 0.01),
        'gla_norm_g': 1.0 + nrm(ks[12], (L, GLA_HEADS * GLA_DV), 0.02),
        's5_a_re': -0.5 + nrm(ks[13], (L, S5_GROUPS, S5_STATE), 0.01),
        's5_a_im': math.pi * jnp.arange(S5_STATE, dtype=f32) + nrm(ks[14], (L, S5_GROUPS, S5_STATE), 0.01),
        's5_log_step': jax.random.uniform(ks[15], (L, S5_GROUPS), f32, math.log(S5_DT_MIN), math.log(S5_DT_MAX)),
        's5_b_re': nrm(ks[16], (L, S5_GROUPS, S5_STATE, S5_GROUP), (2 * S5_GROUP) ** -0.5),
        's5_b_im': nrm(ks[17], (L, S5_GROUPS, S5_STATE, S5_GROUP), (2 * S5_GROUP) ** -0.5),
        's5_c_re': nrm(ks[18], (L, S5_GROUPS, S5_GROUP, S5_STATE), 1.0),
        's5_c_im': nrm(ks[19], (L, S5_GROUPS, S5_GROUP, S5_STATE), 1.0),
        's5_d': nrm(ks[20], (L, S5_WIDTH), 0.5),
        's5_w_glu': nrm(ks[21], (L, S5_WIDTH, S5_WIDTH), S5_WIDTH ** -0.5),
        's5_b_glu': nrm(ks[22], (L, S5_WIDTH), 0.01),
        'w_out': nrm(ks[23], (L, N_MIXERS * GROUP_WIDTH, D_MODEL), (N_MIXERS * GROUP_WIDTH) ** -0.5),
        'norm2_g': 1.0 + nrm(ks[24], (L, D_MODEL), 0.02),
        'peer_wq': nrm(ks[25], (L, D_MODEL, PEER_HEADS * PEER_DKEY), D_MODEL ** -0.5),
        'peer_subkeys': nrm(ks[26], (L, PEER_HEADS, 2, PEER_NKEYS, PEER_DKEY // 2), (PEER_DKEY // 2) ** -0.5),
        'peer_u': nrm(ks[27], (L, PEER_EXPERTS, D_MODEL), D_MODEL ** -0.5),
        'peer_v': nrm(ks[28], (L, PEER_EXPERTS, D_MODEL), PEER_HEADS ** -0.5),
        'final_norm_g': 1.0 + nrm(ks[29], (D_MODEL,), 0.02),
    }


def reference(x, norm1_g, w_in, lru_conv_w, lru_conv_b, lru_wa, lru_ba, lru_wx, lru_bx, lru_lambda,
              gla_wg2, gla_bg, gla_norm_g, s5_a_re, s5_a_im, s5_log_step, s5_b_re, s5_b_im, s5_c_re, s5_c_im,
              s5_d, s5_w_glu, s5_b_glu, w_out, norm2_g, peer_wq, peer_subkeys, peer_u, peer_v, final_norm_g):
    split_points = [int(p) for p in np.cumsum(IN_SPLITS)[:-1]]
    for l in range(DEPTH):
        h = rms_norm(x, norm1_g[l])
        proj = h @ w_in[l]
        (lru_x, lru_gate, gla_q, gla_k, gla_v, gla_lr, gla_og, s5_u,
         moba_q, moba_k, moba_v) = jnp.split(proj, split_points, axis=-1)
        y_a = rglru_mixer(lru_x, lru_gate, lru_conv_w[l], lru_conv_b[l], lru_wa[l], lru_ba[l],
                          lru_wx[l], lru_bx[l], lru_lambda[l])
        y_b = gla_mixer(gla_q, gla_k, gla_v, gla_lr, gla_og, gla_wg2[l], gla_bg[l], gla_norm_g[l])
        y_c = s5_mixer(s5_u, s5_a_re[l], s5_a_im[l], s5_log_step[l], s5_b_re[l], s5_b_im[l],
                       s5_c_re[l], s5_c_im[l], s5_d[l], s5_w_glu[l], s5_b_glu[l])
        y_d = moba_mixer(moba_q, moba_k, moba_v)
        mixed = jnp.concatenate([y_a.astype(x.dtype), y_b.astype(x.dtype),
                                 y_c.astype(x.dtype), y_d.astype(x.dtype)], axis=-1)
        x = x + mixed @ w_out[l]
        x = x + peer_ffn(rms_norm(x, norm2_g[l]), peer_wq[l], peer_subkeys[l], peer_u[l], peer_v[l])
    return rms_norm(x, final_norm_g)
```

```python
import functools
import math

import jax
import jax.numpy as jnp
from jax import lax
from jax.experimental import pallas as pl
from jax.experimental.pallas import tpu as pltpu

F32 = jnp.float32
BF16 = jnp.bfloat16

LANES = 128
SUBLANES = 8
VMEM_LIMIT_BYTES = 56 * 2**20

RMS_EPS = 1e-6
N_MIXERS = 4

LRU_BLOCKS = 8
LRU_CONV = 4
LRU_C = 8.0
LRU_ROWS = 256

GLA_HEADS = 4
GLA_GATE_RANK = 16
GLA_TAU = 16.0
GLA_CHUNK = 64
GLA_TILE = 512

S5_GROUP = 16
S5_STATE = 64
S5_SLAB_GROUPS = LANES // S5_GROUP
S5_SLAB_STATES = S5_SLAB_GROUPS * S5_STATE
S5_TILE = 256

MOBA_HEADS = 8
MOBA_BLOCK = 256
MOBA_TOPK = 3
ROPE_THETA = 500000.0
MASK_VALUE = -1e30

PEER_HEADS = 8
PEER_NKEYS = 128
PEER_TOPK = 16
PEER_TOKEN_TILE = 512
PEER_EXPERT_TILE = 1024


def _params(*semantics):
    return pltpu.CompilerParams(dimension_semantics=semantics, vmem_limit_bytes=VMEM_LIMIT_BYTES)


def _rmsnorm_kernel(x_ref, g_ref, o_ref, *maybe_ot_ref):
    x = x_ref[...]
    y = x * lax.rsqrt(jnp.mean(x * x, axis=-1, keepdims=True) + RMS_EPS) * g_ref[...]
    o_ref[...] = y.astype(o_ref.dtype)
    if maybe_ot_ref:
        maybe_ot_ref[0][...] = y.T.astype(maybe_ot_ref[0].dtype)


def rmsnorm(x, g, out_dtype, with_transpose=False, tm=256):
    T, D = x.shape
    out_shape = [jax.ShapeDtypeStruct((T, D), out_dtype)]
    out_specs = [pl.BlockSpec((tm, D), lambda i: (i, 0))]
    if with_transpose:
        out_shape.append(jax.ShapeDtypeStruct((D, T), BF16))
        out_specs.append(pl.BlockSpec((D, tm), lambda i: (0, i)))
    res = pl.pallas_call(
        _rmsnorm_kernel,
        grid=(T // tm,),
        in_specs=[pl.BlockSpec((tm, D), lambda i: (i, 0)), pl.BlockSpec((1, D), lambda i: (0, 0))],
        out_specs=out_specs,
        out_shape=out_shape,
        compiler_params=_params("parallel"),
    )(x, g.reshape(1, D))
    return res if with_transpose else res[0]


def _mm_kernel(x_ref, w_ref, *rest, nk, has_res):
    if has_res:
        r_ref, o_ref, acc_ref = rest
    else:
        o_ref, acc_ref = rest
    k = pl.program_id(2)

    @pl.when(k == 0)
    def _():
        acc_ref[...] = jnp.zeros_like(acc_ref)

    acc_ref[...] += jnp.dot(x_ref[...], w_ref[...], preferred_element_type=F32)

    @pl.when(k == nk - 1)
    def _():
        acc = acc_ref[...]
        if has_res:
            acc = acc + r_ref[...]
        o_ref[...] = acc.astype(o_ref.dtype)


def matmul(x, w, res=None, *, tm=1024, tn=1024, tk=2048, out_dtype=F32):
    M, K = x.shape
    _, N = w.shape
    tm, tn, tk = min(tm, M), min(tn, N), min(tk, K)
    assert M % tm == 0 and N % tn == 0 and K % tk == 0
    nk = K // tk
    in_specs = [pl.BlockSpec((tm, tk), lambda i, j, k: (i, k)),
                pl.BlockSpec((tk, tn), lambda i, j, k: (k, j))]
    args = [x, w]
    if res is not None:
        in_specs.append(pl.BlockSpec((tm, tn), lambda i, j, k: (i, j)))
        args.append(res)
    return pl.pallas_call(
        functools.partial(_mm_kernel, nk=nk, has_res=res is not None),
        grid=(M // tm, N // tn, nk),
        in_specs=in_specs,
        out_specs=pl.BlockSpec((tm, tn), lambda i, j, k: (i, j)),
        out_shape=jax.ShapeDtypeStruct((M, N), out_dtype),
        scratch_shapes=[pltpu.VMEM((tm, tn), F32)],
        compiler_params=_params("parallel", "parallel", "arbitrary"),
    )(*args)


def _lru_kernel(x_ref, gate_ref, cw_ref, cb_ref, wa_ref, ba_ref, wx_ref, bx_ref, lam_ref, o_ref, *, seq):
    R = LRU_ROWS
    cw = cw_ref[...]
    cb = cb_ref[...]
    ba = ba_ref[...]
    bx = bx_ref[...]
    neg_c_softplus = -LRU_C * jax.nn.softplus(-lam_ref[...])
    wa = wa_ref[0]
    wx = wx_ref[0]
    row = lax.broadcasted_iota(jnp.int32, (R, LANES), 0)
    row8 = lax.broadcasted_iota(jnp.int32, (SUBLANES, LANES), 0)

    def body(c, h):
        r0 = pl.multiple_of(c * R, R)
        xt = x_ref[pl.ds(r0, R), :]
        p0 = pl.multiple_of(jnp.maximum(r0 - SUBLANES, 0), SUBLANES)
        prev = jnp.where(c > 0, x_ref[pl.ds(p0, SUBLANES), :], 0.0)
        xc = xt * cw[LRU_CONV - 1:LRU_CONV] + cb
        for d in range(1, LRU_CONV):
            rolled = pltpu.roll(xt, d, 0)
            head = jnp.where(row8 < d, pltpu.roll(prev, d, 0), rolled[:SUBLANES])
            shifted = jnp.concatenate([head, rolled[SUBLANES:]], axis=0)
            xc = xc + shifted * cw[LRU_CONV - 1 - d:LRU_CONV - d]
        xb = xc.astype(BF16)
        r = jax.nn.sigmoid(jnp.dot(xb, wa, preferred_element_type=F32) + ba)
        i = jax.nn.sigmoid(jnp.dot(xb, wx, preferred_element_type=F32) + bx)
        log_a = r * neg_c_softplus
        a = jnp.exp(log_a)
        b = jnp.sqrt(1.0 - a * a) * (i * xc)
        d = 1
        while d < R:
            keep = row >= d
            a_sh = jnp.where(keep, pltpu.roll(a, d, 0), 1.0)
            b_sh = jnp.where(keep, pltpu.roll(b, d, 0), 0.0)
            b = a * b_sh + b
            a = a * a_sh
            d *= 2
        hs = b + a * h
        o_ref[pl.ds(r0, R), :] = (hs * jax.nn.gelu(gate_ref[pl.ds(r0, R), :])).astype(o_ref.dtype)
        return hs[R - 1:R, :]

    lax.fori_loop(0, seq // R, body, jnp.zeros((1, LANES), F32))


def rglru(pm, batch, seq, x_col, gate_col, conv_w, conv_b, wa, ba, wx, bx, lam):
    T = batch * seq
    W = LRU_BLOCKS * LANES
    vec = lambda v: v.reshape(1, W)
    vspec = pl.BlockSpec((1, LANES), lambda b, n: (0, n))
    wspec = pl.BlockSpec((1, LANES, LANES), lambda b, n: (n, 0, 0))
    return pl.pallas_call(
        functools.partial(_lru_kernel, seq=seq),
        grid=(batch, LRU_BLOCKS),
        in_specs=[pl.BlockSpec((seq, LANES), lambda b, n: (b, x_col + n)),
                  pl.BlockSpec((seq, LANES), lambda b, n: (b, gate_col + n)),
                  pl.BlockSpec((LRU_CONV, LANES), lambda b, n: (0, n)),
                  vspec, wspec, vspec, wspec, vspec, vspec],
        out_specs=pl.BlockSpec((seq, LANES), lambda b, n: (b, n)),
        out_shape=jax.ShapeDtypeStruct((T, W), BF16),
        compiler_params=_params("parallel", "parallel"),
    )(pm, pm, conv_w, vec(conv_b), wa.astype(BF16), vec(ba), wx.astype(BF16), vec(bx), vec(lam))


def _gla_kernel(q_ref, k_ref, v_ref, og_ref, lr_ref, wg_ref, bg_ref, ng_ref, o_ref, st_ref):
    C = GLA_CHUNK
    dk = q_ref.shape[-1]
    dv = v_ref.shape[-1]

    @pl.when(pl.program_id(2) == 0)
    def _():
        st_ref[...] = jnp.zeros_like(st_ref)

    wg = wg_ref[0]
    bg = bg_ref[0]
    ng = ng_ref[0]
    ri = lax.broadcasted_iota(jnp.int32, (C, C), 0)
    ci = lax.broadcasted_iota(jnp.int32, (C, C), 1)
    causal = ri >= ci
    tril = causal.astype(F32)
    scale = dk ** -0.5
    nt = (((1,), (1,)), ((), ()))
    tn = (((0,), (0,)), ((), ()))

    for c in range(GLA_TILE // C):
        rows = pl.ds(c * C, C)
        q = q_ref[rows, :]
        k = k_ref[rows, :]
        v = v_ref[rows, :].astype(BF16)
        pre = jnp.dot(lr_ref[rows, :], wg, precision=lax.Precision.HIGHEST, preferred_element_type=F32) + bg
        g = jax.nn.log_sigmoid(pre) / GLA_TAU
        bc = jnp.dot(tril, g, precision=lax.Precision.HIGHEST, preferred_element_type=F32)
        b_last = bc[C - 1:C, :]
        qe = (q * scale * jnp.exp(bc)).astype(BF16)
        ke = (k * jnp.exp(-bc)).astype(BF16)
        kd = (k * jnp.exp(b_last - bc)).astype(BF16)
        att = lax.dot_general(qe, ke, nt, preferred_element_type=F32)
        att = jnp.where(causal, att, 0.0).astype(BF16)
        st = st_ref[...]
        o = jnp.dot(att, v, preferred_element_type=F32)
        o = o + lax.dot_general(qe, st.astype(BF16), nt, preferred_element_type=F32)
        st_ref[...] = st * jnp.exp(b_last) + lax.dot_general(v, kd, tn, preferred_element_type=F32)
        o = o * lax.rsqrt(jnp.mean(o * o, axis=-1, keepdims=True) + RMS_EPS)
        o_ref[rows, :] = (o * ng * jax.nn.silu(og_ref[rows, :])).astype(o_ref.dtype)


def gla(pm, plr, batch, seq, q_col, k_col, v_col, og_col, wg2, bg, norm_g):
    T = batch * seq
    H = GLA_HEADS
    dk = wg2.shape[-1] // H
    dv = norm_g.shape[-1] // H
    assert dk == LANES and dv == 2 * LANES
    nt = seq // GLA_TILE
    wg = jnp.zeros((LANES, H * dk), F32).at[:GLA_GATE_RANK].set(wg2)
    wg = wg.reshape(LANES, H, dk).transpose(1, 0, 2)
    tok = lambda b, h, t: b * nt + t
    return pl.pallas_call(
        _gla_kernel,
        grid=(batch, H, nt),
        in_specs=[pl.BlockSpec((GLA_TILE, dk), lambda b, h, t: (tok(b, h, t), q_col + h)),
                  pl.BlockSpec((GLA_TILE, dk), lambda b, h, t: (tok(b, h, t), k_col + h)),
                  pl.BlockSpec((GLA_TILE, dv), lambda b, h, t: (tok(b, h, t), v_col // 2 + h)),
                  pl.BlockSpec((GLA_TILE, dv), lambda b, h, t: (tok(b, h, t), og_col // 2 + h)),
                  pl.BlockSpec((GLA_TILE, LANES), lambda b, h, t: (tok(b, h, t), 0)),
                  pl.BlockSpec((1, LANES, dk), lambda b, h, t: (h, 0, 0)),
                  pl.BlockSpec((1, 1, dk), lambda b, h, t: (h, 0, 0)),
                  pl.BlockSpec((1, 1, dv), lambda b, h, t: (h, 0, 0))],
        out_specs=pl.BlockSpec((GLA_TILE, dv), lambda b, h, t: (tok(b, h, t), h)),
        out_shape=jax.ShapeDtypeStruct((T, H * dv), BF16),
        scratch_shapes=[pltpu.VMEM((dv, dk), F32)],
        compiler_params=_params("parallel", "parallel", "arbitrary"),
    )(pm, pm, pm, pm, plr, wg, bg.reshape(H, 1, dk), norm_g.reshape(H, 1, dv))


def _s5_kernel(u_ref, bcat_ref, ccat_ref, pr_ref, pi_ref, d_ref, z_ref, x_ref, carry_ref):
    TT = S5_TILE
    NS = 2 * S5_SLAB_STATES

    @pl.when(pl.program_id(2) == 0)
    def _():
        carry_ref[...] = jnp.zeros_like(carry_ref)

    u = u_ref[...]
    pr = pr_ref[0]
    pi = pi_ref[0]

    def swap(t):
        return pltpu.roll(t, S5_SLAB_STATES, 1)

    x = jnp.dot(u.astype(BF16), bcat_ref[0], preferred_element_type=F32)
    sub = lax.broadcasted_iota(jnp.int32, (TT, NS), 0) % SUBLANES
    for d in (1, 2, 4):
        sh = jnp.where(sub >= d, pltpu.roll(x, d, 0), 0.0)
        x = x + pr[d - 1:d] * sh + pi[d - 1:d] * swap(sh)
    x_ref[...] = x

    def carry_step(i, carry):
        rows = pl.ds(pl.multiple_of(i * SUBLANES, SUBLANES), SUBLANES)
        cb = jnp.broadcast_to(carry, (SUBLANES, NS))
        blk = x_ref[rows, :] + pr * cb + pi * swap(cb)
        x_ref[rows, :] = blk
        return blk[SUBLANES - 1:SUBLANES, :]

    carry_ref[...] = lax.fori_loop(0, TT // SUBLANES, carry_step, carry_ref[...])
    y = jnp.dot(x_ref[...].astype(BF16), ccat_ref[0], preferred_element_type=F32) + d_ref[...] * u
    z_ref[...] = jax.nn.gelu(y)


def s5_prepare(a_re, a_im, log_step, b_re, b_im, c_re, c_im):
    G, P = a_re.shape
    H = S5_GROUP
    ns = G // S5_SLAB_GROUPS
    step = jnp.exp(log_step)[:, None]
    mag = jnp.exp(a_re * step)
    ang = a_im * step
    abar_re, abar_im = mag * jnp.cos(ang), mag * jnp.sin(ang)
    den = a_re * a_re + a_im * a_im
    f_re = ((abar_re - 1.0) * a_re + abar_im * a_im) / den
    f_im = (abar_im * a_re - (abar_re - 1.0) * a_im) / den
    bb_re = f_re[..., None] * b_re - f_im[..., None] * b_im
    bb_im = f_re[..., None] * b_im + f_im[..., None] * b_re
    eye = jnp.eye(S5_SLAB_GROUPS, dtype=F32)

    def in_slab(bb):
        t = bb.reshape(ns, S5_SLAB_GROUPS, P, H)
        return jnp.einsum('sgph,gk->sghkp', t, eye).reshape(ns, LANES, S5_SLAB_STATES)

    def out_slab(cc):
        t = cc.reshape(ns, S5_SLAB_GROUPS, H, P)
        return jnp.einsum('sghp,gk->sgpkh', t, eye).reshape(ns, S5_SLAB_STATES, LANES)

    bcat = jnp.concatenate([in_slab(bb_re), in_slab(bb_im)], axis=-1).astype(BF16)
    ccat = jnp.concatenate([out_slab(c_re), -out_slab(c_im)], axis=1).astype(BF16)
    pw_re, pw_im = [abar_re], [abar_im]
    for _ in range(SUBLANES - 1):
        r, i = pw_re[-1], pw_im[-1]
        pw_re.append(r * abar_re - i * abar_im)
        pw_im.append(r * abar_im + i * abar_re)
    slab = lambda t: jnp.stack(t, 0).reshape(SUBLANES, ns, S5_SLAB_STATES).transpose(1, 0, 2)
    pr, pi = slab(pw_re), slab(pw_im)
    return bcat, ccat, jnp.concatenate([pr, pr], -1), jnp.concatenate([-pi, pi], -1)


def s5_ssm(pm, batch, seq, u_col, prep, d):
    T = batch * seq
    bcat, ccat, pr, pi = prep
    ns = bcat.shape[0]
    NS = 2 * S5_SLAB_STATES
    nt = seq // S5_TILE
    return pl.pallas_call(
        _s5_kernel,
        grid=(batch, ns, nt),
        in_specs=[pl.BlockSpec((S5_TILE, LANES), lambda b, s, t: (b * nt + t, u_col + s)),
                  pl.BlockSpec((1, LANES, NS), lambda b, s, t: (s, 0, 0)),
                  pl.BlockSpec((1, NS, LANES), lambda b, s, t: (s, 0, 0)),
                  pl.BlockSpec((1, SUBLANES, NS), lambda b, s, t: (s, 0, 0)),
                  pl.BlockSpec((1, SUBLANES, NS), lambda b, s, t: (s, 0, 0)),
                  pl.BlockSpec((1, LANES), lambda b, s, t: (0, s))],
        out_specs=pl.BlockSpec((S5_TILE, LANES), lambda b, s, t: (b * nt + t, s)),
        out_shape=jax.ShapeDtypeStruct((T, ns * LANES), F32),
        scratch_shapes=[pltpu.VMEM((S5_TILE, NS), F32), pltpu.VMEM((1, NS), F32)],
        compiler_params=_params("parallel", "parallel", "arbitrary"),
    )(pm, bcat, ccat, pr, pi, d.reshape(1, -1))


def _glu_kernel(z_ref, w_ref, b_ref, o_ref):
    z = z_ref[...]
    y = jnp.dot(z.astype(BF16), w_ref[...], preferred_element_type=F32) + b_ref[...]
    o_ref[...] = (z * jax.nn.sigmoid(y)).astype(o_ref.dtype)


def glu(z, w, b, tm=512):
    T, W = z.shape
    return pl.pallas_call(
        _glu_kernel,
        grid=(T // tm,),
        in_specs=[pl.BlockSpec((tm, W), lambda i: (i, 0)),
                  pl.BlockSpec((W, W), lambda i: (0, 0)),
                  pl.BlockSpec((1, W), lambda i: (0, 0))],
        out_specs=pl.BlockSpec((tm, W), lambda i: (i, 0)),
        out_shape=jax.ShapeDtypeStruct((T, W), BF16),
        compiler_params=_params("parallel"),
    )(z, w.astype(BF16), b.reshape(1, W))


def _rope(t, cos, sin_lo, sin_hi, rope_half):
    return (t * cos + pltpu.roll(t, LANES - rope_half, 1) * sin_lo + pltpu.roll(t, rope_half, 1) * sin_hi)


def rope_tables(seq, head_dim):
    rope_dims = head_dim // 4
    half = rope_dims // 2
    inv = jnp.power(ROPE_THETA, -jnp.arange(half, dtype=F32) / half)
    ang = jnp.arange(seq).astype(F32)[:, None] * inv[None, :]
    cos, sin = jnp.cos(ang), jnp.sin(ang)
    pad = head_dim - rope_dims
    zeros = jnp.zeros((seq, half), F32)
    cos_t = jnp.concatenate([cos, cos, jnp.ones((seq, pad), F32)], axis=-1)
    sin_lo = jnp.concatenate([-sin, zeros, jnp.zeros((seq, pad), F32)], axis=-1)
    sin_hi = jnp.concatenate([zeros, sin, jnp.zeros((seq, pad), F32)], axis=-1)
    return cos_t, sin_lo, sin_hi, half


def _moba_k_kernel(k_ref, cos_ref, slo_ref, shi_ref, kr_ref, km_ref, *, nb, rope_half):
    for n in range(nb):
        rows = pl.ds(n * MOBA_BLOCK, MOBA_BLOCK)
        kr = _rope(k_ref[rows, :], cos_ref[rows, :], slo_ref[rows, :], shi_ref[rows, :], rope_half)
        kr_ref[rows, :] = kr.astype(kr_ref.dtype)
        km_ref[0, 0, n:n + 1, :] = jnp.mean(kr, axis=0, keepdims=True)


def _moba_attn_kernel(q_ref, cos_ref, slo_ref, shi_ref, kr_ref, v_ref, km_ref, o_ref,
                      m_ref, l_ref, acc_ref, *, nb, rope_half):
    BLK = MOBA_BLOCK
    i = pl.program_id(2)
    hd = q_ref.shape[-1]
    scale = hd ** -0.5
    nt = (((1,), (1,)), ((), ()))
    qr = _rope(q_ref[...], cos_ref[...], slo_ref[...], shi_ref[...], rope_half)
    gate = lax.dot_general(qr, km_ref[0, 0], nt, precision=lax.Precision.HIGHEST,
                           preferred_element_type=F32)
    blk_id = lax.broadcasted_iota(jnp.int32, (BLK, nb), 1)
    rank = jnp.zeros((BLK, nb), jnp.int32)
    for m in range(nb):
        gm = gate[:, m:m + 1]
        beats = (gm > gate) | ((gm == gate) & (m < blk_id))
        rank = rank + jnp.where(beats & (m < i), 1, 0)
    sel = ((blk_id < i) & (rank < MOBA_TOPK)).astype(F32)
    qb = qr.astype(BF16)

    own = pl.ds(pl.multiple_of(i * BLK, BLK), BLK)
    s = lax.dot_general(qb, kr_ref[own, :], nt, preferred_element_type=F32) * scale
    ri = lax.broadcasted_iota(jnp.int32, (BLK, BLK), 0)
    ci = lax.broadcasted_iota(jnp.int32, (BLK, BLK), 1)
    s = jnp.where(ci <= ri, s, MASK_VALUE)
    m0 = jnp.max(s, axis=-1, keepdims=True)
    p = jnp.exp(s - m0)
    m_ref[...] = jnp.broadcast_to(m0, m_ref.shape)
    l_ref[...] = jnp.broadcast_to(jnp.sum(p, axis=-1, keepdims=True), l_ref.shape)
    acc_ref[...] = jnp.dot(p.astype(BF16), v_ref[own, :].astype(BF16), preferred_element_type=F32)

    for n in range(nb - 1):
        @pl.when(n < i)
        def _():
            rows = pl.ds(n * BLK, BLK)
            sn = lax.dot_general(qb, kr_ref[rows, :], nt, preferred_element_type=F32) * scale
            sn = jnp.where(sel[:, n:n + 1] > 0.0, sn, MASK_VALUE)
            m_prev = m_ref[:, :1]
            m_new = jnp.maximum(m_prev, jnp.max(sn, axis=-1, keepdims=True))
            alpha = jnp.exp(m_prev - m_new)
            pn = jnp.exp(sn - m_new)
            l_ref[...] = jnp.broadcast_to(alpha * l_ref[:, :1] + jnp.sum(pn, axis=-1, keepdims=True), l_ref.shape)
            acc_ref[...] = alpha * acc_ref[...] + jnp.dot(pn.astype(BF16), v_ref[rows, :].astype(BF16),
                                                          preferred_element_type=F32)
            m_ref[...] = jnp.broadcast_to(m_new, m_ref.shape)

    o_ref[...] = (acc_ref[...] / l_ref[:, :1]).astype(o_ref.dtype)


def moba(pm, batch, seq, q_col, k_col, v_col):
    T = batch * seq
    H, hd, BLK = MOBA_HEADS, LANES, MOBA_BLOCK
    nb = seq // BLK
    cos_t, sin_lo, sin_hi, half = rope_tables(seq, hd)
    full = pl.BlockSpec((seq, hd), lambda b, h: (0, 0))
    kr, kmean = pl.pallas_call(
        functools.partial(_moba_k_kernel, nb=nb, rope_half=half),
        grid=(batch, H),
        in_specs=[pl.BlockSpec((seq, hd), lambda b, h: (b, k_col + h)), full, full, full],
        out_specs=[pl.BlockSpec((seq, hd), lambda b, h: (b, h)),
                   pl.BlockSpec((1, 1, nb, hd), lambda b, h: (b, h, 0, 0))],
        out_shape=[jax.ShapeDtypeStruct((T, H * hd), BF16), jax.ShapeDtypeStruct((batch, H, nb, hd), F32)],
        compiler_params=_params("parallel", "parallel"),
    )(pm, cos_t, sin_lo, sin_hi)
    tab = pl.BlockSpec((BLK, hd), lambda b, h, i: (i, 0))
    return pl.pallas_call(
        functools.partial(_moba_attn_kernel, nb=nb, rope_half=half),
        grid=(batch, H, nb),
        in_specs=[pl.BlockSpec((BLK, hd), lambda b, h, i: (b * nb + i, q_col + h)), tab, tab, tab,
                  pl.BlockSpec((seq, hd), lambda b, h, i: (b, h)),
                  pl.BlockSpec((seq, hd), lambda b, h, i: (b, v_col + h)),
                  pl.BlockSpec((1, 1, nb, hd), lambda b, h, i: (b, h, 0, 0))],
        out_specs=pl.BlockSpec((BLK, hd), lambda b, h, i: (b * nb + i, h)),
        out_shape=jax.ShapeDtypeStruct((T, H * hd), BF16),
        scratch_shapes=[pltpu.VMEM((BLK, LANES), F32), pltpu.VMEM((BLK, LANES), F32), pltpu.VMEM((BLK, hd), F32)],
        compiler_params=_params("parallel", "parallel", "arbitrary"),
    )(pm, cos_t, sin_lo, sin_hi, kr, pm, kmean)


def _gelu_erf(x):
    return 0.5 * x * (1.0 + lax.erf(x * (2.0 ** -0.5)))


def _top_values(x, count):
    vals = []
    for _ in range(count):
        m = jnp.max(x, axis=0, keepdims=True)
        vals.append(m)
        x = jnp.where(x >= m, -jnp.inf, x)
    return vals


def _peer_topk_kernel(q_ref, sk_ref, s1_ref, e1_ref, s2_ref, e2_ref, tau_ref):
    nt = (((1,), (1,)), ((), ()))
    K = PEER_TOPK
    q = q_ref[...]
    half = q.shape[-1] // 2
    s1 = lax.dot_general(sk_ref[0, 0], q[:, :half], nt, precision=lax.Precision.HIGHEST, preferred_element_type=F32)
    s2 = lax.dot_general(sk_ref[0, 1], q[:, half:], nt, precision=lax.Precision.HIGHEST, preferred_element_type=F32)
    v1 = _top_values(s1, K)
    v2 = _top_values(s2, K)
    v2_all = jnp.concatenate(v2, axis=0)
    cand = jnp.concatenate([v1[a] + v2_all for a in range(K)], axis=0)
    tau = _top_values(cand, K)[-1]
    top = v1[0] + v2[0]
    z = jnp.sum(jnp.where(cand >= tau, jnp.exp(cand - top), 0.0), axis=0, keepdims=True)
    s1_ref[0] = s1
    e1_ref[0] = jnp.exp(s1 - v1[0])
    s2_ref[0] = s2
    e2_ref[0] = jnp.exp(s2 - v2[0]) / z
    tau_ref[0] = tau


def peer_topk(q, subkeys, tt=256):
    T = q.shape[0]
    H, _, NK, dh = subkeys.shape
    assert NK == PEER_NKEYS
    spec = pl.BlockSpec((1, NK, tt), lambda i, h: (h, 0, i))
    shape = jax.ShapeDtypeStruct((H, NK, T), F32)
    return pl.pallas_call(
        _peer_topk_kernel,
        grid=(T // tt, H),
        in_specs=[pl.BlockSpec((tt, 2 * dh), lambda i, h: (i, h)),
                  pl.BlockSpec((1, 2, NK, dh), lambda i, h: (h, 0, 0, 0))],
        out_specs=[spec] * 4 + [pl.BlockSpec((1, 1, tt), lambda i, h: (h, 0, i))],
        out_shape=[shape] * 4 + [jax.ShapeDtypeStruct((H, 1, T), F32)],
        compiler_params=_params("parallel", "parallel"),
    )(q, subkeys)


def _peer_gate_kernel(ht_ref, u_ref, s1_ref, e1_ref, s2_ref, e2_ref, tau_ref, w_ref, *, rows_per_tile):
    j = pl.program_id(1)
    NK = PEER_NKEYS
    act = jnp.dot(u_ref[...], ht_ref[...], preferred_element_type=F32)
    for r in range(rows_per_tile):
        k1 = j * rows_per_tile + r
        g = None
        for h in range(PEER_HEADS):
            s1 = s1_ref[h, pl.ds(k1, 1), :]
            e1 = e1_ref[h, pl.ds(k1, 1), :]
            term = jnp.where(s1 + s2_ref[h] >= tau_ref[h], e2_ref[h], 0.0) * e1
            g = term if g is None else g + term
        w = g * _gelu_erf(act[r * NK:(r + 1) * NK, :])
        w_ref[:, r * NK:(r + 1) * NK] = w.T.astype(w_ref.dtype)


def peer_gate(ht, u, tables, tt=PEER_TOKEN_TILE, te=PEER_EXPERT_TILE):
    D, T = ht.shape
    E = u.shape[0]
    H, NK, _ = tables[0].shape
    tspec = pl.BlockSpec((H, NK, tt), lambda i, j: (0, 0, i))
    tau_spec = pl.BlockSpec((H, 1, tt), lambda i, j: (0, 0, i))
    return pl.pallas_call(
        functools.partial(_peer_gate_kernel, rows_per_tile=te // NK),
        grid=(T // tt, E // te),
        in_specs=[pl.BlockSpec((D, tt), lambda i, j: (0, i)),
                  pl.BlockSpec((te, D), lambda i, j: (j, 0)),
                  tspec, tspec, tspec, tspec, tau_spec],
        out_specs=pl.BlockSpec((tt, te), lambda i, j: (i, j)),
        out_shape=jax.ShapeDtypeStruct((T, E), BF16),
        compiler_params=_params("parallel", "arbitrary"),
    )(ht, u, *tables)


def peer_ffn(x, norm_g, wq, subkeys, u_tab, v_tab):
    h, ht = rmsnorm(x, norm_g, BF16, with_transpose=True)
    q = matmul(h, wq.astype(BF16))
    tables = peer_topk(q, subkeys)
    w = peer_gate(ht, u_tab.astype(BF16), tables)
    return matmul(w, v_tab.astype(BF16), res=x)


def _split_w_in(w_in, group_width):
    gw = group_width
    dk_all = gw // 2
    sizes = (gw, gw, dk_all, dk_all, gw, GLA_GATE_RANK, gw, gw, gw, gw, gw)
    offs = [0]
    for s in sizes:
        offs.append(offs[-1] + s)
    seg = lambda n: w_in[:, offs[n]:offs[n + 1]]
    main = jnp.concatenate([seg(n) for n in range(len(sizes)) if n != 5], axis=1).astype(BF16)
    lr = jnp.zeros((w_in.shape[0], LANES), BF16).at[:, :GLA_GATE_RANK].set(seg(5).astype(BF16))
    cols, c = {}, 0
    for name, n in zip(('lru_x', 'lru_gate', 'gla_q', 'gla_k', 'gla_v', 'gla_og', 's5_u', 'moba_q', 'moba_k', 'moba_v'),
                       (0, 1, 2, 3, 4, 6, 7, 8, 9, 10)):
        cols[name] = c // LANES
        c += sizes[n]
    return main, lr, cols


def kernel(x, norm1_g, w_in, lru_conv_w, lru_conv_b, lru_wa, lru_ba, lru_wx, lru_bx, lru_lambda, gla_wg2, gla_bg, gla_norm_g, s5_a_re, s5_a_im, s5_log_step, s5_b_re, s5_b_im, s5_c_re, s5_c_im, s5_d, s5_w_glu, s5_b_glu, w_out, norm2_g, peer_wq, peer_subkeys, peer_u, peer_v, final_norm_g):
    B, S, D = x.shape
    T = B * S
    depth = norm1_g.shape[0]
    gw = D // N_MIXERS
    x = x.reshape(T, D)
    for l in range(depth):
        w_main, w_lr, col = _split_w_in(w_in[l], gw)
        h = rmsnorm(x, norm1_g[l], BF16)
        pm = matmul(h, w_main)
        plr = matmul(h, w_lr, tk=h.shape[1])
        y_a = rglru(pm, B, S, col['lru_x'], col['lru_gate'], lru_conv_w[l], lru_conv_b[l], lru_wa[l], lru_ba[l],
                    lru_wx[l], lru_bx[l], lru_lambda[l])
        y_b = gla(pm, plr, B, S, col['gla_q'], col['gla_k'], col['gla_v'], col['gla_og'],
                  gla_wg2[l], gla_bg[l], gla_norm_g[l])
        prep = s5_prepare(s5_a_re[l], s5_a_im[l], s5_log_step[l], s5_b_re[l], s5_b_im[l], s5_c_re[l], s5_c_im[l])
        z = s5_ssm(pm, B, S, col['s5_u'], prep, s5_d[l])
        y_c = glu(z, s5_w_glu[l], s5_b_glu[l])
        y_d = moba(pm, B, S, col['moba_q'], col['moba_k'], col['moba_v'])
        mixed = jnp.concatenate([y_a, y_b, y_c, y_d], axis=-1)
        x = matmul(mixed, w_out[l].astype(BF16), res=x)
        x = peer_ffn(x, norm2_g[l], peer_wq[l], peer_subkeys[l], peer_u[l], peer_v[l])
    return rmsnorm(x, final_norm_g, F32).reshape(B, S, D)
```

```python
import functools
import math

import jax
import jax.numpy as jnp
from jax import lax
from jax.experimental import pallas as pl
from jax.experimental.pallas import tpu as pltpu

F32 = jnp.float32
BF16 = jnp.bfloat16

LANES = 128
SUBLANES = 8
VMEM_LIMIT_BYTES = 56 * 2**20

RMS_EPS = 1e-6
N_MIXERS = 4

LRU_BLOCKS = 8
LRU_CONV = 4
LRU_C = 8.0
LRU_ROWS = 256

GLA_HEADS = 4
GLA_GATE_RANK = 16
GLA_TAU = 16.0
GLA_CHUNK = 64
GLA_TILE = 512

S5_GROUP = 16
S5_STATE = 64
S5_SLAB_GROUPS = LANES // S5_GROUP
S5_SLAB_STATES = S5_SLAB_GROUPS * S5_STATE
S5_TILE = 256

MOBA_HEADS = 8
MOBA_BLOCK = 256
MOBA_TOPK = 3
ROPE_THETA = 500000.0
MASK_VALUE = -1e30

PEER_HEADS = 8
PEER_NKEYS = 128
PEER_TOPK = 16
PEER_TOKEN_TILE = 512
PEER_EXPERT_TILE = 1024
PEER_SLICE_ROWS = 2


def _params(*semantics):
    return pltpu.CompilerParams(dimension_semantics=semantics, vmem_limit_bytes=VMEM_LIMIT_BYTES)


def _rmsnorm_kernel(x_ref, g_ref, o_ref, *maybe_ot_ref):
    x = x_ref[...]
    y = x * lax.rsqrt(jnp.mean(x * x, axis=-1, keepdims=True) + RMS_EPS) * g_ref[...]
    o_ref[...] = y.astype(o_ref.dtype)
    if maybe_ot_ref:
        maybe_ot_ref[0][...] = y.T.astype(maybe_ot_ref[0].dtype)


def rmsnorm(x, g, out_dtype, with_transpose=False, tm=256):
    T, D = x.shape
    out_shape = [jax.ShapeDtypeStruct((T, D), out_dtype)]
    out_specs = [pl.BlockSpec((tm, D), lambda i: (i, 0))]
    if with_transpose:
        out_shape.append(jax.ShapeDtypeStruct((D, T), BF16))
        out_specs.append(pl.BlockSpec((D, tm), lambda i: (0, i)))
    res = pl.pallas_call(
        _rmsnorm_kernel,
        grid=(T // tm,),
        in_specs=[pl.BlockSpec((tm, D), lambda i: (i, 0)), pl.BlockSpec((1, D), lambda i: (0, 0))],
        out_specs=out_specs,
        out_shape=out_shape,
        name="rmsnorm",
        compiler_params=_params("parallel"),
    )(x, g.reshape(1, D))
    return res if with_transpose else res[0]


def _mm_kernel(x_ref, w_ref, *rest, nk, has_res):
    if has_res:
        r_ref, o_ref, acc_ref = rest
    else:
        o_ref, acc_ref = rest
    k = pl.program_id(2)

    @pl.when(k == 0)
    def _():
        acc_ref[...] = jnp.zeros_like(acc_ref)

    acc_ref[...] += jnp.dot(x_ref[...], w_ref[...], preferred_element_type=F32)

    @pl.when(k == nk - 1)
    def _():
        acc = acc_ref[...]
        if has_res:
            acc = acc + r_ref[...]
        o_ref[...] = acc.astype(o_ref.dtype)


def matmul(x, w, res=None, *, tm=1024, tn=1024, tk=2048, out_dtype=F32, name="matmul"):
    M, K = x.shape
    _, N = w.shape
    tm, tn, tk = min(tm, M), min(tn, N), min(tk, K)
    assert M % tm == 0 and N % tn == 0 and K % tk == 0
    nk = K // tk
    in_specs = [pl.BlockSpec((tm, tk), lambda i, j, k: (i, k)),
                pl.BlockSpec((tk, tn), lambda i, j, k: (k, j))]
    args = [x, w]
    if res is not None:
        in_specs.append(pl.BlockSpec((tm, tn), lambda i, j, k: (i, j)))
        args.append(res)
    return pl.pallas_call(
        functools.partial(_mm_kernel, nk=nk, has_res=res is not None),
        grid=(M // tm, N // tn, nk),
        in_specs=in_specs,
        out_specs=pl.BlockSpec((tm, tn), lambda i, j, k: (i, j)),
        out_shape=jax.ShapeDtypeStruct((M, N), out_dtype),
        scratch_shapes=[pltpu.VMEM((tm, tn), F32)],
        name=name,
        compiler_params=_params("parallel", "parallel", "arbitrary"),
    )(*args)


def _lru_kernel(x_ref, gate_ref, cw_ref, cb_ref, wa_ref, ba_ref, wx_ref, bx_ref, lam_ref, o_ref, *, seq):
    R = LRU_ROWS
    cw = cw_ref[...]
    cb = cb_ref[...]
    ba = ba_ref[...]
    bx = bx_ref[...]
    neg_c_softplus = -LRU_C * jax.nn.softplus(-lam_ref[...])
    wa = wa_ref[0]
    wx = wx_ref[0]
    row = lax.broadcasted_iota(jnp.int32, (R, LANES), 0)
    row8 = lax.broadcasted_iota(jnp.int32, (SUBLANES, LANES), 0)

    def body(c, h):
        r0 = pl.multiple_of(c * R, R)
        xt = x_ref[pl.ds(r0, R), :]
        p0 = pl.multiple_of(jnp.maximum(r0 - SUBLANES, 0), SUBLANES)
        prev = jnp.where(c > 0, x_ref[pl.ds(p0, SUBLANES), :], 0.0)
        xc = xt * cw[LRU_CONV - 1:LRU_CONV] + cb
        for d in range(1, LRU_CONV):
            rolled = pltpu.roll(xt, d, 0)
            head = jnp.where(row8 < d, pltpu.roll(prev, d, 0), rolled[:SUBLANES])
            shifted = jnp.concatenate([head, rolled[SUBLANES:]], axis=0)
            xc = xc + shifted * cw[LRU_CONV - 1 - d:LRU_CONV - d]
        xb = xc.astype(BF16)
        r = jax.nn.sigmoid(jnp.dot(xb, wa, preferred_element_type=F32) + ba)
        i = jax.nn.sigmoid(jnp.dot(xb, wx, preferred_element_type=F32) + bx)
        log_a = r * neg_c_softplus
        a = jnp.exp(log_a)
        b = jnp.sqrt(1.0 - a * a) * (i * xc)
        d = 1
        while d < R:
            keep = row >= d
            a_sh = jnp.where(keep, pltpu.roll(a, d, 0), 1.0)
            b_sh = jnp.where(keep, pltpu.roll(b, d, 0), 0.0)
            b = a * b_sh + b
            a = a * a_sh
            d *= 2
        hs = b + a * h
        o_ref[pl.ds(r0, R), :] = (hs * jax.nn.gelu(gate_ref[pl.ds(r0, R), :])).astype(o_ref.dtype)
        return hs[R - 1:R, :]

    lax.fori_loop(0, seq // R, body, jnp.zeros((1, LANES), F32))


def rglru(pm, batch, seq, x_col, gate_col, conv_w, conv_b, wa, ba, wx, bx, lam):
    T = batch * seq
    W = LRU_BLOCKS * LANES
    vec = lambda v: v.reshape(1, W)
    vspec = pl.BlockSpec((1, LANES), lambda b, n: (0, n))
    wspec = pl.BlockSpec((1, LANES, LANES), lambda b, n: (n, 0, 0))
    return pl.pallas_call(
        functools.partial(_lru_kernel, seq=seq),
        grid=(batch, LRU_BLOCKS),
        in_specs=[pl.BlockSpec((seq, LANES), lambda b, n: (b, x_col + n)),
                  pl.BlockSpec((seq, LANES), lambda b, n: (b, gate_col + n)),
                  pl.BlockSpec((LRU_CONV, LANES), lambda b, n: (0, n)),
                  vspec, wspec, vspec, wspec, vspec, vspec],
        out_specs=pl.BlockSpec((seq, LANES), lambda b, n: (b, n)),
        out_shape=jax.ShapeDtypeStruct((T, W), BF16),
        name="rglru",
        compiler_params=_params("parallel", "parallel"),
    )(pm, pm, conv_w, vec(conv_b), wa.astype(BF16), vec(ba), wx.astype(BF16), vec(bx), vec(lam))


def _gla_kernel(q_ref, k_ref, v_ref, og_ref, lr_ref, wg_ref, bg_ref, ng_ref, o_ref, st_ref):
    C = GLA_CHUNK
    dk = q_ref.shape[-1]
    dv = v_ref.shape[-1]

    @pl.when(pl.program_id(2) == 0)
    def _():
        st_ref[...] = jnp.zeros_like(st_ref)

    wg = wg_ref[0]
    bg = bg_ref[0]
    ng = ng_ref[0]
    ri = lax.broadcasted_iota(jnp.int32, (C, C), 0)
    ci = lax.broadcasted_iota(jnp.int32, (C, C), 1)
    causal = ri >= ci
    tril = causal.astype(F32)
    scale = dk ** -0.5
    nt = (((1,), (1,)), ((), ()))
    tn = (((0,), (0,)), ((), ()))

    for c in range(GLA_TILE // C):
        rows = pl.ds(c * C, C)
        q = q_ref[rows, :]
        k = k_ref[rows, :]
        v = v_ref[rows, :].astype(BF16)
        pre = jnp.dot(lr_ref[rows, :], wg, precision=lax.Precision.HIGHEST, preferred_element_type=F32) + bg
        g = jax.nn.log_sigmoid(pre) / GLA_TAU
        bc = jnp.dot(tril, g, precision=lax.Precision.HIGHEST, preferred_element_type=F32)
        b_last = bc[C - 1:C, :]
        qe = (q * scale * jnp.exp(bc)).astype(BF16)
        ke = (k * jnp.exp(-bc)).astype(BF16)
        kd = (k * jnp.exp(b_last - bc)).astype(BF16)
        att = lax.dot_general(qe, ke, nt, preferred_element_type=F32)
        att = jnp.where(causal, att, 0.0).astype(BF16)
        st = st_ref[...]
        o = jnp.dot(att, v, preferred_element_type=F32)
        o = o + lax.dot_general(qe, st.astype(BF16), nt, preferred_element_type=F32)
        st_ref[...] = st * jnp.exp(b_last) + lax.dot_general(v, kd, tn, preferred_element_type=F32)
        o = o * lax.rsqrt(jnp.mean(o * o, axis=-1, keepdims=True) + RMS_EPS)
        o_ref[rows, :] = (o * ng * jax.nn.silu(og_ref[rows, :])).astype(o_ref.dtype)


def gla(pm, plr, batch, seq, q_col, k_col, v_col, og_col, wg2, bg, norm_g):
    T = batch * seq
    H = GLA_HEADS
    dk = wg2.shape[-1] // H
    dv = norm_g.shape[-1] // H
    assert dk == LANES and dv == 2 * LANES
    nt = seq // GLA_TILE
    wg = jnp.zeros((LANES, H * dk), F32).at[:GLA_GATE_RANK].set(wg2)
    wg = wg.reshape(LANES, H, dk).transpose(1, 0, 2)
    tok = lambda b, h, t: b * nt + t
    return pl.pallas_call(
        _gla_kernel,
        grid=(batch, H, nt),
        in_specs=[pl.BlockSpec((GLA_TILE, dk), lambda b, h, t: (tok(b, h, t), q_col + h)),
                  pl.BlockSpec((GLA_TILE, dk), lambda b, h, t: (tok(b, h, t), k_col + h)),
                  pl.BlockSpec((GLA_TILE, dv), lambda b, h, t: (tok(b, h, t), v_col // 2 + h)),
                  pl.BlockSpec((GLA_TILE, dv), lambda b, h, t: (tok(b, h, t), og_col // 2 + h)),
                  pl.BlockSpec((GLA_TILE, LANES), lambda b, h, t: (tok(b, h, t), 0)),
                  pl.BlockSpec((1, LANES, dk), lambda b, h, t: (h, 0, 0)),
                  pl.BlockSpec((1, 1, dk), lambda b, h, t: (h, 0, 0)),
                  pl.BlockSpec((1, 1, dv), lambda b, h, t: (h, 0, 0))],
        out_specs=pl.BlockSpec((GLA_TILE, dv), lambda b, h, t: (tok(b, h, t), h)),
        out_shape=jax.ShapeDtypeStruct((T, H * dv), BF16),
        scratch_shapes=[pltpu.VMEM((dv, dk), F32)],
        name="gla",
        compiler_params=_params("parallel", "parallel", "arbitrary"),
    )(pm, pm, pm, pm, plr, wg, bg.reshape(H, 1, dk), norm_g.reshape(H, 1, dv))


def _s5_kernel(u_ref, bcat_ref, ccat_ref, pr_ref, pi_ref, d_ref, z_ref, x_ref, carry_ref):
    TT = S5_TILE
    NS = 2 * S5_SLAB_STATES

    @pl.when(pl.program_id(2) == 0)
    def _():
        carry_ref[...] = jnp.zeros_like(carry_ref)

    u = u_ref[...]
    pr = pr_ref[0]
    pi = pi_ref[0]

    def swap(t):
        return pltpu.roll(t, S5_SLAB_STATES, 1)

    x = jnp.dot(u.astype(BF16), bcat_ref[0], preferred_element_type=F32)
    sub = lax.broadcasted_iota(jnp.int32, (TT, NS), 0) % SUBLANES
    for d in (1, 2, 4):
        sh = jnp.where(sub >= d, pltpu.roll(x, d, 0), 0.0)
        x = x + pr[d - 1:d] * sh + pi[d - 1:d] * swap(sh)
    x_ref[...] = x

    def carry_step(i, carry):
        rows = pl.ds(pl.multiple_of(i * SUBLANES, SUBLANES), SUBLANES)
        cb = jnp.broadcast_to(carry, (SUBLANES, NS))
        blk = x_ref[rows, :] + pr * cb + pi * swap(cb)
        x_ref[rows, :] = blk
        return blk[SUBLANES - 1:SUBLANES, :]

    carry_ref[...] = lax.fori_loop(0, TT // SUBLANES, carry_step, carry_ref[...])
    y = jnp.dot(x_ref[...].astype(BF16), ccat_ref[0], preferred_element_type=F32) + d_ref[...] * u
    z_ref[...] = jax.nn.gelu(y)


def s5_prepare(a_re, a_im, log_step, b_re, b_im, c_re, c_im):
    G, P = a_re.shape
    H = S5_GROUP
    ns = G // S5_SLAB_GROUPS
    step = jnp.exp(log_step)[:, None]
    mag = jnp.exp(a_re * step)
    ang = a_im * step
    abar_re, abar_im = mag * jnp.cos(ang), mag * jnp.sin(ang)
    den = a_re * a_re + a_im * a_im
    f_re = ((abar_re - 1.0) * a_re + abar_im * a_im) / den
    f_im = (abar_im * a_re - (abar_re - 1.0) * a_im) / den
    bb_re = f_re[..., None] * b_re - f_im[..., None] * b_im
    bb_im = f_re[..., None] * b_im + f_im[..., None] * b_re
    eye = jnp.eye(S5_SLAB_GROUPS, dtype=F32)

    def in_slab(bb):
        t = bb.reshape(ns, S5_SLAB_GROUPS, P, H)
        return jnp.einsum('sgph,gk->sghkp', t, eye).reshape(ns, LANES, S5_SLAB_STATES)

    def out_slab(cc):
        t = cc.reshape(ns, S5_SLAB_GROUPS, H, P)
        return jnp.einsum('sghp,gk->sgpkh', t, eye).reshape(ns, S5_SLAB_STATES, LANES)

    bcat = jnp.concatenate([in_slab(bb_re), in_slab(bb_im)], axis=-1).astype(BF16)
    ccat = jnp.concatenate([out_slab(c_re), -out_slab(c_im)], axis=1).astype(BF16)
    pw_re, pw_im = [abar_re], [abar_im]
    for _ in range(SUBLANES - 1):
        r, i = pw_re[-1], pw_im[-1]
        pw_re.append(r * abar_re - i * abar_im)
        pw_im.append(r * abar_im + i * abar_re)
    slab = lambda t: jnp.stack(t, 0).reshape(SUBLANES, ns, S5_SLAB_STATES).transpose(1, 0, 2)
    pr, pi = slab(pw_re), slab(pw_im)
    return bcat, ccat, jnp.concatenate([pr, pr], -1), jnp.concatenate([-pi, pi], -1)


def s5_ssm(pm, batch, seq, u_col, prep, d):
    T = batch * seq
    bcat, ccat, pr, pi = prep
    ns = bcat.shape[0]
    NS = 2 * S5_SLAB_STATES
    nt = seq // S5_TILE
    return pl.pallas_call(
        _s5_kernel,
        grid=(batch, ns, nt),
        in_specs=[pl.BlockSpec((S5_TILE, LANES), lambda b, s, t: (b * nt + t, u_col + s)),
                  pl.BlockSpec((1, LANES, NS), lambda b, s, t: (s, 0, 0)),
                  pl.BlockSpec((1, NS, LANES), lambda b, s, t: (s, 0, 0)),
                  pl.BlockSpec((1, SUBLANES, NS), lambda b, s, t: (s, 0, 0)),
                  pl.BlockSpec((1, SUBLANES, NS), lambda b, s, t: (s, 0, 0)),
                  pl.BlockSpec((1, LANES), lambda b, s, t: (0, s))],
        out_specs=pl.BlockSpec((S5_TILE, LANES), lambda b, s, t: (b * nt + t, s)),
        out_shape=jax.ShapeDtypeStruct((T, ns * LANES), F32),
        scratch_shapes=[pltpu.VMEM((S5_TILE, NS), F32), pltpu.VMEM((1, NS), F32)],
        name="s5_ssm",
        compiler_params=_params("parallel", "parallel", "arbitrary"),
    )(pm, bcat, ccat, pr, pi, d.reshape(1, -1))


def _glu_kernel(z_ref, w_ref, b_ref, o_ref):
    z = z_ref[...]
    y = jnp.dot(z.astype(BF16), w_ref[...], preferred_element_type=F32) + b_ref[...]
    o_ref[...] = (z * jax.nn.sigmoid(y)).astype(o_ref.dtype)


def glu(z, w, b, tm=512):
    T, W = z.shape
    return pl.pallas_call(
        _glu_kernel,
        grid=(T // tm,),
        in_specs=[pl.BlockSpec((tm, W), lambda i: (i, 0)),
                  pl.BlockSpec((W, W), lambda i: (0, 0)),
                  pl.BlockSpec((1, W), lambda i: (0, 0))],
        out_specs=pl.BlockSpec((tm, W), lambda i: (i, 0)),
        out_shape=jax.ShapeDtypeStruct((T, W), BF16),
        name="s5_glu",
        compiler_params=_params("parallel"),
    )(z, w.astype(BF16), b.reshape(1, W))


def _rope(t, cos, sin_lo, sin_hi, rope_half):
    return (t * cos + pltpu.roll(t, LANES - rope_half, 1) * sin_lo + pltpu.roll(t, rope_half, 1) * sin_hi)


def rope_tables(seq, head_dim):
    rope_dims = head_dim // 4
    half = rope_dims // 2
    inv = jnp.power(ROPE_THETA, -jnp.arange(half, dtype=F32) / half)
    ang = jnp.arange(seq).astype(F32)[:, None] * inv[None, :]
    cos, sin = jnp.cos(ang), jnp.sin(ang)
    pad = head_dim - rope_dims
    zeros = jnp.zeros((seq, half), F32)
    cos_t = jnp.concatenate([cos, cos, jnp.ones((seq, pad), F32)], axis=-1)
    sin_lo = jnp.concatenate([-sin, zeros, jnp.zeros((seq, pad), F32)], axis=-1)
    sin_hi = jnp.concatenate([zeros, sin, jnp.zeros((seq, pad), F32)], axis=-1)
    return cos_t, sin_lo, sin_hi, half


def _moba_kv_kernel(k_ref, v_ref, cos_ref, slo_ref, shi_ref, kr_ref, km_ref, vt_ref, *, nb, rope_half):
    for n in range(nb):
        rows = pl.ds(n * MOBA_BLOCK, MOBA_BLOCK)
        kr = _rope(k_ref[rows, :], cos_ref[rows, :], slo_ref[rows, :], shi_ref[rows, :], rope_half)
        kr_ref[rows, :] = kr.astype(kr_ref.dtype)
        km_ref[0, 0, n:n + 1, :] = jnp.mean(kr, axis=0, keepdims=True)
        vt_ref[0, 0, n] = v_ref[rows, :].T.astype(vt_ref.dtype)


def _moba_attn_kernel(q_ref, cos_ref, slo_ref, shi_ref, kr_ref, vt_ref, km_ref, o_ref, sel_ref, *, nb, rope_half):
    BLK = MOBA_BLOCK
    i = pl.program_id(2)
    hd = q_ref.shape[-1]
    log2_scale = hd ** -0.5 * math.log2(math.e)
    qt = _rope(q_ref[...], cos_ref[...], slo_ref[...], shi_ref[...], rope_half).T
    gate = jnp.dot(km_ref[0, 0], qt, precision=lax.Precision.HIGHEST, preferred_element_type=F32)
    blk_id = lax.broadcasted_iota(jnp.int32, (nb, BLK), 0)
    rank = jnp.zeros((nb, BLK), jnp.int32)
    for m in range(nb):
        gm = gate[m:m + 1, :]
        beats = (gm > gate) | ((gm == gate) & (m < blk_id))
        rank = rank + jnp.where(beats & (m < i), 1, 0)
    sel_ref[...] = ((blk_id < i) & (rank < MOBA_TOPK)).astype(F32)
    qb = (qt * log2_scale).astype(BF16)

    own = pl.ds(pl.multiple_of(i * BLK, BLK), BLK)
    s = jnp.dot(kr_ref[own, :], qb, preferred_element_type=F32)
    key = lax.broadcasted_iota(jnp.int32, (BLK, BLK), 0)
    qry = lax.broadcasted_iota(jnp.int32, (BLK, BLK), 1)
    s = jnp.where(key <= qry, s, MASK_VALUE)
    m0 = jnp.max(s, axis=0, keepdims=True)
    p = jnp.exp2(s - m0)
    l0 = jnp.sum(p, axis=0, keepdims=True)
    acc0 = jnp.dot(vt_ref[0, 0, i], p.astype(BF16), preferred_element_type=F32)

    def body(n, carry):
        m_prev, l_prev, acc = carry
        rows = pl.ds(pl.multiple_of(n * BLK, BLK), BLK)
        sn = jnp.dot(kr_ref[rows, :], qb, preferred_element_type=F32)
        sn = jnp.where(sel_ref[pl.ds(n, 1), :] > 0.0, sn, MASK_VALUE)
        m_new = jnp.maximum(m_prev, jnp.max(sn, axis=0, keepdims=True))
        alpha = jnp.exp2(m_prev - m_new)
        pn = jnp.exp2(sn - m_new)
        l_new = alpha * l_prev + jnp.sum(pn, axis=0, keepdims=True)
        acc = alpha * acc + jnp.dot(vt_ref[0, 0, n], pn.astype(BF16), preferred_element_type=F32)
        return m_new, l_new, acc

    _, l, acc = lax.fori_loop(0, i, body, (m0, l0, acc0))
    o_ref[...] = (acc / l).T.astype(o_ref.dtype)


def moba(pm, batch, seq, q_col, k_col, v_col):
    T = batch * seq
    H, hd, BLK = MOBA_HEADS, LANES, MOBA_BLOCK
    nb = seq // BLK
    cos_t, sin_lo, sin_hi, half = rope_tables(seq, hd)
    full = pl.BlockSpec((seq, hd), lambda b, h: (0, 0))
    kr, kmean, vt = pl.pallas_call(
        functools.partial(_moba_kv_kernel, nb=nb, rope_half=half),
        grid=(batch, H),
        in_specs=[pl.BlockSpec((seq, hd), lambda b, h: (b, k_col + h)),
                  pl.BlockSpec((seq, hd), lambda b, h: (b, v_col + h)), full, full, full],
        out_specs=[pl.BlockSpec((seq, hd), lambda b, h: (b, h)),
                   pl.BlockSpec((1, 1, nb, hd), lambda b, h: (b, h, 0, 0)),
                   pl.BlockSpec((1, 1, nb, hd, BLK), lambda b, h: (b, h, 0, 0, 0))],
        out_shape=[jax.ShapeDtypeStruct((T, H * hd), BF16), jax.ShapeDtypeStruct((batch, H, nb, hd), F32),
                   jax.ShapeDtypeStruct((batch, H, nb, hd, BLK), BF16)],
        name="moba_kv",
        compiler_params=_params("parallel", "parallel"),
    )(pm, pm, cos_t, sin_lo, sin_hi)
    tab = pl.BlockSpec((BLK, hd), lambda b, h, i: (i, 0))
    return pl.pallas_call(
        functools.partial(_moba_attn_kernel, nb=nb, rope_half=half),
        grid=(batch, H, nb),
        in_specs=[pl.BlockSpec((BLK, hd), lambda b, h, i: (b * nb + i, q_col + h)), tab, tab, tab,
                  pl.BlockSpec((seq, hd), lambda b, h, i: (b, h)),
                  pl.BlockSpec((1, 1, nb, hd, BLK), lambda b, h, i: (b, h, 0, 0, 0)),
                  pl.BlockSpec((1, 1, nb, hd), lambda b, h, i: (b, h, 0, 0))],
        out_specs=pl.BlockSpec((BLK, hd), lambda b, h, i: (b * nb + i, h)),
        out_shape=jax.ShapeDtypeStruct((T, H * hd), BF16),
        scratch_shapes=[pltpu.VMEM((nb, BLK), F32)],
        name="moba_attn",
        compiler_params=_params("parallel", "parallel", "arbitrary"),
    )(pm, cos_t, sin_lo, sin_hi, kr, vt, kmean)


def _gelu_erf(x):
    return 0.5 * x * (1.0 + lax.erf(x * (2.0 ** -0.5)))


def _top_values(x, count):
    vals = []
    for _ in range(count):
        m = jnp.max(x, axis=0, keepdims=True)
        vals.append(m)
        x = jnp.where(x >= m, -jnp.inf, x)
    return vals


def _peer_topk_kernel(q_ref, sk_ref, s1_ref, e1_ref, s2_ref, e2_ref, tau_ref):
    nt = (((1,), (1,)), ((), ()))
    K = PEER_TOPK
    q = q_ref[...]
    half = q.shape[-1] // 2
    s1 = lax.dot_general(sk_ref[0, 0], q[:, :half], nt, precision=lax.Precision.HIGHEST, preferred_element_type=F32)
    s2 = lax.dot_general(sk_ref[0, 1], q[:, half:], nt, precision=lax.Precision.HIGHEST, preferred_element_type=F32)
    v1 = _top_values(s1, K)
    v2 = _top_values(s2, K)
    v2_all = jnp.concatenate(v2, axis=0)
    cand = jnp.concatenate([v1[a] + v2_all for a in range(K)], axis=0)
    tau = _top_values(cand, K)[-1]
    top = v1[0] + v2[0]
    z = jnp.sum(jnp.where(cand >= tau, jnp.exp(cand - top), 0.0), axis=0, keepdims=True)
    s1_ref[0] = s1
    e1_ref[0] = jnp.exp(s1 - v1[0])
    s2_ref[0] = s2
    e2_ref[0] = jnp.exp(s2 - v2[0]) / z
    tau_ref[0] = tau


def peer_topk(q, subkeys, tt=256):
    T = q.shape[0]
    H, _, NK, dh = subkeys.shape
    assert NK == PEER_NKEYS
    spec = pl.BlockSpec((1, NK, tt), lambda i, h: (h, 0, i))
    shape = jax.ShapeDtypeStruct((H, NK, T), F32)
    return pl.pallas_call(
        _peer_topk_kernel,
        grid=(T // tt, H),
        in_specs=[pl.BlockSpec((tt, 2 * dh), lambda i, h: (i, h)),
                  pl.BlockSpec((1, 2, NK, dh), lambda i, h: (h, 0, 0, 0))],
        out_specs=[spec] * 4 + [pl.BlockSpec((1, 1, tt), lambda i, h: (h, 0, i))],
        out_shape=[shape] * 4 + [jax.ShapeDtypeStruct((H, 1, T), F32)],
        name="peer_topk",
        compiler_params=_params("parallel", "parallel"),
    )(q, subkeys)


def _peer_gate_kernel(ht_ref, u_ref, s1_ref, e1_ref, s2_ref, e2_ref, tau_ref, w_ref, *, rows_per_tile):
    j = pl.program_id(1)
    NK = PEER_NKEYS
    for s in range(rows_per_tile // PEER_SLICE_ROWS):
        gates = []
        for r in range(s * PEER_SLICE_ROWS, (s + 1) * PEER_SLICE_ROWS):
            k1 = j * rows_per_tile + r
            g = None
            for h in range(PEER_HEADS):
                s1 = s1_ref[h, pl.ds(k1, 1), :]
                e1 = e1_ref[h, pl.ds(k1, 1), :]
                term = jnp.where(s1 + s2_ref[h] >= tau_ref[h], e2_ref[h], 0.0) * e1
                g = term if g is None else g + term
            gates.append(g)
        lo = s * PEER_SLICE_ROWS * NK
        act = jnp.dot(u_ref[lo:lo + PEER_SLICE_ROWS * NK, :], ht_ref[...], preferred_element_type=F32)
        for r in range(PEER_SLICE_ROWS):
            w = gates[r] * _gelu_erf(act[r * NK:(r + 1) * NK, :])
            w_ref[:, lo + r * NK:lo + (r + 1) * NK] = w.T.astype(w_ref.dtype)


def peer_gate(ht, u, tables, tt=PEER_TOKEN_TILE, te=PEER_EXPERT_TILE):
    D, T = ht.shape
    E = u.shape[0]
    H, NK, _ = tables[0].shape
    tspec = pl.BlockSpec((H, NK, tt), lambda i, j: (0, 0, i))
    tau_spec = pl.BlockSpec((H, 1, tt), lambda i, j: (0, 0, i))
    return pl.pallas_call(
        functools.partial(_peer_gate_kernel, rows_per_tile=te // NK),
        grid=(T // tt, E // te),
        in_specs=[pl.BlockSpec((D, tt), lambda i, j: (0, i)),
                  pl.BlockSpec((te, D), lambda i, j: (j, 0)),
                  tspec, tspec, tspec, tspec, tau_spec],
        out_specs=pl.BlockSpec((tt, te), lambda i, j: (i, j)),
        out_shape=jax.ShapeDtypeStruct((T, E), BF16),
        name="peer_gate",
        compiler_params=_params("parallel", "arbitrary"),
    )(ht, u, *tables)


def peer_ffn(x, norm_g, wq, subkeys, u_tab, v_tab):
    h, ht = rmsnorm(x, norm_g, BF16, with_transpose=True)
    q = matmul(h, wq.astype(BF16), name="peer_query")
    tables = peer_topk(q, subkeys)
    w = peer_gate(ht, u_tab.astype(BF16), tables)
    return matmul(w, v_tab.astype(BF16), res=x, name="peer_out")


def _split_w_in(w_in, group_width):
    gw = group_width
    dk_all = gw // 2
    sizes = (gw, gw, dk_all, dk_all, gw, GLA_GATE_RANK, gw, gw, gw, gw, gw)
    offs = [0]
    for s in sizes:
        offs.append(offs[-1] + s)
    seg = lambda n: w_in[:, offs[n]:offs[n + 1]]
    main = jnp.concatenate([seg(n) for n in range(len(sizes)) if n != 5], axis=1).astype(BF16)
    lr = jnp.zeros((w_in.shape[0], LANES), BF16).at[:, :GLA_GATE_RANK].set(seg(5).astype(BF16))
    cols, c = {}, 0
    for name, n in zip(('lru_x', 'lru_gate', 'gla_q', 'gla_k', 'gla_v', 'gla_og', 's5_u', 'moba_q', 'moba_k', 'moba_v'),
                       (0, 1, 2, 3, 4, 6, 7, 8, 9, 10)):
        cols[name] = c // LANES
        c += sizes[n]
    return main, lr, cols


def kernel(x, norm1_g, w_in, lru_conv_w, lru_conv_b, lru_wa, lru_ba, lru_wx, lru_bx, lru_lambda, gla_wg2, gla_bg, gla_norm_g, s5_a_re, s5_a_im, s5_log_step, s5_b_re, s5_b_im, s5_c_re, s5_c_im, s5_d, s5_w_glu, s5_b_glu, w_out, norm2_g, peer_wq, peer_subkeys, peer_u, peer_v, final_norm_g):
    B, S, D = x.shape
    T = B * S
    depth = norm1_g.shape[0]
    gw = D // N_MIXERS
    x = x.reshape(T, D)
    for l in range(depth):
        w_main, w_lr, col = _split_w_in(w_in[l], gw)
        h = rmsnorm(x, norm1_g[l], BF16)
        pm = matmul(h, w_main, name="in_proj")
        plr = matmul(h, w_lr, tk=h.shape[1], name="in_proj_lr")
        y_a = rglru(pm, B, S, col['lru_x'], col['lru_gate'], lru_conv_w[l], lru_conv_b[l], lru_wa[l], lru_ba[l],
                    lru_wx[l], lru_bx[l], lru_lambda[l])
        y_b = gla(pm, plr, B, S, col['gla_q'], col['gla_k'], col['gla_v'], col['gla_og'],
                  gla_wg2[l], gla_bg[l], gla_norm_g[l])
        prep = s5_prepare(s5_a_re[l], s5_a_im[l], s5_log_step[l], s5_b_re[l], s5_b_im[l], s5_c_re[l], s5_c_im[l])
        z = s5_ssm(pm, B, S, col['s5_u'], prep, s5_d[l])
        y_c = glu(z, s5_w_glu[l], s5_b_glu[l])
        y_d = moba(pm, B, S, col['moba_q'], col['moba_k'], col['moba_v'])
        mixed = jnp.concatenate([y_a, y_b, y_c, y_d], axis=-1)
        x = matmul(mixed, w_out[l].astype(BF16), res=x, name="out_proj")
        x = peer_ffn(x, norm2_g[l], peer_wq[l], peer_subkeys[l], peer_u[l], peer_v[l])
    return rmsnorm(x, final_norm_g, F32).reshape(B, S, D)
```

```python
import functools
import math

import jax
import jax.numpy as jnp
from jax import lax
from jax.experimental import pallas as pl
from jax.experimental.pallas import tpu as pltpu

F32 = jnp.float32
BF16 = jnp.bfloat16

LANES = 128
SUBLANES = 8
VMEM_LIMIT_BYTES = 56 * 2**20

RMS_EPS = 1e-6
N_MIXERS = 4

LRU_BLOCKS = 8
LRU_CONV = 4
LRU_C = 8.0
LRU_ROWS = 256

GLA_HEADS = 4
GLA_GATE_RANK = 16
GLA_TAU = 16.0
GLA_CHUNK = 64
GLA_TILE = 512

S5_GROUP = 16
S5_STATE = 64
S5_SLAB_GROUPS = LANES // S5_GROUP
S5_SLAB_STATES = S5_SLAB_GROUPS * S5_STATE
S5_TILE = 256

MOBA_HEADS = 8
MOBA_BLOCK = 256
MOBA_TOPK = 3
MOBA_GROUP = 4
ROPE_THETA = 500000.0
MASK_VALUE = -1e30

PEER_HEADS = 8
PEER_NKEYS = 128
PEER_TOPK = 16
PEER_TOKEN_TILE = 512
PEER_EXPERT_TILE = 1024
PEER_SLICE_ROWS = 2


def _params(*semantics):
    return pltpu.CompilerParams(dimension_semantics=semantics, vmem_limit_bytes=VMEM_LIMIT_BYTES)


def _rmsnorm_kernel(x_ref, g_ref, o_ref, *maybe_ot_ref):
    x = x_ref[...]
    y = x * lax.rsqrt(jnp.mean(x * x, axis=-1, keepdims=True) + RMS_EPS) * g_ref[...]
    o_ref[...] = y.astype(o_ref.dtype)
    if maybe_ot_ref:
        maybe_ot_ref[0][...] = y.T.astype(maybe_ot_ref[0].dtype)


def rmsnorm(x, g, out_dtype, with_transpose=False, tm=256):
    T, D = x.shape
    out_shape = [jax.ShapeDtypeStruct((T, D), out_dtype)]
    out_specs = [pl.BlockSpec((tm, D), lambda i: (i, 0))]
    if with_transpose:
        out_shape.append(jax.ShapeDtypeStruct((D, T), BF16))
        out_specs.append(pl.BlockSpec((D, tm), lambda i: (0, i)))
    res = pl.pallas_call(
        _rmsnorm_kernel,
        grid=(T // tm,),
        in_specs=[pl.BlockSpec((tm, D), lambda i: (i, 0)), pl.BlockSpec((1, D), lambda i: (0, 0))],
        out_specs=out_specs,
        out_shape=out_shape,
        name="rmsnorm",
        compiler_params=_params("parallel"),
    )(x, g.reshape(1, D))
    return res if with_transpose else res[0]


def _mm_kernel(x_ref, w_ref, *rest, nk, has_res):
    if has_res:
        r_ref, o_ref, acc_ref = rest
    else:
        o_ref, acc_ref = rest
    k = pl.program_id(2)

    @pl.when(k == 0)
    def _():
        acc_ref[...] = jnp.zeros_like(acc_ref)

    acc_ref[...] += jnp.dot(x_ref[...], w_ref[...].astype(BF16), preferred_element_type=F32)

    @pl.when(k == nk - 1)
    def _():
        acc = acc_ref[...]
        if has_res:
            acc = acc + r_ref[...]
        o_ref[...] = acc.astype(o_ref.dtype)


def matmul(x, w, res=None, *, layer=None, n_cols=None, tm=1024, tn=1024, tk=2048, res_buffers=2,
           out_dtype=F32, name="matmul"):
    M, K = x.shape
    N = n_cols if n_cols is not None else w.shape[-1]
    tm, tn, tk = min(tm, M), min(tn, N), min(tk, K)
    assert M % tm == 0 and N % tn == 0 and K % tk == 0 and w.shape[-2] == K
    nk = K // tk
    if layer is None:
        wspec = pl.BlockSpec((tk, tn), lambda i, j, k: (k, j))
    else:
        wspec = pl.BlockSpec((None, tk, tn), lambda i, j, k: (layer, k, j))
    in_specs = [pl.BlockSpec((tm, tk), lambda i, j, k: (i, k)), wspec]
    args = [x, w]
    if res is not None:
        in_specs.append(pl.BlockSpec((tm, tn), lambda i, j, k: (i, j), pipeline_mode=pl.Buffered(res_buffers)))
        args.append(res)
    return pl.pallas_call(
        functools.partial(_mm_kernel, nk=nk, has_res=res is not None),
        grid=(M // tm, N // tn, nk),
        in_specs=in_specs,
        out_specs=pl.BlockSpec((tm, tn), lambda i, j, k: (i, j)),
        out_shape=jax.ShapeDtypeStruct((M, N), out_dtype),
        scratch_shapes=[pltpu.VMEM((tm, tn), F32)],
        name=name,
        compiler_params=_params("parallel", "parallel", "arbitrary"),
    )(*args)


def _lru_kernel(x_ref, gate_ref, cw_ref, cb_ref, wa_ref, ba_ref, wx_ref, bx_ref, lam_ref, o_ref, *, seq):
    R = LRU_ROWS
    cw = cw_ref[...]
    cb = cb_ref[...]
    ba = ba_ref[...]
    bx = bx_ref[...]
    neg_c_softplus = -LRU_C * jax.nn.softplus(-lam_ref[...])
    wa = wa_ref[0]
    wx = wx_ref[0]
    row = lax.broadcasted_iota(jnp.int32, (R, LANES), 0)
    row8 = lax.broadcasted_iota(jnp.int32, (SUBLANES, LANES), 0)

    def body(c, h):
        r0 = pl.multiple_of(c * R, R)
        xt = x_ref[pl.ds(r0, R), :]
        p0 = pl.multiple_of(jnp.maximum(r0 - SUBLANES, 0), SUBLANES)
        prev = jnp.where(c > 0, x_ref[pl.ds(p0, SUBLANES), :], 0.0)
        xc = xt * cw[LRU_CONV - 1:LRU_CONV] + cb
        for d in range(1, LRU_CONV):
            rolled = pltpu.roll(xt, d, 0)
            head = jnp.where(row8 < d, pltpu.roll(prev, d, 0), rolled[:SUBLANES])
            shifted = jnp.concatenate([head, rolled[SUBLANES:]], axis=0)
            xc = xc + shifted * cw[LRU_CONV - 1 - d:LRU_CONV - d]
        xb = xc.astype(BF16)
        r = jax.nn.sigmoid(jnp.dot(xb, wa, preferred_element_type=F32) + ba)
        i = jax.nn.sigmoid(jnp.dot(xb, wx, preferred_element_type=F32) + bx)
        log_a = r * neg_c_softplus
        a = jnp.exp(log_a)
        b = jnp.sqrt(1.0 - a * a) * (i * xc)
        d = 1
        while d < R:
            keep = row >= d
            a_sh = jnp.where(keep, pltpu.roll(a, d, 0), 1.0)
            b_sh = jnp.where(keep, pltpu.roll(b, d, 0), 0.0)
            b = a * b_sh + b
            a = a * a_sh
            d *= 2
        hs = b + a * h
        o_ref[pl.ds(r0, R), :] = (hs * jax.nn.gelu(gate_ref[pl.ds(r0, R), :])).astype(o_ref.dtype)
        return hs[R - 1:R, :]

    lax.fori_loop(0, seq // R, body, jnp.zeros((1, LANES), F32))


def rglru(pm, batch, seq, x_col, gate_col, conv_w, conv_b, wa, ba, wx, bx, lam):
    T = batch * seq
    W = LRU_BLOCKS * LANES
    vec = lambda v: v.reshape(1, W)
    vspec = pl.BlockSpec((1, LANES), lambda b, n: (0, n))
    wspec = pl.BlockSpec((1, LANES, LANES), lambda b, n: (n, 0, 0))
    return pl.pallas_call(
        functools.partial(_lru_kernel, seq=seq),
        grid=(batch, LRU_BLOCKS),
        in_specs=[pl.BlockSpec((seq, LANES), lambda b, n: (b, x_col + n)),
                  pl.BlockSpec((seq, LANES), lambda b, n: (b, gate_col + n)),
                  pl.BlockSpec((LRU_CONV, LANES), lambda b, n: (0, n)),
                  vspec, wspec, vspec, wspec, vspec, vspec],
        out_specs=pl.BlockSpec((seq, LANES), lambda b, n: (b, n)),
        out_shape=jax.ShapeDtypeStruct((T, W), BF16),
        name="rglru",
        compiler_params=_params("parallel", "parallel"),
    )(pm, pm, conv_w, vec(conv_b), wa.astype(BF16), vec(ba), wx.astype(BF16), vec(bx), vec(lam))


def _gla_kernel(q_ref, k_ref, v_ref, og_ref, lr_ref, wg_ref, bg_ref, ng_ref, o_ref, st_ref):
    C = GLA_CHUNK
    dk = q_ref.shape[-1]
    dv = v_ref.shape[-1]

    @pl.when(pl.program_id(2) == 0)
    def _():
        st_ref[...] = jnp.zeros_like(st_ref)

    wg = wg_ref[0]
    bg = bg_ref[0]
    ng = ng_ref[0]
    ri = lax.broadcasted_iota(jnp.int32, (C, C), 0)
    ci = lax.broadcasted_iota(jnp.int32, (C, C), 1)
    causal = ri >= ci
    tril = causal.astype(F32)
    scale = dk ** -0.5
    nt = (((1,), (1,)), ((), ()))
    tn = (((0,), (0,)), ((), ()))

    for c in range(GLA_TILE // C):
        rows = pl.ds(c * C, C)
        q = q_ref[rows, :]
        k = k_ref[rows, :]
        v = v_ref[rows, :].astype(BF16)
        pre = jnp.dot(lr_ref[rows, :], wg, precision=lax.Precision.HIGHEST, preferred_element_type=F32) + bg
        g = jax.nn.log_sigmoid(pre) / GLA_TAU
        bc = jnp.dot(tril, g, precision=lax.Precision.HIGHEST, preferred_element_type=F32)
        b_last = bc[C - 1:C, :]
        qe = (q * scale * jnp.exp(bc)).astype(BF16)
        ke = (k * jnp.exp(-bc)).astype(BF16)
        kd = (k * jnp.exp(b_last - bc)).astype(BF16)
        att = lax.dot_general(qe, ke, nt, preferred_element_type=F32)
        att = jnp.where(causal, att, 0.0).astype(BF16)
        st = st_ref[...]
        o = jnp.dot(att, v, preferred_element_type=F32)
        o = o + lax.dot_general(qe, st.astype(BF16), nt, preferred_element_type=F32)
        st_ref[...] = st * jnp.exp(b_last) + lax.dot_general(v, kd, tn, preferred_element_type=F32)
        o = o * lax.rsqrt(jnp.mean(o * o, axis=-1, keepdims=True) + RMS_EPS)
        o_ref[rows, :] = (o * ng * jax.nn.silu(og_ref[rows, :])).astype(o_ref.dtype)


def gla(pm, pm_og, plr, batch, seq, q_col, k_col, v_col, og_col, wg2, bg, norm_g):
    T = batch * seq
    H = GLA_HEADS
    dk = wg2.shape[-1] // H
    dv = norm_g.shape[-1] // H
    assert dk == LANES and dv == 2 * LANES
    nt = seq // GLA_TILE
    wg = jnp.zeros((LANES, H * dk), F32).at[:GLA_GATE_RANK].set(wg2)
    wg = wg.reshape(LANES, H, dk).transpose(1, 0, 2)
    tok = lambda b, h, t: b * nt + t
    return pl.pallas_call(
        _gla_kernel,
        grid=(batch, H, nt),
        in_specs=[pl.BlockSpec((GLA_TILE, dk), lambda b, h, t: (tok(b, h, t), q_col + h)),
                  pl.BlockSpec((GLA_TILE, dk), lambda b, h, t: (tok(b, h, t), k_col + h)),
                  pl.BlockSpec((GLA_TILE, dv), lambda b, h, t: (tok(b, h, t), v_col // 2 + h)),
                  pl.BlockSpec((GLA_TILE, dv), lambda b, h, t: (tok(b, h, t), og_col // 2 + h)),
                  pl.BlockSpec((GLA_TILE, LANES), lambda b, h, t: (tok(b, h, t), 0)),
                  pl.BlockSpec((1, LANES, dk), lambda b, h, t: (h, 0, 0)),
                  pl.BlockSpec((1, 1, dk), lambda b, h, t: (h, 0, 0)),
                  pl.BlockSpec((1, 1, dv), lambda b, h, t: (h, 0, 0))],
        out_specs=pl.BlockSpec((GLA_TILE, dv), lambda b, h, t: (tok(b, h, t), h)),
        out_shape=jax.ShapeDtypeStruct((T, H * dv), BF16),
        scratch_shapes=[pltpu.VMEM((dv, dk), F32)],
        name="gla",
        compiler_params=_params("parallel", "parallel", "arbitrary"),
    )(pm, pm, pm, pm_og, plr, wg, bg.reshape(H, 1, dk), norm_g.reshape(H, 1, dv))


def _s5_kernel(u_ref, bcat_ref, ccat_ref, pr_ref, pi_ref, d_ref, z_ref, x_ref, carry_ref):
    TT = S5_TILE
    NS = 2 * S5_SLAB_STATES

    @pl.when(pl.program_id(2) == 0)
    def _():
        carry_ref[...] = jnp.zeros_like(carry_ref)

    u = u_ref[...]
    pr = pr_ref[0]
    pi = pi_ref[0]

    def swap(t):
        return pltpu.roll(t, S5_SLAB_STATES, 1)

    x = jnp.dot(u.astype(BF16), bcat_ref[0], preferred_element_type=F32)
    sub = lax.broadcasted_iota(jnp.int32, (TT, NS), 0) % SUBLANES
    for d in (1, 2, 4):
        sh = jnp.where(sub >= d, pltpu.roll(x, d, 0), 0.0)
        x = x + pr[d - 1:d] * sh + pi[d - 1:d] * swap(sh)
    x_ref[...] = x

    def carry_step(i, carry):
        rows = pl.ds(pl.multiple_of(i * SUBLANES, SUBLANES), SUBLANES)
        cb = jnp.broadcast_to(carry, (SUBLANES, NS))
        blk = x_ref[rows, :] + pr * cb + pi * swap(cb)
        x_ref[rows, :] = blk
        return blk[SUBLANES - 1:SUBLANES, :]

    carry_ref[...] = lax.fori_loop(0, TT // SUBLANES, carry_step, carry_ref[...])
    y = jnp.dot(x_ref[...].astype(BF16), ccat_ref[0], preferred_element_type=F32) + d_ref[...] * u
    z_ref[...] = jax.nn.gelu(y)


def s5_prepare(a_re, a_im, log_step, b_re, b_im, c_re, c_im):
    G, P = a_re.shape
    H = S5_GROUP
    ns = G // S5_SLAB_GROUPS
    step = jnp.exp(log_step)[:, None]
    mag = jnp.exp(a_re * step)
    ang = a_im * step
    abar_re, abar_im = mag * jnp.cos(ang), mag * jnp.sin(ang)
    den = a_re * a_re + a_im * a_im
    f_re = ((abar_re - 1.0) * a_re + abar_im * a_im) / den
    f_im = (abar_im * a_re - (abar_re - 1.0) * a_im) / den
    bb_re = f_re[..., None] * b_re - f_im[..., None] * b_im
    bb_im = f_re[..., None] * b_im + f_im[..., None] * b_re
    eye = jnp.eye(S5_SLAB_GROUPS, dtype=F32)

    def in_slab(bb):
        t = bb.reshape(ns, S5_SLAB_GROUPS, P, H)
        return jnp.einsum('sgph,gk->sghkp', t, eye).reshape(ns, LANES, S5_SLAB_STATES)

    def out_slab(cc):
        t = cc.reshape(ns, S5_SLAB_GROUPS, H, P)
        return jnp.einsum('sghp,gk->sgpkh', t, eye).reshape(ns, S5_SLAB_STATES, LANES)

    bcat = jnp.concatenate([in_slab(bb_re), in_slab(bb_im)], axis=-1).astype(BF16)
    ccat = jnp.concatenate([out_slab(c_re), -out_slab(c_im)], axis=1).astype(BF16)
    pw_re, pw_im = [abar_re], [abar_im]
    for _ in range(SUBLANES - 1):
        r, i = pw_re[-1], pw_im[-1]
        pw_re.append(r * abar_re - i * abar_im)
        pw_im.append(r * abar_im + i * abar_re)
    slab = lambda t: jnp.stack(t, 0).reshape(SUBLANES, ns, S5_SLAB_STATES).transpose(1, 0, 2)
    pr, pi = slab(pw_re), slab(pw_im)
    return bcat, ccat, jnp.concatenate([pr, pr], -1), jnp.concatenate([-pi, pi], -1)


def s5_ssm(pm, batch, seq, u_col, prep, d):
    T = batch * seq
    bcat, ccat, pr, pi = prep
    ns = bcat.shape[0]
    NS = 2 * S5_SLAB_STATES
    nt = seq // S5_TILE
    return pl.pallas_call(
        _s5_kernel,
        grid=(batch, ns, nt),
        in_specs=[pl.BlockSpec((S5_TILE, LANES), lambda b, s, t: (b * nt + t, u_col + s)),
                  pl.BlockSpec((1, LANES, NS), lambda b, s, t: (s, 0, 0)),
                  pl.BlockSpec((1, NS, LANES), lambda b, s, t: (s, 0, 0)),
                  pl.BlockSpec((1, SUBLANES, NS), lambda b, s, t: (s, 0, 0)),
                  pl.BlockSpec((1, SUBLANES, NS), lambda b, s, t: (s, 0, 0)),
                  pl.BlockSpec((1, LANES), lambda b, s, t: (0, s))],
        out_specs=pl.BlockSpec((S5_TILE, LANES), lambda b, s, t: (b * nt + t, s)),
        out_shape=jax.ShapeDtypeStruct((T, ns * LANES), F32),
        scratch_shapes=[pltpu.VMEM((S5_TILE, NS), F32), pltpu.VMEM((1, NS), F32)],
        name="s5_ssm",
        compiler_params=_params("parallel", "parallel", "arbitrary"),
    )(pm, bcat, ccat, pr, pi, d.reshape(1, -1))


def _glu_kernel(z_ref, w_ref, b_ref, o_ref):
    z = z_ref[...]
    y = jnp.dot(z.astype(BF16), w_ref[...], preferred_element_type=F32) + b_ref[...]
    o_ref[...] = (z * jax.nn.sigmoid(y)).astype(o_ref.dtype)


def glu(z, w, b, tm=512):
    T, W = z.shape
    return pl.pallas_call(
        _glu_kernel,
        grid=(T // tm,),
        in_specs=[pl.BlockSpec((tm, W), lambda i: (i, 0)),
                  pl.BlockSpec((W, W), lambda i: (0, 0)),
                  pl.BlockSpec((1, W), lambda i: (0, 0))],
        out_specs=pl.BlockSpec((tm, W), lambda i: (i, 0)),
        out_shape=jax.ShapeDtypeStruct((T, W), BF16),
        name="s5_glu",
        compiler_params=_params("parallel"),
    )(z, w.astype(BF16), b.reshape(1, W))


def _rope(t, cos, sin_lo, sin_hi, rope_half):
    return (t * cos + pltpu.roll(t, LANES - rope_half, 1) * sin_lo + pltpu.roll(t, rope_half, 1) * sin_hi)


def rope_tables(seq, head_dim):
    rope_dims = head_dim // 4
    half = rope_dims // 2
    inv = jnp.power(ROPE_THETA, -jnp.arange(half, dtype=F32) / half)
    ang = jnp.arange(seq).astype(F32)[:, None] * inv[None, :]
    cos, sin = jnp.cos(ang), jnp.sin(ang)
    pad = head_dim - rope_dims
    zeros = jnp.zeros((seq, half), F32)
    cos_t = jnp.concatenate([cos, cos, jnp.ones((seq, pad), F32)], axis=-1)
    sin_lo = jnp.concatenate([-sin, zeros, jnp.zeros((seq, pad), F32)], axis=-1)
    sin_hi = jnp.concatenate([zeros, sin, jnp.zeros((seq, pad), F32)], axis=-1)
    return cos_t, sin_lo, sin_hi, half


def _moba_kv_kernel(k_ref, v_ref, cos_ref, slo_ref, shi_ref, kr_ref, km_ref, vt_ref, *, nb, rope_half):
    for n in range(nb):
        rows = pl.ds(n * MOBA_BLOCK, MOBA_BLOCK)
        kr = _rope(k_ref[rows, :], cos_ref[rows, :], slo_ref[rows, :], shi_ref[rows, :], rope_half)
        kr_ref[rows, :] = kr.astype(kr_ref.dtype)
        km_ref[0, 0, n:n + 1, :] = jnp.mean(kr, axis=0, keepdims=True)
        vt_ref[0, 0, n] = v_ref[rows, :].T.astype(vt_ref.dtype)


def _moba_attn_kernel(q_ref, cos_ref, slo_ref, shi_ref, kr_ref, vt_ref, km_ref, o_ref, sel_ref, *, nb, rope_half):
    BLK = MOBA_BLOCK
    i = pl.program_id(2)
    hd = q_ref.shape[-1]
    log2_scale = hd ** -0.5 * math.log2(math.e)
    qt = _rope(q_ref[...], cos_ref[...], slo_ref[...], shi_ref[...], rope_half).T
    gate = jnp.dot(km_ref[0, 0], qt, precision=lax.Precision.HIGHEST, preferred_element_type=F32)
    blk_id = lax.broadcasted_iota(jnp.int32, (nb, BLK), 0)
    rank = jnp.zeros((nb, BLK), jnp.int32)
    for m in range(nb):
        gm = gate[m:m + 1, :]
        beats = (gm > gate) | ((gm == gate) & (m < blk_id))
        rank = rank + jnp.where(beats & (m < i), 1, 0)
    sel_ref[...] = ((blk_id < i) & (rank < MOBA_TOPK)).astype(F32)
    qb = (qt * log2_scale).astype(BF16)

    own = pl.ds(pl.multiple_of(i * BLK, BLK), BLK)
    s = jnp.dot(kr_ref[own, :], qb, preferred_element_type=F32)
    key = lax.broadcasted_iota(jnp.int32, (BLK, BLK), 0)
    qry = lax.broadcasted_iota(jnp.int32, (BLK, BLK), 1)
    s = jnp.where(key <= qry, s, MASK_VALUE)
    m0 = jnp.max(s, axis=0, keepdims=True)
    p = jnp.exp2(s - m0)
    l0 = jnp.sum(p, axis=0, keepdims=True)
    acc0 = jnp.dot(vt_ref[0, 0, i], p.astype(BF16), preferred_element_type=F32)

    def body(g, carry):
        m_prev, l_prev, acc = carry
        scores = []
        for b in range(MOBA_GROUP):
            n = g * MOBA_GROUP + b
            rows = pl.ds(pl.multiple_of(n * BLK, BLK), BLK)
            sn = jnp.dot(kr_ref[rows, :], qb, preferred_element_type=F32)
            scores.append(jnp.where(sel_ref[pl.ds(n, 1), :] > 0.0, sn, MASK_VALUE))
        m_new = m_prev
        for sn in scores:
            m_new = jnp.maximum(m_new, jnp.max(sn, axis=0, keepdims=True))
        alpha = jnp.exp2(m_prev - m_new)
        l_new = alpha * l_prev
        acc = alpha * acc
        for b, sn in enumerate(scores):
            pn = jnp.exp2(sn - m_new)
            l_new = l_new + jnp.sum(pn, axis=0, keepdims=True)
            acc = acc + jnp.dot(vt_ref[0, 0, g * MOBA_GROUP + b], pn.astype(BF16), preferred_element_type=F32)
        return m_new, l_new, acc

    groups = (i + MOBA_GROUP - 1) // MOBA_GROUP
    _, l, acc = lax.fori_loop(0, groups, body, (m0, l0, acc0))
    o_ref[...] = (acc / l).T.astype(o_ref.dtype)


def moba(pm, batch, seq, q_col, k_col, v_col):
    T = batch * seq
    H, hd, BLK = MOBA_HEADS, LANES, MOBA_BLOCK
    nb = seq // BLK
    assert seq % BLK == 0 and nb % MOBA_GROUP == 0
    cos_t, sin_lo, sin_hi, half = rope_tables(seq, hd)
    full = pl.BlockSpec((seq, hd), lambda b, h: (0, 0))
    kr, kmean, vt = pl.pallas_call(
        functools.partial(_moba_kv_kernel, nb=nb, rope_half=half),
        grid=(batch, H),
        in_specs=[pl.BlockSpec((seq, hd), lambda b, h: (b, k_col + h)),
                  pl.BlockSpec((seq, hd), lambda b, h: (b, v_col + h)), full, full, full],
        out_specs=[pl.BlockSpec((seq, hd), lambda b, h: (b, h)),
                   pl.BlockSpec((1, 1, nb, hd), lambda b, h: (b, h, 0, 0)),
                   pl.BlockSpec((1, 1, nb, hd, BLK), lambda b, h: (b, h, 0, 0, 0))],
        out_shape=[jax.ShapeDtypeStruct((T, H * hd), BF16), jax.ShapeDtypeStruct((batch, H, nb, hd), F32),
                   jax.ShapeDtypeStruct((batch, H, nb, hd, BLK), BF16)],
        name="moba_kv",
        compiler_params=_params("parallel", "parallel"),
    )(pm, pm, cos_t, sin_lo, sin_hi)
    tab = pl.BlockSpec((BLK, hd), lambda b, h, i: (i, 0))
    return pl.pallas_call(
        functools.partial(_moba_attn_kernel, nb=nb, rope_half=half),
        grid=(batch, H, nb),
        in_specs=[pl.BlockSpec((BLK, hd), lambda b, h, i: (b * nb + i, q_col + h)), tab, tab, tab,
                  pl.BlockSpec((seq, hd), lambda b, h, i: (b, h)),
                  pl.BlockSpec((1, 1, nb, hd, BLK), lambda b, h, i: (b, h, 0, 0, 0)),
                  pl.BlockSpec((1, 1, nb, hd), lambda b, h, i: (b, h, 0, 0))],
        out_specs=pl.BlockSpec((BLK, hd), lambda b, h, i: (b * nb + i, h)),
        out_shape=jax.ShapeDtypeStruct((T, H * hd), BF16),
        scratch_shapes=[pltpu.VMEM((nb, BLK), F32)],
        name="moba_attn",
        compiler_params=_params("parallel", "parallel", "arbitrary"),
    )(pm, cos_t, sin_lo, sin_hi, kr, vt, kmean)


def _gelu_erf(x):
    return 0.5 * x * (1.0 + lax.erf(x * (2.0 ** -0.5)))


def _top_values(x, count):
    vals = []
    for _ in range(count):
        m = jnp.max(x, axis=0, keepdims=True)
        vals.append(m)
        x = jnp.where(x >= m, -jnp.inf, x)
    return vals


def _peer_topk_kernel(q_ref, sk_ref, thr_ref, e1_ref, s2_ref, e2_ref):
    nt = (((1,), (1,)), ((), ()))
    K = PEER_TOPK
    q = q_ref[...]
    half = q.shape[-1] // 2
    s1 = lax.dot_general(sk_ref[0, 0], q[:, :half], nt, precision=lax.Precision.HIGHEST, preferred_element_type=F32)
    s2 = lax.dot_general(sk_ref[0, 1], q[:, half:], nt, precision=lax.Precision.HIGHEST, preferred_element_type=F32)
    v1 = _top_values(s1, K)
    v2 = _top_values(s2, K)
    v2_all = jnp.concatenate(v2, axis=0)
    cand = jnp.concatenate([v1[a] + v2_all for a in range(K)], axis=0)
    tau = _top_values(cand, K)[-1]
    top = v1[0] + v2[0]
    z = jnp.sum(jnp.where(cand >= tau, jnp.exp(cand - top), 0.0), axis=0, keepdims=True)
    thr = jnp.full_like(s1, jnp.inf)
    for b in range(K):
        thr = jnp.where(s1 + v2[b] >= tau, v2[b], thr)
    thr_ref[0] = jnp.where(s1 >= v1[K - 1], thr, jnp.inf)
    e1_ref[0] = jnp.exp(s1 - v1[0])
    s2_ref[0] = s2
    e2_ref[0] = jnp.exp(s2 - v2[0]) / z


def peer_topk(q, subkeys, tt=256):
    T = q.shape[0]
    H, _, NK, dh = subkeys.shape
    assert NK == PEER_NKEYS
    spec = pl.BlockSpec((1, NK, tt), lambda i, h: (h, 0, i))
    shape = jax.ShapeDtypeStruct((H, NK, T), F32)
    return pl.pallas_call(
        _peer_topk_kernel,
        grid=(T // tt, H),
        in_specs=[pl.BlockSpec((tt, 2 * dh), lambda i, h: (i, h)),
                  pl.BlockSpec((1, 2, NK, dh), lambda i, h: (h, 0, 0, 0))],
        out_specs=[spec] * 4,
        out_shape=[shape] * 4,
        name="peer_topk",
        compiler_params=_params("parallel", "parallel"),
    )(q, subkeys)


def _peer_gate_kernel(ht_ref, u_ref, thr_ref, e1_ref, s2_ref, e2_ref, w_ref, act_ref, g_ref, *, rows_per_tile):
    NK = PEER_NKEYS
    nchunks = ht_ref.shape[1] // LANES
    for s in range(rows_per_tile // PEER_SLICE_ROWS):
        rows = slice(s * PEER_SLICE_ROWS * NK, (s + 1) * PEER_SLICE_ROWS * NK)
        act_ref[rows, :] = jnp.dot(u_ref[rows, :], ht_ref[...], preferred_element_type=F32)
        for k1 in range(s * PEER_SLICE_ROWS, (s + 1) * PEER_SLICE_ROWS):
            for c in range(nchunks):
                cols = slice(c * LANES, (c + 1) * LANES)
                g = None
                for h in range(PEER_HEADS):
                    thr = thr_ref[h, k1:k1 + 1, cols]
                    e1 = e1_ref[h, k1:k1 + 1, cols]
                    term = jnp.where(s2_ref[h, :, cols] >= thr, e2_ref[h, :, cols], 0.0) * e1
                    g = term if g is None else g + term
                g_ref[k1 * NK:(k1 + 1) * NK, cols] = g
    for k1 in range(rows_per_tile):
        erows = slice(k1 * NK, (k1 + 1) * NK)
        for c in range(nchunks):
            cols = slice(c * LANES, (c + 1) * LANES)
            w = g_ref[erows, cols] * _gelu_erf(act_ref[erows, cols])
            w_ref[cols, erows] = w.T.astype(w_ref.dtype)


def peer_gate(ht, u, tables, tt=PEER_TOKEN_TILE, te=PEER_EXPERT_TILE):
    D, T = ht.shape
    E = u.shape[0]
    H, NK, _ = tables[0].shape
    rows_per_tile = te // NK
    assert rows_per_tile % SUBLANES == 0
    k1spec = pl.BlockSpec((H, rows_per_tile, tt), lambda i, j: (0, j, i))
    k2spec = pl.BlockSpec((H, NK, tt), lambda i, j: (0, 0, i))
    return pl.pallas_call(
        functools.partial(_peer_gate_kernel, rows_per_tile=rows_per_tile),
        grid=(T // tt, E // te),
        in_specs=[pl.BlockSpec((D, tt), lambda i, j: (0, i)),
                  pl.BlockSpec((te, D), lambda i, j: (j, 0)),
                  k1spec, k1spec, k2spec, k2spec],
        out_specs=pl.BlockSpec((tt, te), lambda i, j: (i, j)),
        out_shape=jax.ShapeDtypeStruct((T, E), BF16),
        scratch_shapes=[pltpu.VMEM((te, tt), F32), pltpu.VMEM((te, tt), F32)],
        name="peer_gate",
        compiler_params=_params("parallel", "arbitrary"),
    )(ht, u, *tables)


def peer_ffn(x, layer, norm_g, wq, subkeys, u_tab, v_tab):
    h, ht = rmsnorm(x, norm_g, BF16, with_transpose=True)
    q = matmul(h, wq, layer=layer, name="peer_query")
    tables = peer_topk(q, subkeys)
    w = peer_gate(ht, u_tab.astype(BF16), tables)
    return matmul(w, v_tab, res=x, layer=layer, tm=2048, tk=1024, res_buffers=1, name="peer_out")


def _in_proj_layout(group_width):
    gw = group_width
    a_names, a_sizes = ('lru_x', 'lru_gate', 'gla_q', 'gla_k', 'gla_v'), (gw, gw, gw // 2, gw // 2, gw)
    b_names, b_sizes = ('gla_og', 's5_u', 'moba_q', 'moba_k', 'moba_v'), (gw,) * 5
    cols = {}
    for names, sizes in ((a_names, a_sizes), (b_names, b_sizes)):
        c = 0
        for name, size in zip(names, sizes):
            cols[name] = c // LANES
            c += size
    return sum(a_sizes), sum(b_sizes), cols


def kernel(x, norm1_g, w_in, lru_conv_w, lru_conv_b, lru_wa, lru_ba, lru_wx, lru_bx, lru_lambda, gla_wg2, gla_bg, gla_norm_g, s5_a_re, s5_a_im, s5_log_step, s5_b_re, s5_b_im, s5_c_re, s5_c_im, s5_d, s5_w_glu, s5_b_glu, w_out, norm2_g, peer_wq, peer_subkeys, peer_u, peer_v, final_norm_g):
    B, S, D = x.shape
    T = B * S
    depth = norm1_g.shape[0]
    gw = D // N_MIXERS
    x = x.reshape(T, D)
    n_a, n_b, col = _in_proj_layout(gw)
    lr0 = n_a
    for l in range(depth):
        w_b = w_in[l, :, lr0 + GLA_GATE_RANK:].astype(BF16)
        w_lr = jnp.zeros((D, LANES), BF16).at[:, :GLA_GATE_RANK].set(w_in[l, :, lr0:lr0 + GLA_GATE_RANK].astype(BF16))
        h = rmsnorm(x, norm1_g[l], BF16)
        pa = matmul(h, w_in, layer=l, n_cols=n_a, name="in_proj_a")
        pb = matmul(h, w_b, name="in_proj_b")
        plr = matmul(h, w_lr, tk=D, name="in_proj_lr")
        y_a = rglru(pa, B, S, col['lru_x'], col['lru_gate'], lru_conv_w[l], lru_conv_b[l], lru_wa[l], lru_ba[l],
                    lru_wx[l], lru_bx[l], lru_lambda[l])
        y_b = gla(pa, pb, plr, B, S, col['gla_q'], col['gla_k'], col['gla_v'], col['gla_og'],
                  gla_wg2[l], gla_bg[l], gla_norm_g[l])
        prep = s5_prepare(s5_a_re[l], s5_a_im[l], s5_log_step[l], s5_b_re[l], s5_b_im[l], s5_c_re[l], s5_c_im[l])
        z = s5_ssm(pb, B, S, col['s5_u'], prep, s5_d[l])
        y_c = glu(z, s5_w_glu[l], s5_b_glu[l])
        y_d = moba(pb, B, S, col['moba_q'], col['moba_k'], col['moba_v'])
        mixed = jnp.concatenate([y_a, y_b, y_c, y_d], axis=-1)
        x = matmul(mixed, w_out, res=x, layer=l, name="out_proj")
        x = peer_ffn(x, l, norm2_g[l], peer_wq, peer_subkeys[l], peer_u[l], peer_v)
    return rmsnorm(x, final_norm_g, F32).reshape(B, S, D)
```

```python
import functools
import math

import jax
import jax.numpy as jnp
from jax import lax
from jax.experimental import pallas as pl
from jax.experimental.pallas import tpu as pltpu

F32 = jnp.float32
BF16 = jnp.bfloat16

LANES = 128
SUBLANES = 8
VMEM_LIMIT_BYTES = 56 * 2**20

RMS_EPS = 1e-6
N_MIXERS = 4

LRU_BLOCKS = 8
LRU_CONV = 4
LRU_C = 8.0
LRU_ROWS = 256

GLA_HEADS = 4
GLA_GATE_RANK = 16
GLA_TAU = 16.0
GLA_CHUNK = 64
GLA_TILE = 512

S5_GROUP = 16
S5_STATE = 64
S5_SLAB_GROUPS = LANES // S5_GROUP
S5_SLAB_STATES = S5_SLAB_GROUPS * S5_STATE
S5_TILE = 256

MOBA_HEADS = 8
MOBA_BLOCK = 256
MOBA_TOPK = 3
MOBA_GROUP = 4
ROPE_THETA = 500000.0
MASK_VALUE = -1e30

PEER_HEADS = 8
PEER_NKEYS = 128
PEER_TOPK = 16
PEER_TOKEN_TILE = 512
PEER_EXPERT_TILE = 1024
PEER_SLICE_ROWS = 2


def _params(*semantics):
    return pltpu.CompilerParams(dimension_semantics=semantics, vmem_limit_bytes=VMEM_LIMIT_BYTES)


def _rmsnorm_kernel(x_ref, g_ref, o_ref, *maybe_ot_ref):
    x = x_ref[...]
    y = x * lax.rsqrt(jnp.mean(x * x, axis=-1, keepdims=True) + RMS_EPS) * g_ref[...]
    o_ref[...] = y.astype(o_ref.dtype)
    if maybe_ot_ref:
        maybe_ot_ref[0][...] = y.T.astype(maybe_ot_ref[0].dtype)


def rmsnorm(x, g, out_dtype, with_transpose=False, tm=256):
    T, D = x.shape
    out_shape = [jax.ShapeDtypeStruct((T, D), out_dtype)]
    out_specs = [pl.BlockSpec((tm, D), lambda i: (i, 0))]
    if with_transpose:
        out_shape.append(jax.ShapeDtypeStruct((D, T), BF16))
        out_specs.append(pl.BlockSpec((D, tm), lambda i: (0, i)))
    res = pl.pallas_call(
        _rmsnorm_kernel,
        grid=(T // tm,),
        in_specs=[pl.BlockSpec((tm, D), lambda i: (i, 0)), pl.BlockSpec((1, D), lambda i: (0, 0))],
        out_specs=out_specs,
        out_shape=out_shape,
        name="rmsnorm",
        compiler_params=_params("parallel"),
    )(x, g.reshape(1, D))
    return res if with_transpose else res[0]


def _mm_kernel(x_ref, w_ref, *rest, nk, has_res, w_transposed):
    r_ref = rest[0] if has_res else None
    o_ref = rest[1 if has_res else 0]
    contract = (((1,), (1 if w_transposed else 0,)), ((), ()))
    part = lax.dot_general(x_ref[...], w_ref[...], contract, preferred_element_type=F32)
    if nk == 1:
        o_ref[...] = (part + r_ref[...] if has_res else part).astype(o_ref.dtype)
        return
    acc_ref = rest[-1]
    k = pl.program_id(2)

    @pl.when(k == 0)
    def _():
        acc_ref[...] = part

    @pl.when(k > 0)
    def _():
        acc_ref[...] += part

    @pl.when(k == nk - 1)
    def _():
        acc = acc_ref[...]
        o_ref[...] = (acc + r_ref[...] if has_res else acc).astype(o_ref.dtype)


def matmul(x, w, res=None, *, layer=None, w_transposed=False, tm=1024, tn=1024, tk=4096, out_dtype=F32,
           name="matmul"):
    M, K = x.shape
    k_axis, n_axis = (-1, -2) if w_transposed else (-2, -1)
    N = w.shape[n_axis]
    tm, tn, tk = min(tm, M), min(tn, N), min(tk, K)
    assert M % tm == 0 and N % tn == 0 and K % tk == 0 and w.shape[k_axis] == K
    nk = K // tk
    wblock = (tn, tk) if w_transposed else (tk, tn)
    windex = (lambda i, j, k: (j, k)) if w_transposed else (lambda i, j, k: (k, j))
    if layer is None:
        wspec = pl.BlockSpec(wblock, windex)
    else:
        wspec = pl.BlockSpec((None,) + wblock, lambda i, j, k: (layer,) + windex(i, j, k))
    in_specs = [pl.BlockSpec((tm, tk), lambda i, j, k: (i, k)), wspec]
    args = [x, w]
    if res is not None:
        in_specs.append(pl.BlockSpec((tm, tn), lambda i, j, k: (i, j)))
        args.append(res)
    return pl.pallas_call(
        functools.partial(_mm_kernel, nk=nk, has_res=res is not None, w_transposed=w_transposed),
        grid=(M // tm, N // tn, nk),
        in_specs=in_specs,
        out_specs=pl.BlockSpec((tm, tn), lambda i, j, k: (i, j)),
        out_shape=jax.ShapeDtypeStruct((M, N), out_dtype),
        scratch_shapes=[pltpu.VMEM((tm, tn), F32)] if nk > 1 else [],
        name=name,
        compiler_params=_params("parallel", "parallel", "arbitrary"),
    )(*args)


def _lru_kernel(x_ref, gate_ref, cw_ref, cb_ref, wa_ref, ba_ref, wx_ref, bx_ref, lam_ref, o_ref, *, seq):
    R = LRU_ROWS
    cw = cw_ref[...]
    cb = cb_ref[...]
    ba = ba_ref[...]
    bx = bx_ref[...]
    neg_c_softplus = -LRU_C * jax.nn.softplus(-lam_ref[...])
    wa = wa_ref[0]
    wx = wx_ref[0]
    row = lax.broadcasted_iota(jnp.int32, (R, LANES), 0)
    row8 = lax.broadcasted_iota(jnp.int32, (SUBLANES, LANES), 0)

    def body(c, h):
        r0 = pl.multiple_of(c * R, R)
        xt = x_ref[pl.ds(r0, R), :]
        p0 = pl.multiple_of(jnp.maximum(r0 - SUBLANES, 0), SUBLANES)
        prev = jnp.where(c > 0, x_ref[pl.ds(p0, SUBLANES), :], 0.0)
        xc = xt * cw[LRU_CONV - 1:LRU_CONV] + cb
        for d in range(1, LRU_CONV):
            rolled = pltpu.roll(xt, d, 0)
            head = jnp.where(row8 < d, pltpu.roll(prev, d, 0), rolled[:SUBLANES])
            shifted = jnp.concatenate([head, rolled[SUBLANES:]], axis=0)
            xc = xc + shifted * cw[LRU_CONV - 1 - d:LRU_CONV - d]
        xb = xc.astype(BF16)
        r = jax.nn.sigmoid(jnp.dot(xb, wa, preferred_element_type=F32) + ba)
        i = jax.nn.sigmoid(jnp.dot(xb, wx, preferred_element_type=F32) + bx)
        log_a = r * neg_c_softplus
        a = jnp.exp(log_a)
        b = jnp.sqrt(1.0 - a * a) * (i * xc)
        d = 1
        while d < R:
            keep = row >= d
            a_sh = jnp.where(keep, pltpu.roll(a, d, 0), 1.0)
            b_sh = jnp.where(keep, pltpu.roll(b, d, 0), 0.0)
            b = a * b_sh + b
            a = a * a_sh
            d *= 2
        hs = b + a * h
        o_ref[pl.ds(r0, R), :] = (hs * jax.nn.gelu(gate_ref[pl.ds(r0, R), :])).astype(o_ref.dtype)
        return hs[R - 1:R, :]

    lax.fori_loop(0, seq // R, body, jnp.zeros((1, LANES), F32))


def rglru(pm, batch, seq, x_col, gate_col, conv_w, conv_b, wa, ba, wx, bx, lam):
    T = batch * seq
    W = LRU_BLOCKS * LANES
    vec = lambda v: v.reshape(1, W)
    vspec = pl.BlockSpec((1, LANES), lambda b, n: (0, n))
    wspec = pl.BlockSpec((1, LANES, LANES), lambda b, n: (n, 0, 0))
    return pl.pallas_call(
        functools.partial(_lru_kernel, seq=seq),
        grid=(batch, LRU_BLOCKS),
        in_specs=[pl.BlockSpec((seq, LANES), lambda b, n: (b, x_col + n)),
                  pl.BlockSpec((seq, LANES), lambda b, n: (b, gate_col + n)),
                  pl.BlockSpec((LRU_CONV, LANES), lambda b, n: (0, n)),
                  vspec, wspec, vspec, wspec, vspec, vspec],
        out_specs=pl.BlockSpec((seq, LANES), lambda b, n: (b, n)),
        out_shape=jax.ShapeDtypeStruct((T, W), BF16),
        name="rglru",
        compiler_params=_params("parallel", "parallel"),
    )(pm, pm, conv_w, vec(conv_b), wa.astype(BF16), vec(ba), wx.astype(BF16), vec(bx), vec(lam))


def _gla_kernel(q_ref, k_ref, v_ref, og_ref, lr_ref, wg_ref, bg_ref, ng_ref, o_ref, st_ref):
    C = GLA_CHUNK
    dk = q_ref.shape[-1]
    dv = v_ref.shape[-1]

    @pl.when(pl.program_id(2) == 0)
    def _():
        st_ref[...] = jnp.zeros_like(st_ref)

    wg = wg_ref[0]
    bg = bg_ref[0]
    ng = ng_ref[0]
    ri = lax.broadcasted_iota(jnp.int32, (C, C), 0)
    ci = lax.broadcasted_iota(jnp.int32, (C, C), 1)
    causal = ri >= ci
    tril = causal.astype(F32)
    scale = dk ** -0.5
    nt = (((1,), (1,)), ((), ()))
    tn = (((0,), (0,)), ((), ()))

    for c in range(GLA_TILE // C):
        rows = pl.ds(c * C, C)
        q = q_ref[rows, :]
        k = k_ref[rows, :]
        v = v_ref[rows, :].astype(BF16)
        pre = jnp.dot(lr_ref[rows, :], wg, precision=lax.Precision.HIGHEST, preferred_element_type=F32) + bg
        g = jax.nn.log_sigmoid(pre) / GLA_TAU
        bc = jnp.dot(tril, g, precision=lax.Precision.HIGHEST, preferred_element_type=F32)
        b_last = bc[C - 1:C, :]
        qe = (q * scale * jnp.exp(bc)).astype(BF16)
        ke = (k * jnp.exp(-bc)).astype(BF16)
        kd = (k * jnp.exp(b_last - bc)).astype(BF16)
        att = lax.dot_general(qe, ke, nt, preferred_element_type=F32)
        att = jnp.where(causal, att, 0.0).astype(BF16)
        st = st_ref[...]
        o = jnp.dot(att, v, preferred_element_type=F32)
        o = o + lax.dot_general(qe, st.astype(BF16), nt, preferred_element_type=F32)
        st_ref[...] = st * jnp.exp(b_last) + lax.dot_general(v, kd, tn, preferred_element_type=F32)
        o = o * lax.rsqrt(jnp.mean(o * o, axis=-1, keepdims=True) + RMS_EPS)
        o_ref[rows, :] = (o * ng * jax.nn.silu(og_ref[rows, :])).astype(o_ref.dtype)


def gla(pm, pm_og, plr, batch, seq, q_col, k_col, v_col, og_col, wg2, bg, norm_g):
    T = batch * seq
    H = GLA_HEADS
    dk = wg2.shape[-1] // H
    dv = norm_g.shape[-1] // H
    assert dk == LANES and dv == 2 * LANES
    nt = seq // GLA_TILE
    wg = jnp.zeros((LANES, H * dk), F32).at[:GLA_GATE_RANK].set(wg2)
    wg = wg.reshape(LANES, H, dk).transpose(1, 0, 2)
    tok = lambda b, h, t: b * nt + t
    return pl.pallas_call(
        _gla_kernel,
        grid=(batch, H, nt),
        in_specs=[pl.BlockSpec((GLA_TILE, dk), lambda b, h, t: (tok(b, h, t), q_col + h)),
                  pl.BlockSpec((GLA_TILE, dk), lambda b, h, t: (tok(b, h, t), k_col + h)),
                  pl.BlockSpec((GLA_TILE, dv), lambda b, h, t: (tok(b, h, t), v_col // 2 + h)),
                  pl.BlockSpec((GLA_TILE, dv), lambda b, h, t: (tok(b, h, t), og_col // 2 + h)),
                  pl.BlockSpec((GLA_TILE, LANES), lambda b, h, t: (tok(b, h, t), 0)),
                  pl.BlockSpec((1, LANES, dk), lambda b, h, t: (h, 0, 0)),
                  pl.BlockSpec((1, 1, dk), lambda b, h, t: (h, 0, 0)),
                  pl.BlockSpec((1, 1, dv), lambda b, h, t: (h, 0, 0))],
        out_specs=pl.BlockSpec((GLA_TILE, dv), lambda b, h, t: (tok(b, h, t), h)),
        out_shape=jax.ShapeDtypeStruct((T, H * dv), BF16),
        scratch_shapes=[pltpu.VMEM((dv, dk), F32)],
        name="gla",
        compiler_params=_params("parallel", "parallel", "arbitrary"),
    )(pm, pm, pm, pm_og, plr, wg, bg.reshape(H, 1, dk), norm_g.reshape(H, 1, dv))


def _s5_kernel(u_ref, bcat_ref, ccat_ref, pr_ref, pi_ref, d_ref, z_ref, x_ref, carry_ref):
    TT = S5_TILE
    NS = 2 * S5_SLAB_STATES

    @pl.when(pl.program_id(2) == 0)
    def _():
        carry_ref[...] = jnp.zeros_like(carry_ref)

    u = u_ref[...]
    pr = pr_ref[0]
    pi = pi_ref[0]

    def swap(t):
        return pltpu.roll(t, S5_SLAB_STATES, 1)

    x = jnp.dot(u.astype(BF16), bcat_ref[0], preferred_element_type=F32)
    sub = lax.broadcasted_iota(jnp.int32, (TT, NS), 0) % SUBLANES
    for d in (1, 2, 4):
        sh = jnp.where(sub >= d, pltpu.roll(x, d, 0), 0.0)
        x = x + pr[d - 1:d] * sh + pi[d - 1:d] * swap(sh)
    x_ref[...] = x

    def carry_step(i, carry):
        rows = pl.ds(pl.multiple_of(i * SUBLANES, SUBLANES), SUBLANES)
        cb = jnp.broadcast_to(carry, (SUBLANES, NS))
        blk = x_ref[rows, :] + pr * cb + pi * swap(cb)
        x_ref[rows, :] = blk
        return blk[SUBLANES - 1:SUBLANES, :]

    carry_ref[...] = lax.fori_loop(0, TT // SUBLANES, carry_step, carry_ref[...])
    y = jnp.dot(x_ref[...].astype(BF16), ccat_ref[0], preferred_element_type=F32) + d_ref[...] * u
    z_ref[...] = jax.nn.gelu(y)


def s5_prepare(a_re, a_im, log_step, b_re, b_im, c_re, c_im):
    G, P = a_re.shape
    H = S5_GROUP
    ns = G // S5_SLAB_GROUPS
    step = jnp.exp(log_step)[:, None]
    mag = jnp.exp(a_re * step)
    ang = a_im * step
    abar_re, abar_im = mag * jnp.cos(ang), mag * jnp.sin(ang)
    den = a_re * a_re + a_im * a_im
    f_re = ((abar_re - 1.0) * a_re + abar_im * a_im) / den
    f_im = (abar_im * a_re - (abar_re - 1.0) * a_im) / den
    bb_re = f_re[..., None] * b_re - f_im[..., None] * b_im
    bb_im = f_re[..., None] * b_im + f_im[..., None] * b_re
    eye = jnp.eye(S5_SLAB_GROUPS, dtype=F32)

    def in_slab(bb):
        t = bb.reshape(ns, S5_SLAB_GROUPS, P, H)
        return jnp.einsum('sgph,gk->sghkp', t, eye).reshape(ns, LANES, S5_SLAB_STATES)

    def out_slab(cc):
        t = cc.reshape(ns, S5_SLAB_GROUPS, H, P)
        return jnp.einsum('sghp,gk->sgpkh', t, eye).reshape(ns, S5_SLAB_STATES, LANES)

    bcat = jnp.concatenate([in_slab(bb_re), in_slab(bb_im)], axis=-1).astype(BF16)
    ccat = jnp.concatenate([out_slab(c_re), -out_slab(c_im)], axis=1).astype(BF16)
    pw_re, pw_im = [abar_re], [abar_im]
    for _ in range(SUBLANES - 1):
        r, i = pw_re[-1], pw_im[-1]
        pw_re.append(r * abar_re - i * abar_im)
        pw_im.append(r * abar_im + i * abar_re)
    slab = lambda t: jnp.stack(t, 0).reshape(SUBLANES, ns, S5_SLAB_STATES).transpose(1, 0, 2)
    pr, pi = slab(pw_re), slab(pw_im)
    return bcat, ccat, jnp.concatenate([pr, pr], -1), jnp.concatenate([-pi, pi], -1)


def s5_ssm(pm, batch, seq, u_col, prep, d):
    T = batch * seq
    bcat, ccat, pr, pi = prep
    ns = bcat.shape[0]
    NS = 2 * S5_SLAB_STATES
    nt = seq // S5_TILE
    return pl.pallas_call(
        _s5_kernel,
        grid=(batch, ns, nt),
        in_specs=[pl.BlockSpec((S5_TILE, LANES), lambda b, s, t: (b * nt + t, u_col + s)),
                  pl.BlockSpec((1, LANES, NS), lambda b, s, t: (s, 0, 0)),
                  pl.BlockSpec((1, NS, LANES), lambda b, s, t: (s, 0, 0)),
                  pl.BlockSpec((1, SUBLANES, NS), lambda b, s, t: (s, 0, 0)),
                  pl.BlockSpec((1, SUBLANES, NS), lambda b, s, t: (s, 0, 0)),
                  pl.BlockSpec((1, LANES), lambda b, s, t: (0, s))],
        out_specs=pl.BlockSpec((S5_TILE, LANES), lambda b, s, t: (b * nt + t, s)),
        out_shape=jax.ShapeDtypeStruct((T, ns * LANES), F32),
        scratch_shapes=[pltpu.VMEM((S5_TILE, NS), F32), pltpu.VMEM((1, NS), F32)],
        name="s5_ssm",
        compiler_params=_params("parallel", "parallel", "arbitrary"),
    )(pm, bcat, ccat, pr, pi, d.reshape(1, -1))


def _glu_kernel(z_ref, w_ref, b_ref, o_ref):
    z = z_ref[...]
    y = jnp.dot(z.astype(BF16), w_ref[...], preferred_element_type=F32) + b_ref[...]
    o_ref[...] = (z * jax.nn.sigmoid(y)).astype(o_ref.dtype)


def glu(z, w, b, tm=512):
    T, W = z.shape
    return pl.pallas_call(
        _glu_kernel,
        grid=(T // tm,),
        in_specs=[pl.BlockSpec((tm, W), lambda i: (i, 0)),
                  pl.BlockSpec((W, W), lambda i: (0, 0)),
                  pl.BlockSpec((1, W), lambda i: (0, 0))],
        out_specs=pl.BlockSpec((tm, W), lambda i: (i, 0)),
        out_shape=jax.ShapeDtypeStruct((T, W), BF16),
        name="s5_glu",
        compiler_params=_params("parallel"),
    )(z, w.astype(BF16), b.reshape(1, W))


def _rope(t, cos, sin_lo, sin_hi, rope_half):
    return (t * cos + pltpu.roll(t, LANES - rope_half, 1) * sin_lo + pltpu.roll(t, rope_half, 1) * sin_hi)


def rope_tables(seq, head_dim):
    rope_dims = head_dim // 4
    half = rope_dims // 2
    inv = jnp.power(ROPE_THETA, -jnp.arange(half, dtype=F32) / half)
    ang = jnp.arange(seq).astype(F32)[:, None] * inv[None, :]
    cos, sin = jnp.cos(ang), jnp.sin(ang)
    pad = head_dim - rope_dims
    zeros = jnp.zeros((seq, half), F32)
    cos_t = jnp.concatenate([cos, cos, jnp.ones((seq, pad), F32)], axis=-1)
    sin_lo = jnp.concatenate([-sin, zeros, jnp.zeros((seq, pad), F32)], axis=-1)
    sin_hi = jnp.concatenate([zeros, sin, jnp.zeros((seq, pad), F32)], axis=-1)
    return cos_t, sin_lo, sin_hi, half


def _moba_kv_kernel(k_ref, v_ref, cos_ref, slo_ref, shi_ref, kr_ref, km_ref, vt_ref, *, nb, rope_half):
    for n in range(nb):
        rows = pl.ds(n * MOBA_BLOCK, MOBA_BLOCK)
        kr = _rope(k_ref[rows, :], cos_ref[rows, :], slo_ref[rows, :], shi_ref[rows, :], rope_half)
        kr_ref[rows, :] = kr.astype(kr_ref.dtype)
        km_ref[0, 0, n:n + 1, :] = jnp.mean(kr, axis=0, keepdims=True)
        vt_ref[0, 0, n] = v_ref[rows, :].T.astype(vt_ref.dtype)


def _moba_attn_kernel(q_ref, cos_ref, slo_ref, shi_ref, kr_ref, vt_ref, km_ref, o_ref, sel_ref, *, nb, rope_half):
    BLK = MOBA_BLOCK
    i = pl.program_id(2)
    hd = q_ref.shape[-1]
    log2_scale = hd ** -0.5 * math.log2(math.e)
    qt = _rope(q_ref[...], cos_ref[...], slo_ref[...], shi_ref[...], rope_half).T
    gate = jnp.dot(km_ref[0, 0], qt, precision=lax.Precision.HIGHEST, preferred_element_type=F32)
    blk_id = lax.broadcasted_iota(jnp.int32, (nb, BLK), 0)
    rank = jnp.zeros((nb, BLK), jnp.int32)
    for m in range(nb):
        gm = gate[m:m + 1, :]
        beats = (gm > gate) | ((gm == gate) & (m < blk_id))
        rank = rank + jnp.where(beats & (m < i), 1, 0)
    sel_ref[...] = ((blk_id < i) & (rank < MOBA_TOPK)).astype(F32)
    qb = (qt * log2_scale).astype(BF16)

    own = pl.ds(pl.multiple_of(i * BLK, BLK), BLK)
    s = jnp.dot(kr_ref[own, :], qb, preferred_element_type=F32)
    key = lax.broadcasted_iota(jnp.int32, (BLK, BLK), 0)
    qry = lax.broadcasted_iota(jnp.int32, (BLK, BLK), 1)
    s = jnp.where(key <= qry, s, MASK_VALUE)
    m0 = jnp.max(s, axis=0, keepdims=True)
    p = jnp.exp2(s - m0)
    l0 = jnp.sum(p, axis=0, keepdims=True)
    acc0 = jnp.dot(vt_ref[0, 0, i], p.astype(BF16), preferred_element_type=F32)

    def body(g, carry):
        m_prev, l_prev, acc = carry
        scores = []
        for b in range(MOBA_GROUP):
            n = g * MOBA_GROUP + b
            rows = pl.ds(pl.multiple_of(n * BLK, BLK), BLK)
            sn = jnp.dot(kr_ref[rows, :], qb, preferred_element_type=F32)
            scores.append(jnp.where(sel_ref[pl.ds(n, 1), :] > 0.0, sn, MASK_VALUE))
        m_new = m_prev
        for sn in scores:
            m_new = jnp.maximum(m_new, jnp.max(sn, axis=0, keepdims=True))
        alpha = jnp.exp2(m_prev - m_new)
        l_new = alpha * l_prev
        acc = alpha * acc
        for b, sn in enumerate(scores):
            pn = jnp.exp2(sn - m_new)
            l_new = l_new + jnp.sum(pn, axis=0, keepdims=True)
            acc = acc + jnp.dot(vt_ref[0, 0, g * MOBA_GROUP + b], pn.astype(BF16), preferred_element_type=F32)
        return m_new, l_new, acc

    groups = (i + MOBA_GROUP - 1) // MOBA_GROUP
    _, l, acc = lax.fori_loop(0, groups, body, (m0, l0, acc0))
    o_ref[...] = (acc / l).T.astype(o_ref.dtype)


def moba(pm, batch, seq, q_col, k_col, v_col):
    T = batch * seq
    H, hd, BLK = MOBA_HEADS, LANES, MOBA_BLOCK
    nb = seq // BLK
    assert seq % BLK == 0 and nb % MOBA_GROUP == 0
    cos_t, sin_lo, sin_hi, half = rope_tables(seq, hd)
    full = pl.BlockSpec((seq, hd), lambda b, h: (0, 0))
    kr, kmean, vt = pl.pallas_call(
        functools.partial(_moba_kv_kernel, nb=nb, rope_half=half),
        grid=(batch, H),
        in_specs=[pl.BlockSpec((seq, hd), lambda b, h: (b, k_col + h)),
                  pl.BlockSpec((seq, hd), lambda b, h: (b, v_col + h)), full, full, full],
        out_specs=[pl.BlockSpec((seq, hd), lambda b, h: (b, h)),
                   pl.BlockSpec((1, 1, nb, hd), lambda b, h: (b, h, 0, 0)),
                   pl.BlockSpec((1, 1, nb, hd, BLK), lambda b, h: (b, h, 0, 0, 0))],
        out_shape=[jax.ShapeDtypeStruct((T, H * hd), BF16), jax.ShapeDtypeStruct((batch, H, nb, hd), F32),
                   jax.ShapeDtypeStruct((batch, H, nb, hd, BLK), BF16)],
        name="moba_kv",
        compiler_params=_params("parallel", "parallel"),
    )(pm, pm, cos_t, sin_lo, sin_hi)
    tab = pl.BlockSpec((BLK, hd), lambda b, h, i: (i, 0))
    return pl.pallas_call(
        functools.partial(_moba_attn_kernel, nb=nb, rope_half=half),
        grid=(batch, H, nb),
        in_specs=[pl.BlockSpec((BLK, hd), lambda b, h, i: (b * nb + i, q_col + h)), tab, tab, tab,
                  pl.BlockSpec((seq, hd), lambda b, h, i: (b, h)),
                  pl.BlockSpec((1, 1, nb, hd, BLK), lambda b, h, i: (b, h, 0, 0, 0)),
                  pl.BlockSpec((1, 1, nb, hd), lambda b, h, i: (b, h, 0, 0))],
        out_specs=pl.BlockSpec((BLK, hd), lambda b, h, i: (b * nb + i, h)),
        out_shape=jax.ShapeDtypeStruct((T, H * hd), BF16),
        scratch_shapes=[pltpu.VMEM((nb, BLK), F32)],
        name="moba_attn",
        compiler_params=_params("parallel", "parallel", "arbitrary"),
    )(pm, cos_t, sin_lo, sin_hi, kr, vt, kmean)


def _gelu_erf(x):
    return 0.5 * x * (1.0 + lax.erf(x * (2.0 ** -0.5)))


def _top_values(x, count):
    vals = []
    for _ in range(count):
        m = jnp.max(x, axis=0, keepdims=True)
        vals.append(m)
        x = jnp.where(x >= m, -jnp.inf, x)
    return vals


def _peer_topk_kernel(q_ref, sk_ref, thr_ref, e1_ref, s2_ref, e2_ref):
    nt = (((1,), (1,)), ((), ()))
    K = PEER_TOPK
    q = q_ref[...]
    half = q.shape[-1] // 2
    s1 = lax.dot_general(sk_ref[0, 0], q[:, :half], nt, precision=lax.Precision.HIGHEST, preferred_element_type=F32)
    s2 = lax.dot_general(sk_ref[0, 1], q[:, half:], nt, precision=lax.Precision.HIGHEST, preferred_element_type=F32)
    v1 = _top_values(s1, K)
    v2 = _top_values(s2, K)
    v2_all = jnp.concatenate(v2, axis=0)
    cand = jnp.concatenate([v1[a] + v2_all for a in range(K)], axis=0)
    tau = _top_values(cand, K)[-1]
    top = v1[0] + v2[0]
    z = jnp.sum(jnp.where(cand >= tau, jnp.exp(cand - top), 0.0), axis=0, keepdims=True)
    thr = jnp.full_like(s1, jnp.inf)
    for b in range(K):
        thr = jnp.where(s1 + v2[b] >= tau, v2[b], thr)
    thr_ref[0] = jnp.where(s1 >= v1[K - 1], thr, jnp.inf)
    e1_ref[0] = jnp.exp(s1 - v1[0])
    s2_ref[0] = s2
    e2_ref[0] = jnp.exp(s2 - v2[0]) / z


def peer_topk(q, subkeys, tt=256):
    T = q.shape[0]
    H, _, NK, dh = subkeys.shape
    assert NK == PEER_NKEYS
    spec = pl.BlockSpec((1, NK, tt), lambda i, h: (h, 0, i))
    shape = jax.ShapeDtypeStruct((H, NK, T), F32)
    return pl.pallas_call(
        _peer_topk_kernel,
        grid=(T // tt, H),
        in_specs=[pl.BlockSpec((tt, 2 * dh), lambda i, h: (i, h)),
                  pl.BlockSpec((1, 2, NK, dh), lambda i, h: (h, 0, 0, 0))],
        out_specs=[spec] * 4,
        out_shape=[shape] * 4,
        name="peer_topk",
        compiler_params=_params("parallel", "parallel"),
    )(q, subkeys)


def _peer_gate_kernel(ht_ref, u_ref, thr_ref, e1_ref, s2_ref, e2_ref, w_ref, act_ref, g_ref, *, rows_per_tile):
    NK = PEER_NKEYS
    nchunks = ht_ref.shape[1] // LANES
    for s in range(rows_per_tile // PEER_SLICE_ROWS):
        rows = slice(s * PEER_SLICE_ROWS * NK, (s + 1) * PEER_SLICE_ROWS * NK)
        act_ref[rows, :] = jnp.dot(u_ref[rows, :], ht_ref[...], preferred_element_type=F32)
        for k1 in range(s * PEER_SLICE_ROWS, (s + 1) * PEER_SLICE_ROWS):
            for c in range(nchunks):
                cols = slice(c * LANES, (c + 1) * LANES)
                g = None
                for h in range(PEER_HEADS):
                    thr = thr_ref[h, k1:k1 + 1, cols]
                    e1 = e1_ref[h, k1:k1 + 1, cols]
                    term = jnp.where(s2_ref[h, :, cols] >= thr, e2_ref[h, :, cols], 0.0) * e1
                    g = term if g is None else g + term
                g_ref[k1 * NK:(k1 + 1) * NK, cols] = g
    for k1 in range(rows_per_tile):
        erows = slice(k1 * NK, (k1 + 1) * NK)
        for c in range(nchunks):
            cols = slice(c * LANES, (c + 1) * LANES)
            w = g_ref[erows, cols] * _gelu_erf(act_ref[erows, cols])
            w_ref[cols, erows] = w.T.astype(w_ref.dtype)


def peer_gate(ht, u, layer, tables, tt=PEER_TOKEN_TILE, te=PEER_EXPERT_TILE):
    D, T = ht.shape
    E = u.shape[1]
    H, NK, _ = tables[0].shape
    rows_per_tile = te // NK
    assert rows_per_tile % SUBLANES == 0
    k1spec = pl.BlockSpec((H, rows_per_tile, tt), lambda i, j: (0, j, i))
    k2spec = pl.BlockSpec((H, NK, tt), lambda i, j: (0, 0, i))
    return pl.pallas_call(
        functools.partial(_peer_gate_kernel, rows_per_tile=rows_per_tile),
        grid=(T // tt, E // te),
        in_specs=[pl.BlockSpec((D, tt), lambda i, j: (0, i)),
                  pl.BlockSpec((None, te, D), lambda i, j: (layer, j, 0)),
                  k1spec, k1spec, k2spec, k2spec],
        out_specs=pl.BlockSpec((tt, te), lambda i, j: (i, j)),
        out_shape=jax.ShapeDtypeStruct((T, E), BF16),
        scratch_shapes=[pltpu.VMEM((te, tt), F32), pltpu.VMEM((te, tt), F32)],
        name="peer_gate",
        compiler_params=_params("parallel", "arbitrary"),
    )(ht, u, *tables)


def peer_ffn(x, layer, norm_g, wq, subkeys, u_tab, v_tab):
    h, ht = rmsnorm(x, norm_g, BF16, with_transpose=True)
    q = matmul(h, wq, layer=layer, name="peer_query")
    tables = peer_topk(q, subkeys)
    w = peer_gate(ht, u_tab, layer, tables)
    return matmul(w, v_tab, res=x, layer=layer, tk=2048, name="peer_out")


def _in_proj_layout(group_width):
    gw = group_width
    names = ('lru_x', 'lru_gate', 'gla_q', 'gla_k', 'gla_v', 'gla_og', 's5_u', 'moba_q', 'moba_k', 'moba_v')
    sizes = (gw, gw, gw // 2, gw // 2, gw, gw, gw, gw, gw, gw)
    cols, c = {}, 0
    for name, size in zip(names, sizes):
        cols[name] = c // LANES
        c += size
    return sum(sizes[:5]), cols


def kernel(x, norm1_g, w_in, lru_conv_w, lru_conv_b, lru_wa, lru_ba, lru_wx, lru_bx, lru_lambda, gla_wg2, gla_bg, gla_norm_g, s5_a_re, s5_a_im, s5_log_step, s5_b_re, s5_b_im, s5_c_re, s5_c_im, s5_d, s5_w_glu, s5_b_glu, w_out, norm2_g, peer_wq, peer_subkeys, peer_u, peer_v, final_norm_g):
    B, S, D = x.shape
    T = B * S
    depth = norm1_g.shape[0]
    gw = D // N_MIXERS
    x = x.reshape(T, D)
    lr0, col = _in_proj_layout(gw)
    w_in_t = jnp.transpose(w_in, (0, 2, 1))
    w_main = jnp.concatenate([w_in_t[:, :lr0], w_in_t[:, lr0 + GLA_GATE_RANK:]], axis=1).astype(BF16)
    w_lr = jnp.zeros((depth, LANES, D), BF16).at[:, :GLA_GATE_RANK].set(
        w_in_t[:, lr0:lr0 + GLA_GATE_RANK].astype(BF16))
    w_out_bf16 = w_out.astype(BF16)
    peer_wq_bf16 = peer_wq.astype(BF16)
    peer_u_bf16 = peer_u.astype(BF16)
    peer_v_bf16 = peer_v.astype(BF16)
    for l in range(depth):
        h = rmsnorm(x, norm1_g[l], BF16)
        pm = matmul(h, w_main, layer=l, w_transposed=True, name="in_proj")
        plr = matmul(h, w_lr, layer=l, w_transposed=True, name="in_proj_lr")
        y_a = rglru(pm, B, S, col['lru_x'], col['lru_gate'], lru_conv_w[l], lru_conv_b[l], lru_wa[l], lru_ba[l],
                    lru_wx[l], lru_bx[l], lru_lambda[l])
        y_b = gla(pm, pm, plr, B, S, col['gla_q'], col['gla_k'], col['gla_v'], col['gla_og'],
                  gla_wg2[l], gla_bg[l], gla_norm_g[l])
        prep = s5_prepare(s5_a_re[l], s5_a_im[l], s5_log_step[l], s5_b_re[l], s5_b_im[l], s5_c_re[l], s5_c_im[l])
        z = s5_ssm(pm, B, S, col['s5_u'], prep, s5_d[l])
        y_c = glu(z, s5_w_glu[l], s5_b_glu[l])
        y_d = moba(pm, B, S, col['moba_q'], col['moba_k'], col['moba_v'])
        mixed = jnp.concatenate([y_a, y_b, y_c, y_d], axis=-1)
        x = matmul(mixed, w_out_bf16, res=x, layer=l, name="out_proj")
        x = peer_ffn(x, l, norm2_g[l], peer_wq_bf16, peer_subkeys[l], peer_u_bf16, peer_v_bf16)
    return rmsnorm(x, final_norm_g, F32).reshape(B, S, D)
```

```python
import functools
import math

import jax
import jax.numpy as jnp
from jax import lax
from jax.experimental import pallas as pl
from jax.experimental.pallas import tpu as pltpu

F32 = jnp.float32
BF16 = jnp.bfloat16

LANES = 128
SUBLANES = 8
VMEM_LIMIT_BYTES = 56 * 2**20

RMS_EPS = 1e-6
N_MIXERS = 4

LRU_BLOCKS = 8
LRU_CONV = 4
LRU_C = 8.0
LRU_ROWS = 256

GLA_HEADS = 4
GLA_GATE_RANK = 16
GLA_TAU = 16.0
GLA_CHUNK = 64
GLA_TILE = 512
GLA_HEADS_PER_STEP = 2

S5_GROUP = 16
S5_STATE = 64
S5_SLAB_GROUPS = LANES // S5_GROUP
S5_SLAB_STATES = S5_SLAB_GROUPS * S5_STATE
S5_TILE = 256

MOBA_HEADS = 8
MOBA_BLOCK = 256
MOBA_TOPK = 3
MOBA_GROUP = 4
MOBA_HEADS_PER_STEP = 2
ROPE_THETA = 500000.0
MASK_VALUE = -1e30

PEER_HEADS = 8
PEER_NKEYS = 128
PEER_TOPK = 16
PEER_TOKEN_TILE = 512
PEER_EXPERT_TILE = 1024
PEER_SLICE_ROWS = 2


def _params(*semantics):
    return pltpu.CompilerParams(dimension_semantics=semantics, vmem_limit_bytes=VMEM_LIMIT_BYTES)


def _rmsnorm_kernel(x_ref, g_ref, o_ref, *maybe_ot_ref):
    x = x_ref[...]
    y = x * lax.rsqrt(jnp.mean(x * x, axis=-1, keepdims=True) + RMS_EPS) * g_ref[...]
    o_ref[...] = y.astype(o_ref.dtype)
    if maybe_ot_ref:
        maybe_ot_ref[0][...] = y.T.astype(maybe_ot_ref[0].dtype)


def rmsnorm(x, g, out_dtype, with_transpose=False, tm=256):
    T, D = x.shape
    out_shape = [jax.ShapeDtypeStruct((T, D), out_dtype)]
    out_specs = [pl.BlockSpec((tm, D), lambda i: (i, 0))]
    if with_transpose:
        out_shape.append(jax.ShapeDtypeStruct((D, T), BF16))
        out_specs.append(pl.BlockSpec((D, tm), lambda i: (0, i)))
    res = pl.pallas_call(
        _rmsnorm_kernel,
        grid=(T // tm,),
        in_specs=[pl.BlockSpec((tm, D), lambda i: (i, 0)), pl.BlockSpec((1, D), lambda i: (0, 0))],
        out_specs=out_specs,
        out_shape=out_shape,
        name="rmsnorm",
        compiler_params=_params("parallel"),
    )(x, g.reshape(1, D))
    return res if with_transpose else res[0]


def _mm_kernel(x_ref, w_ref, *rest, nk, has_res, w_transposed):
    r_ref = rest[0] if has_res else None
    o_ref = rest[1 if has_res else 0]
    contract = (((1,), (1 if w_transposed else 0,)), ((), ()))
    part = lax.dot_general(x_ref[...], w_ref[...], contract, preferred_element_type=F32)
    if nk == 1:
        o_ref[...] = (part + r_ref[...] if has_res else part).astype(o_ref.dtype)
        return
    acc_ref = rest[-1]
    k = pl.program_id(2)

    @pl.when(k == 0)
    def _():
        acc_ref[...] = part

    @pl.when(k > 0)
    def _():
        acc_ref[...] += part

    @pl.when(k == nk - 1)
    def _():
        acc = acc_ref[...]
        o_ref[...] = (acc + r_ref[...] if has_res else acc).astype(o_ref.dtype)


def matmul(x, w, res=None, *, layer=None, w_transposed=False, tm=1024, tn=1024, tk=4096, out_dtype=F32,
           name="matmul"):
    M, K = x.shape
    k_axis, n_axis = (-1, -2) if w_transposed else (-2, -1)
    N = w.shape[n_axis]
    tm, tn, tk = min(tm, M), min(tn, N), min(tk, K)
    assert M % tm == 0 and N % tn == 0 and K % tk == 0 and w.shape[k_axis] == K
    nk = K // tk
    wblock = (tn, tk) if w_transposed else (tk, tn)
    windex = (lambda i, j, k: (j, k)) if w_transposed else (lambda i, j, k: (k, j))
    if layer is None:
        wspec = pl.BlockSpec(wblock, windex)
    else:
        wspec = pl.BlockSpec((None,) + wblock, lambda i, j, k: (layer,) + windex(i, j, k))
    in_specs = [pl.BlockSpec((tm, tk), lambda i, j, k: (i, k)), wspec]
    args = [x, w]
    if res is not None:
        in_specs.append(pl.BlockSpec((tm, tn), lambda i, j, k: (i, j)))
        args.append(res)
    return pl.pallas_call(
        functools.partial(_mm_kernel, nk=nk, has_res=res is not None, w_transposed=w_transposed),
        grid=(M // tm, N // tn, nk),
        in_specs=in_specs,
        out_specs=pl.BlockSpec((tm, tn), lambda i, j, k: (i, j)),
        out_shape=jax.ShapeDtypeStruct((M, N), out_dtype),
        scratch_shapes=[pltpu.VMEM((tm, tn), F32)] if nk > 1 else [],
        name=name,
        compiler_params=_params("parallel", "parallel", "arbitrary"),
    )(*args)


def _lru_kernel(x_ref, gate_ref, cw_ref, cb_ref, wa_ref, ba_ref, wx_ref, bx_ref, lam_ref, o_ref, *, seq):
    R = LRU_ROWS
    cw = cw_ref[...]
    cb = cb_ref[...]
    ba = ba_ref[...]
    bx = bx_ref[...]
    neg_c_softplus = -LRU_C * jax.nn.softplus(-lam_ref[...])
    wa = wa_ref[0]
    wx = wx_ref[0]
    row = lax.broadcasted_iota(jnp.int32, (R, LANES), 0)
    row8 = lax.broadcasted_iota(jnp.int32, (SUBLANES, LANES), 0)

    def body(c, h):
        r0 = pl.multiple_of(c * R, R)
        xt = x_ref[pl.ds(r0, R), :]
        p0 = pl.multiple_of(jnp.maximum(r0 - SUBLANES, 0), SUBLANES)
        prev = jnp.where(c > 0, x_ref[pl.ds(p0, SUBLANES), :], 0.0)
        xc = xt * cw[LRU_CONV - 1:LRU_CONV] + cb
        for d in range(1, LRU_CONV):
            rolled = pltpu.roll(xt, d, 0)
            head = jnp.where(row8 < d, pltpu.roll(prev, d, 0), rolled[:SUBLANES])
            shifted = jnp.concatenate([head, rolled[SUBLANES:]], axis=0)
            xc = xc + shifted * cw[LRU_CONV - 1 - d:LRU_CONV - d]
        xb = xc.astype(BF16)
        r = jax.nn.sigmoid(jnp.dot(xb, wa, preferred_element_type=F32) + ba)
        i = jax.nn.sigmoid(jnp.dot(xb, wx, preferred_element_type=F32) + bx)
        log_a = r * neg_c_softplus
        a = jnp.exp(log_a)
        b = jnp.sqrt(1.0 - a * a) * (i * xc)
        d = 1
        while d < R:
            keep = row >= d
            a_sh = jnp.where(keep, pltpu.roll(a, d, 0), 1.0)
            b_sh = jnp.where(keep, pltpu.roll(b, d, 0), 0.0)
            b = a * b_sh + b
            a = a * a_sh
            d *= 2
        hs = b + a * h
        o_ref[pl.ds(r0, R), :] = (hs * jax.nn.gelu(gate_ref[pl.ds(r0, R), :])).astype(o_ref.dtype)
        return hs[R - 1:R, :]

    lax.fori_loop(0, seq // R, body, jnp.zeros((1, LANES), F32))


def rglru(pm, batch, seq, x_col, gate_col, conv_w, conv_b, wa, ba, wx, bx, lam):
    T = batch * seq
    W = LRU_BLOCKS * LANES
    vec = lambda v: v.reshape(1, W)
    vspec = pl.BlockSpec((1, LANES), lambda b, n: (0, n))
    wspec = pl.BlockSpec((1, LANES, LANES), lambda b, n: (n, 0, 0))
    return pl.pallas_call(
        functools.partial(_lru_kernel, seq=seq),
        grid=(batch, LRU_BLOCKS),
        in_specs=[pl.BlockSpec((seq, LANES), lambda b, n: (b, x_col + n)),
                  pl.BlockSpec((seq, LANES), lambda b, n: (b, gate_col + n)),
                  pl.BlockSpec((LRU_CONV, LANES), lambda b, n: (0, n)),
                  vspec, wspec, vspec, wspec, vspec, vspec],
        out_specs=pl.BlockSpec((seq, LANES), lambda b, n: (b, n)),
        out_shape=jax.ShapeDtypeStruct((T, W), BF16),
        name="rglru",
        compiler_params=_params("parallel", "parallel"),
    )(pm, pm, conv_w, vec(conv_b), wa.astype(BF16), vec(ba), wx.astype(BF16), vec(bx), vec(lam))


def _gla_kernel(q_ref, k_ref, v_ref, og_ref, lr_ref, wg_ref, bg_ref, ng_ref, o_ref, st_ref):
    C = GLA_CHUNK
    NH = GLA_HEADS_PER_STEP
    dk = q_ref.shape[-1] // NH
    dv = v_ref.shape[-1] // NH

    @pl.when(pl.program_id(2) == 0)
    def _():
        st_ref[...] = jnp.zeros_like(st_ref)

    ri = lax.broadcasted_iota(jnp.int32, (C, C), 0)
    ci = lax.broadcasted_iota(jnp.int32, (C, C), 1)
    causal = ri >= ci
    tril = causal.astype(F32)
    scale = dk ** -0.5
    nt = (((1,), (1,)), ((), ()))
    tn = (((0,), (0,)), ((), ()))

    for c in range(GLA_TILE // C):
        rows = pl.ds(c * C, C)
        lr = lr_ref[rows, :]
        for hh in range(NH):
            kcols = slice(hh * dk, (hh + 1) * dk)
            vcols = slice(hh * dv, (hh + 1) * dv)
            q = q_ref[rows, kcols]
            k = k_ref[rows, kcols]
            v = v_ref[rows, vcols].astype(BF16)
            pre = jnp.dot(lr, wg_ref[hh], precision=lax.Precision.HIGHEST, preferred_element_type=F32) + bg_ref[hh]
            g = jax.nn.log_sigmoid(pre) / GLA_TAU
            bc = jnp.dot(tril, g, precision=lax.Precision.HIGHEST, preferred_element_type=F32)
            b_last = bc[C - 1:C, :]
            qe = (q * scale * jnp.exp(bc)).astype(BF16)
            ke = (k * jnp.exp(-bc)).astype(BF16)
            kd = (k * jnp.exp(b_last - bc)).astype(BF16)
            att = lax.dot_general(qe, ke, nt, preferred_element_type=F32)
            att = jnp.where(causal, att, 0.0).astype(BF16)
            st = st_ref[hh]
            o = jnp.dot(att, v, preferred_element_type=F32)
            o = o + lax.dot_general(qe, st.astype(BF16), nt, preferred_element_type=F32)
            st_ref[hh] = st * jnp.exp(b_last) + lax.dot_general(v, kd, tn, preferred_element_type=F32)
            o = o * lax.rsqrt(jnp.mean(o * o, axis=-1, keepdims=True) + RMS_EPS)
            o_ref[rows, vcols] = (o * ng_ref[hh] * jax.nn.silu(og_ref[rows, vcols])).astype(o_ref.dtype)


def gla(pm, pm_og, plr, batch, seq, q_col, k_col, v_col, og_col, wg2, bg, norm_g):
    T = batch * seq
    H = GLA_HEADS
    dk = wg2.shape[-1] // H
    dv = norm_g.shape[-1] // H
    assert dk == LANES and dv == 2 * LANES
    nt = seq // GLA_TILE
    wg = jnp.zeros((LANES, H * dk), F32).at[:GLA_GATE_RANK].set(wg2)
    wg = wg.reshape(LANES, H, dk).transpose(1, 0, 2)
    tok = lambda b, h, t: b * nt + t
    NH = GLA_HEADS_PER_STEP
    kb, vb = NH * dk // LANES, NH * dv // LANES
    assert H % NH == 0 and q_col % kb == 0 and k_col % kb == 0 and v_col % vb == 0 and og_col % vb == 0
    return pl.pallas_call(
        _gla_kernel,
        grid=(batch, H // NH, nt),
        in_specs=[pl.BlockSpec((GLA_TILE, NH * dk), lambda b, h, t: (tok(b, h, t), q_col // kb + h)),
                  pl.BlockSpec((GLA_TILE, NH * dk), lambda b, h, t: (tok(b, h, t), k_col // kb + h)),
                  pl.BlockSpec((GLA_TILE, NH * dv), lambda b, h, t: (tok(b, h, t), v_col // vb + h)),
                  pl.BlockSpec((GLA_TILE, NH * dv), lambda b, h, t: (tok(b, h, t), og_col // vb + h)),
                  pl.BlockSpec((GLA_TILE, LANES), lambda b, h, t: (tok(b, h, t), 0)),
                  pl.BlockSpec((NH, LANES, dk), lambda b, h, t: (h, 0, 0)),
                  pl.BlockSpec((NH, 1, dk), lambda b, h, t: (h, 0, 0)),
                  pl.BlockSpec((NH, 1, dv), lambda b, h, t: (h, 0, 0))],
        out_specs=pl.BlockSpec((GLA_TILE, NH * dv), lambda b, h, t: (tok(b, h, t), h)),
        out_shape=jax.ShapeDtypeStruct((T, H * dv), BF16),
        scratch_shapes=[pltpu.VMEM((NH, dv, dk), F32)],
        name="gla",
        compiler_params=_params("parallel", "parallel", "arbitrary"),
    )(pm, pm, pm, pm_og, plr, wg, bg.reshape(H, 1, dk), norm_g.reshape(H, 1, dv))


def _s5_kernel(u_ref, bcat_ref, ccat_ref, pr_ref, pi_ref, d_ref, z_ref, x_ref, carry_ref):
    TT = S5_TILE
    NS = 2 * S5_SLAB_STATES

    @pl.when(pl.program_id(2) == 0)
    def _():
        carry_ref[...] = jnp.zeros_like(carry_ref)

    u = u_ref[...]
    pr = pr_ref[0]
    pi = pi_ref[0]

    def swap(t):
        return pltpu.roll(t, S5_SLAB_STATES, 1)

    x = jnp.dot(u.astype(BF16), bcat_ref[0], preferred_element_type=F32)
    sub = lax.broadcasted_iota(jnp.int32, (TT, NS), 0) % SUBLANES
    for d in (1, 2, 4):
        sh = jnp.where(sub >= d, pltpu.roll(x, d, 0), 0.0)
        x = x + pr[d - 1:d] * sh + pi[d - 1:d] * swap(sh)
    x_ref[...] = x

    def carry_step(i, carry):
        rows = pl.ds(pl.multiple_of(i * SUBLANES, SUBLANES), SUBLANES)
        cb = jnp.broadcast_to(carry, (SUBLANES, NS))
        blk = x_ref[rows, :] + pr * cb + pi * swap(cb)
        x_ref[rows, :] = blk
        return blk[SUBLANES - 1:SUBLANES, :]

    carry_ref[...] = lax.fori_loop(0, TT // SUBLANES, carry_step, carry_ref[...])
    y = jnp.dot(x_ref[...].astype(BF16), ccat_ref[0], preferred_element_type=F32) + d_ref[...] * u
    z_ref[...] = jax.nn.gelu(y)


def s5_prepare(a_re, a_im, log_step, b_re, b_im, c_re, c_im):
    G, P = a_re.shape
    H = S5_GROUP
    ns = G // S5_SLAB_GROUPS
    step = jnp.exp(log_step)[:, None]
    mag = jnp.exp(a_re * step)
    ang = a_im * step
    abar_re, abar_im = mag * jnp.cos(ang), mag * jnp.sin(ang)
    den = a_re * a_re + a_im * a_im
    f_re = ((abar_re - 1.0) * a_re + abar_im * a_im) / den
    f_im = (abar_im * a_re - (abar_re - 1.0) * a_im) / den
    bb_re = f_re[..., None] * b_re - f_im[..., None] * b_im
    bb_im = f_re[..., None] * b_im + f_im[..., None] * b_re
    eye = jnp.eye(S5_SLAB_GROUPS, dtype=F32)

    def in_slab(bb):
        t = bb.reshape(ns, S5_SLAB_GROUPS, P, H)
        return jnp.einsum('sgph,gk->sghkp', t, eye).reshape(ns, LANES, S5_SLAB_STATES)

    def out_slab(cc):
        t = cc.reshape(ns, S5_SLAB_GROUPS, H, P)
        return jnp.einsum('sghp,gk->sgpkh', t, eye).reshape(ns, S5_SLAB_STATES, LANES)

    bcat = jnp.concatenate([in_slab(bb_re), in_slab(bb_im)], axis=-1).astype(BF16)
    ccat = jnp.concatenate([out_slab(c_re), -out_slab(c_im)], axis=1).astype(BF16)
    pw_re, pw_im = [abar_re], [abar_im]
    for _ in range(SUBLANES - 1):
        r, i = pw_re[-1], pw_im[-1]
        pw_re.append(r * abar_re - i * abar_im)
        pw_im.append(r * abar_im + i * abar_re)
    slab = lambda t: jnp.stack(t, 0).reshape(SUBLANES, ns, S5_SLAB_STATES).transpose(1, 0, 2)
    pr, pi = slab(pw_re), slab(pw_im)
    return bcat, ccat, jnp.concatenate([pr, pr], -1), jnp.concatenate([-pi, pi], -1)


def s5_ssm(pm, batch, seq, u_col, prep, d):
    T = batch * seq
    bcat, ccat, pr, pi = prep
    ns = bcat.shape[0]
    NS = 2 * S5_SLAB_STATES
    nt = seq // S5_TILE
    return pl.pallas_call(
        _s5_kernel,
        grid=(batch, ns, nt),
        in_specs=[pl.BlockSpec((S5_TILE, LANES), lambda b, s, t: (b * nt + t, u_col + s)),
                  pl.BlockSpec((1, LANES, NS), lambda b, s, t: (s, 0, 0)),
                  pl.BlockSpec((1, NS, LANES), lambda b, s, t: (s, 0, 0)),
                  pl.BlockSpec((1, SUBLANES, NS), lambda b, s, t: (s, 0, 0)),
                  pl.BlockSpec((1, SUBLANES, NS), lambda b, s, t: (s, 0, 0)),
                  pl.BlockSpec((1, LANES), lambda b, s, t: (0, s))],
        out_specs=pl.BlockSpec((S5_TILE, LANES), lambda b, s, t: (b * nt + t, s)),
        out_shape=jax.ShapeDtypeStruct((T, ns * LANES), F32),
        scratch_shapes=[pltpu.VMEM((S5_TILE, NS), F32), pltpu.VMEM((1, NS), F32)],
        name="s5_ssm",
        compiler_params=_params("parallel", "parallel", "arbitrary"),
    )(pm, bcat, ccat, pr, pi, d.reshape(1, -1))


def _glu_kernel(z_ref, w_ref, b_ref, o_ref):
    z = z_ref[...]
    y = jnp.dot(z.astype(BF16), w_ref[...], preferred_element_type=F32) + b_ref[...]
    o_ref[...] = (z * jax.nn.sigmoid(y)).astype(o_ref.dtype)


def glu(z, w, b, tm=512):
    T, W = z.shape
    return pl.pallas_call(
        _glu_kernel,
        grid=(T // tm,),
        in_specs=[pl.BlockSpec((tm, W), lambda i: (i, 0)),
                  pl.BlockSpec((W, W), lambda i: (0, 0)),
                  pl.BlockSpec((1, W), lambda i: (0, 0))],
        out_specs=pl.BlockSpec((tm, W), lambda i: (i, 0)),
        out_shape=jax.ShapeDtypeStruct((T, W), BF16),
        name="s5_glu",
        compiler_params=_params("parallel"),
    )(z, w.astype(BF16), b.reshape(1, W))


def _rope(t, cos, sin_lo, sin_hi, rope_half):
    return (t * cos + pltpu.roll(t, LANES - rope_half, 1) * sin_lo + pltpu.roll(t, rope_half, 1) * sin_hi)


def rope_tables(seq, head_dim):
    rope_dims = head_dim // 4
    half = rope_dims // 2
    inv = jnp.power(ROPE_THETA, -jnp.arange(half, dtype=F32) / half)
    ang = jnp.arange(seq).astype(F32)[:, None] * inv[None, :]
    cos, sin = jnp.cos(ang), jnp.sin(ang)
    pad = head_dim - rope_dims
    zeros = jnp.zeros((seq, half), F32)
    cos_t = jnp.concatenate([cos, cos, jnp.ones((seq, pad), F32)], axis=-1)
    sin_lo = jnp.concatenate([-sin, zeros, jnp.zeros((seq, pad), F32)], axis=-1)
    sin_hi = jnp.concatenate([zeros, sin, jnp.zeros((seq, pad), F32)], axis=-1)
    return cos_t, sin_lo, sin_hi, half


def _moba_kv_kernel(k_ref, v_ref, cos_ref, slo_ref, shi_ref, kr_ref, km_ref, vt_ref, *, nb, rope_half):
    for n in range(nb):
        rows = pl.ds(n * MOBA_BLOCK, MOBA_BLOCK)
        kr = _rope(k_ref[rows, :], cos_ref[rows, :], slo_ref[rows, :], shi_ref[rows, :], rope_half)
        kr_ref[rows, :] = kr.astype(kr_ref.dtype)
        km_ref[0, 0, n:n + 1, :] = jnp.mean(kr, axis=0, keepdims=True)
        vt_ref[0, 0, n] = v_ref[rows, :].T.astype(vt_ref.dtype)


def _moba_attn_kernel(q_ref, cos_ref, slo_ref, shi_ref, kr_ref, vt_ref, km_ref, o_ref, sel_ref, acc_ref,
                      *, nb, rope_half):
    BLK = MOBA_BLOCK
    NH = MOBA_HEADS_PER_STEP
    i = pl.program_id(2)
    hd = q_ref.shape[-1] // NH
    log2_scale = hd ** -0.5 * math.log2(math.e)
    blk_id = lax.broadcasted_iota(jnp.int32, (nb, BLK), 0)
    key = lax.broadcasted_iota(jnp.int32, (BLK, BLK), 0)
    qry = lax.broadcasted_iota(jnp.int32, (BLK, BLK), 1)
    own = pl.ds(pl.multiple_of(i * BLK, BLK), BLK)
    cos, slo, shi = cos_ref[...], slo_ref[...], shi_ref[...]
    lanes = [slice(hh * hd, (hh + 1) * hd) for hh in range(NH)]

    qbs, stats = [], []
    for hh in range(NH):
        qt = _rope(q_ref[:, lanes[hh]], cos, slo, shi, rope_half).T
        gate = jnp.dot(km_ref[0, hh], qt, precision=lax.Precision.HIGHEST, preferred_element_type=F32)
        rank = jnp.zeros((nb, BLK), jnp.int32)
        for m in range(nb):
            gm = gate[m:m + 1, :]
            beats = (gm > gate) | ((gm == gate) & (m < blk_id))
            rank = rank + jnp.where(beats & (m < i), 1, 0)
        sel_ref[hh] = ((blk_id < i) & (rank < MOBA_TOPK)).astype(F32)
        qb = (qt * log2_scale).astype(BF16)
        s = jnp.dot(kr_ref[own, lanes[hh]], qb, preferred_element_type=F32)
        s = jnp.where(key <= qry, s, MASK_VALUE)
        m0 = jnp.max(s, axis=0, keepdims=True)
        p = jnp.exp2(s - m0)
        acc_ref[hh] = jnp.dot(vt_ref[0, hh, i], p.astype(BF16), preferred_element_type=F32)
        qbs.append(qb)
        stats += [m0, jnp.sum(p, axis=0, keepdims=True)]

    def body(g, carry):
        out = []
        for hh in range(NH):
            m_prev, l_prev = carry[2 * hh], carry[2 * hh + 1]
            scores = []
            for b in range(MOBA_GROUP):
                n = g * MOBA_GROUP + b
                rows = pl.ds(pl.multiple_of(n * BLK, BLK), BLK)
                sn = jnp.dot(kr_ref[rows, lanes[hh]], qbs[hh], preferred_element_type=F32)
                scores.append(jnp.where(sel_ref[hh, pl.ds(n, 1), :] > 0.0, sn, MASK_VALUE))
            m_new = m_prev
            for sn in scores:
                m_new = jnp.maximum(m_new, jnp.max(sn, axis=0, keepdims=True))
            alpha = jnp.exp2(m_prev - m_new)
            l_new = alpha * l_prev
            acc = alpha * acc_ref[hh]
            for b, sn in enumerate(scores):
                pn = jnp.exp2(sn - m_new)
                l_new = l_new + jnp.sum(pn, axis=0, keepdims=True)
                acc = acc + jnp.dot(vt_ref[0, hh, g * MOBA_GROUP + b], pn.astype(BF16), preferred_element_type=F32)
            acc_ref[hh] = acc
            out += [m_new, l_new]
        return tuple(out)

    groups = (i + MOBA_GROUP - 1) // MOBA_GROUP
    stats = lax.fori_loop(0, groups, body, tuple(stats))
    for hh in range(NH):
        o_ref[:, lanes[hh]] = (acc_ref[hh] / stats[2 * hh + 1]).T.astype(o_ref.dtype)


def moba(pm, batch, seq, q_col, k_col, v_col):
    T = batch * seq
    H, hd, BLK = MOBA_HEADS, LANES, MOBA_BLOCK
    nb = seq // BLK
    assert seq % BLK == 0 and nb % MOBA_GROUP == 0
    cos_t, sin_lo, sin_hi, half = rope_tables(seq, hd)
    full = pl.BlockSpec((seq, hd), lambda b, h: (0, 0))
    kr, kmean, vt = pl.pallas_call(
        functools.partial(_moba_kv_kernel, nb=nb, rope_half=half),
        grid=(batch, H),
        in_specs=[pl.BlockSpec((seq, hd), lambda b, h: (b, k_col + h)),
                  pl.BlockSpec((seq, hd), lambda b, h: (b, v_col + h)), full, full, full],
        out_specs=[pl.BlockSpec((seq, hd), lambda b, h: (b, h)),
                   pl.BlockSpec((1, 1, nb, hd), lambda b, h: (b, h, 0, 0)),
                   pl.BlockSpec((1, 1, nb, hd, BLK), lambda b, h: (b, h, 0, 0, 0))],
        out_shape=[jax.ShapeDtypeStruct((T, H * hd), BF16), jax.ShapeDtypeStruct((batch, H, nb, hd), F32),
                   jax.ShapeDtypeStruct((batch, H, nb, hd, BLK), BF16)],
        name="moba_kv",
        compiler_params=_params("parallel", "parallel"),
    )(pm, pm, cos_t, sin_lo, sin_hi)
    tab = pl.BlockSpec((BLK, hd), lambda b, h, i: (i, 0))
    NH = MOBA_HEADS_PER_STEP
    assert H % NH == 0 and q_col % NH == 0
    return pl.pallas_call(
        functools.partial(_moba_attn_kernel, nb=nb, rope_half=half),
        grid=(batch, H // NH, nb),
        in_specs=[pl.BlockSpec((BLK, NH * hd), lambda b, h, i: (b * nb + i, q_col // NH + h)), tab, tab, tab,
                  pl.BlockSpec((seq, NH * hd), lambda b, h, i: (b, h)),
                  pl.BlockSpec((1, NH, nb, hd, BLK), lambda b, h, i: (b, h, 0, 0, 0)),
                  pl.BlockSpec((1, NH, nb, hd), lambda b, h, i: (b, h, 0, 0))],
        out_specs=pl.BlockSpec((BLK, NH * hd), lambda b, h, i: (b * nb + i, h)),
        out_shape=jax.ShapeDtypeStruct((T, H * hd), BF16),
        scratch_shapes=[pltpu.VMEM((NH, nb, BLK), F32), pltpu.VMEM((NH, hd, BLK), F32)],
        name="moba_attn",
        compiler_params=_params("parallel", "parallel", "arbitrary"),
    )(pm, cos_t, sin_lo, sin_hi, kr, vt, kmean)


def _gelu_erf(x):
    return 0.5 * x * (1.0 + lax.erf(x * (2.0 ** -0.5)))


def _top_values(x, count):
    vals = []
    for _ in range(count):
        m = jnp.max(x, axis=0, keepdims=True)
        vals.append(m)
        x = jnp.where(x >= m, -jnp.inf, x)
    return vals


def _peer_topk_kernel(q_ref, sk_ref, thr_ref, e1_ref, s2_ref, e2_ref):
    nt = (((1,), (1,)), ((), ()))
    K = PEER_TOPK
    q = q_ref[...]
    half = q.shape[-1] // 2
    s1 = lax.dot_general(sk_ref[0, 0], q[:, :half], nt, precision=lax.Precision.HIGHEST, preferred_element_type=F32)
    s2 = lax.dot_general(sk_ref[0, 1], q[:, half:], nt, precision=lax.Precision.HIGHEST, preferred_element_type=F32)
    v1 = _top_values(s1, K)
    v2 = _top_values(s2, K)
    v2_all = jnp.concatenate(v2, axis=0)
    cand = jnp.concatenate([v1[0] + v2_all] + [v1[a] + v2_all[:K // 2] for a in range(1, K)], axis=0)
    tau = _top_values(cand, K)[-1]
    top = v1[0] + v2[0]
    z = jnp.sum(jnp.where(cand >= tau, jnp.exp(cand - top), 0.0), axis=0, keepdims=True)
    thr = jnp.full_like(s1, jnp.inf)
    for b in range(K):
        thr = jnp.where(s1 + v2[b] >= tau, v2[b], thr)
    thr_ref[0] = jnp.where(s1 >= v1[K - 1], thr, jnp.inf)
    e1_ref[0] = jnp.exp(s1 - v1[0])
    s2_ref[0] = s2
    e2_ref[0] = jnp.exp(s2 - v2[0]) / z


def peer_topk(q, subkeys, tt=256):
    T = q.shape[0]
    H, _, NK, dh = subkeys.shape
    assert NK == PEER_NKEYS
    spec = pl.BlockSpec((1, NK, tt), lambda i, h: (h, 0, i))
    shape = jax.ShapeDtypeStruct((H, NK, T), F32)
    return pl.pallas_call(
        _peer_topk_kernel,
        grid=(T // tt, H),
        in_specs=[pl.BlockSpec((tt, 2 * dh), lambda i, h: (i, h)),
                  pl.BlockSpec((1, 2, NK, dh), lambda i, h: (h, 0, 0, 0))],
        out_specs=[spec] * 4,
        out_shape=[shape] * 4,
        name="peer_topk",
        compiler_params=_params("parallel", "parallel"),
    )(q, subkeys)


def _peer_gate_kernel(ht_ref, u_ref, thr_ref, e1_ref, s2_ref, e2_ref, w_ref, act_ref, g_ref, *, rows_per_tile):
    NK = PEER_NKEYS
    nchunks = ht_ref.shape[1] // LANES
    for s in range(rows_per_tile // PEER_SLICE_ROWS):
        rows = slice(s * PEER_SLICE_ROWS * NK, (s + 1) * PEER_SLICE_ROWS * NK)
        act_ref[rows, :] = jnp.dot(u_ref[rows, :], ht_ref[...], preferred_element_type=F32)
        for k1 in range(s * PEER_SLICE_ROWS, (s + 1) * PEER_SLICE_ROWS):
            for c in range(nchunks):
                cols = slice(c * LANES, (c + 1) * LANES)
                g = None
                for h in range(PEER_HEADS):
                    thr = thr_ref[h, k1:k1 + 1, cols]
                    e1 = e1_ref[h, k1:k1 + 1, cols]
                    term = jnp.where(s2_ref[h, :, cols] >= thr, e2_ref[h, :, cols], 0.0) * e1
                    g = term if g is None else g + term
                g_ref[k1 * NK:(k1 + 1) * NK, cols] = g
    for k1 in range(rows_per_tile):
        erows = slice(k1 * NK, (k1 + 1) * NK)
        for c in range(nchunks):
            cols = slice(c * LANES, (c + 1) * LANES)
            w = g_ref[erows, cols] * _gelu_erf(act_ref[erows, cols])
            w_ref[cols, erows] = w.T.astype(w_ref.dtype)


def peer_gate(ht, u, layer, tables, tt=PEER_TOKEN_TILE, te=PEER_EXPERT_TILE):
    D, T = ht.shape
    E = u.shape[1]
    H, NK, _ = tables[0].shape
    rows_per_tile = te // NK
    assert rows_per_tile % SUBLANES == 0
    k1spec = pl.BlockSpec((H, rows_per_tile, tt), lambda i, j: (0, j, i))
    k2spec = pl.BlockSpec((H, NK, tt), lambda i, j: (0, 0, i))
    return pl.pallas_call(
        functools.partial(_peer_gate_kernel, rows_per_tile=rows_per_tile),
        grid=(T // tt, E // te),
        in_specs=[pl.BlockSpec((D, tt), lambda i, j: (0, i)),
                  pl.BlockSpec((None, te, D), lambda i, j: (layer, j, 0)),
                  k1spec, k1spec, k2spec, k2spec],
        out_specs=pl.BlockSpec((tt, te), lambda i, j: (i, j)),
        out_shape=jax.ShapeDtypeStruct((T, E), BF16),
        scratch_shapes=[pltpu.VMEM((te, tt), F32), pltpu.VMEM((te, tt), F32)],
        name="peer_gate",
        compiler_params=_params("parallel", "arbitrary"),
    )(ht, u, *tables)


def peer_ffn(x, layer, norm_g, wq, subkeys, u_tab, v_tab):
    h, ht = rmsnorm(x, norm_g, BF16, with_transpose=True)
    q = matmul(h, wq, layer=layer, name="peer_query")
    tables = peer_topk(q, subkeys)
    w = peer_gate(ht, u_tab, layer, tables)
    return matmul(w, v_tab, res=x, layer=layer, tk=2048, name="peer_out")


def _in_proj_layout(group_width):
    gw = group_width
    names = ('lru_x', 'lru_gate', 'gla_q', 'gla_k', 'gla_v', 'gla_og', 's5_u', 'moba_q', 'moba_k', 'moba_v')
    sizes = (gw, gw, gw // 2, gw // 2, gw, gw, gw, gw, gw, gw)
    cols, c = {}, 0
    for name, size in zip(names, sizes):
        cols[name] = c // LANES
        c += size
    return sum(sizes[:5]), cols


def kernel(x, norm1_g, w_in, lru_conv_w, lru_conv_b, lru_wa, lru_ba, lru_wx, lru_bx, lru_lambda, gla_wg2, gla_bg, gla_norm_g, s5_a_re, s5_a_im, s5_log_step, s5_b_re, s5_b_im, s5_c_re, s5_c_im, s5_d, s5_w_glu, s5_b_glu, w_out, norm2_g, peer_wq, peer_subkeys, peer_u, peer_v, final_norm_g):
    B, S, D = x.shape
    T = B * S
    depth = norm1_g.shape[0]
    gw = D // N_MIXERS
    x = x.reshape(T, D)
    lr0, col = _in_proj_layout(gw)
    w_in_t = jnp.transpose(w_in, (0, 2, 1))
    w_main = jnp.concatenate([w_in_t[:, :lr0], w_in_t[:, lr0 + GLA_GATE_RANK:]], axis=1).astype(BF16)
    w_lr = jnp.zeros((depth, LANES, D), BF16).at[:, :GLA_GATE_RANK].set(
        w_in_t[:, lr0:lr0 + GLA_GATE_RANK].astype(BF16))
    w_out_bf16 = w_out.astype(BF16)
    peer_wq_bf16 = peer_wq.astype(BF16)
    peer_u_bf16 = peer_u.astype(BF16)
    peer_v_bf16 = peer_v.astype(BF16)
    for l in range(depth):
        h = rmsnorm(x, norm1_g[l], BF16)
        pm = matmul(h, w_main, layer=l, w_transposed=True, name="in_proj")
        plr = matmul(h, w_lr, layer=l, w_transposed=True, name="in_proj_lr")
        y_a = rglru(pm, B, S, col['lru_x'], col['lru_gate'], lru_conv_w[l], lru_conv_b[l], lru_wa[l], lru_ba[l],
                    lru_wx[l], lru_bx[l], lru_lambda[l])
        y_b = gla(pm, pm, plr, B, S, col['gla_q'], col['gla_k'], col['gla_v'], col['gla_og'],
                  gla_wg2[l], gla_bg[l], gla_norm_g[l])
        prep = s5_prepare(s5_a_re[l], s5_a_im[l], s5_log_step[l], s5_b_re[l], s5_b_im[l], s5_c_re[l], s5_c_im[l])
        z = s5_ssm(pm, B, S, col['s5_u'], prep, s5_d[l])
        y_c = glu(z, s5_w_glu[l], s5_b_glu[l])
        y_d = moba(pm, B, S, col['moba_q'], col['moba_k'], col['moba_v'])
        mixed = jnp.concatenate([y_a, y_b, y_c, y_d], axis=-1)
        x = matmul(mixed, w_out_bf16, res=x, layer=l, name="out_proj")
        x = peer_ffn(x, l, norm2_g[l], peer_wq_bf16, peer_subkeys[l], peer_u_bf16, peer_v_bf16)
    return rmsnorm(x, final_norm_g, F32).reshape(B, S, D)
```

```python
import functools
import math

import jax
import jax.numpy as jnp
from jax import lax
from jax.experimental import pallas as pl
from jax.experimental.pallas import tpu as pltpu

F32 = jnp.float32
BF16 = jnp.bfloat16
FP8 = jnp.float8_e4m3fn
FP8_TARGET_MAX = 256.0
FP8_TINY = 1e-30

LANES = 128
SUBLANES = 8
VMEM_LIMIT_BYTES = 56 * 2**20

RMS_EPS = 1e-6
N_MIXERS = 4

LRU_BLOCKS = 8
LRU_CONV = 4
LRU_C = 8.0
LRU_ROWS = 256

GLA_HEADS = 4
GLA_GATE_RANK = 16
GLA_TAU = 16.0
GLA_CHUNK = 64
GLA_TILE = 512
GLA_HEADS_PER_STEP = 2

S5_GROUP = 16
S5_STATE = 64
S5_SLAB_GROUPS = LANES // S5_GROUP
S5_SLAB_STATES = S5_SLAB_GROUPS * S5_STATE
S5_TILE = 256

MOBA_HEADS = 8
MOBA_BLOCK = 256
MOBA_TOPK = 3
MOBA_GROUP = 4
MOBA_HEADS_PER_STEP = 2
ROPE_THETA = 500000.0
MASK_VALUE = -1e30

PEER_HEADS = 8
PEER_NKEYS = 128
PEER_TOPK = 16
PEER_TOKEN_TILE = 512
PEER_EXPERT_TILE = 1024
PEER_SLICE_ROWS = 2


def _params(*semantics):
    return pltpu.CompilerParams(dimension_semantics=semantics, vmem_limit_bytes=VMEM_LIMIT_BYTES)


def _rmsnorm_kernel(x_ref, g_ref, o_ref, *maybe_quantised):
    x = x_ref[...]
    y = x * lax.rsqrt(jnp.mean(x * x, axis=-1, keepdims=True) + RMS_EPS) * g_ref[...]
    o_ref[...] = y.astype(o_ref.dtype)
    if maybe_quantised:
        qt_ref, scale_ref = maybe_quantised
        yt = y.T
        amax = jnp.maximum(jnp.max(jnp.abs(yt), axis=0, keepdims=True), FP8_TINY)
        qt_ref[...] = (yt * (FP8_TARGET_MAX / amax)).astype(qt_ref.dtype)
        scale_ref[...] = amax * (1.0 / FP8_TARGET_MAX)


def rmsnorm(x, g, out_dtype, with_fp8_transpose=False, tm=256):
    T, D = x.shape
    out_shape = [jax.ShapeDtypeStruct((T, D), out_dtype)]
    out_specs = [pl.BlockSpec((tm, D), lambda i: (i, 0))]
    with_transpose = with_fp8_transpose
    if with_transpose:
        out_shape += [jax.ShapeDtypeStruct((D, T), FP8), jax.ShapeDtypeStruct((1, T), F32)]
        out_specs += [pl.BlockSpec((D, tm), lambda i: (0, i)), pl.BlockSpec((1, tm), lambda i: (0, i))]
    res = pl.pallas_call(
        _rmsnorm_kernel,
        grid=(T // tm,),
        in_specs=[pl.BlockSpec((tm, D), lambda i: (i, 0)), pl.BlockSpec((1, D), lambda i: (0, 0))],
        out_specs=out_specs,
        out_shape=out_shape,
        name="rmsnorm",
        compiler_params=_params("parallel"),
    )(x, g.reshape(1, D))
    return res if with_transpose else res[0]


def _mm_kernel(x_ref, w_ref, *rest, nk, has_res, w_transposed):
    r_ref = rest[0] if has_res else None
    o_ref = rest[1 if has_res else 0]
    contract = (((1,), (1 if w_transposed else 0,)), ((), ()))
    part = lax.dot_general(x_ref[...], w_ref[...], contract, preferred_element_type=F32)
    if nk == 1:
        o_ref[...] = (part + r_ref[...] if has_res else part).astype(o_ref.dtype)
        return
    acc_ref = rest[-1]
    k = pl.program_id(2)

    @pl.when(k == 0)
    def _():
        acc_ref[...] = part

    @pl.when(k > 0)
    def _():
        acc_ref[...] += part

    @pl.when(k == nk - 1)
    def _():
        acc = acc_ref[...]
        o_ref[...] = (acc + r_ref[...] if has_res else acc).astype(o_ref.dtype)


def matmul(x, w, res=None, *, layer=None, w_transposed=False, tm=1024, tn=1024, tk=4096, out_dtype=F32,
           name="matmul"):
    M, K = x.shape
    k_axis, n_axis = (-1, -2) if w_transposed else (-2, -1)
    N = w.shape[n_axis]
    tm, tn, tk = min(tm, M), min(tn, N), min(tk, K)
    assert M % tm == 0 and N % tn == 0 and K % tk == 0 and w.shape[k_axis] == K
    nk = K // tk
    wblock = (tn, tk) if w_transposed else (tk, tn)
    windex = (lambda i, j, k: (j, k)) if w_transposed else (lambda i, j, k: (k, j))
    if layer is None:
        wspec = pl.BlockSpec(wblock, windex)
    else:
        wspec = pl.BlockSpec((None,) + wblock, lambda i, j, k: (layer,) + windex(i, j, k))
    in_specs = [pl.BlockSpec((tm, tk), lambda i, j, k: (i, k)), wspec]
    args = [x, w]
    if res is not None:
        in_specs.append(pl.BlockSpec((tm, tn), lambda i, j, k: (i, j)))
        args.append(res)
    return pl.pallas_call(
        functools.partial(_mm_kernel, nk=nk, has_res=res is not None, w_transposed=w_transposed),
        grid=(M // tm, N // tn, nk),
        in_specs=in_specs,
        out_specs=pl.BlockSpec((tm, tn), lambda i, j, k: (i, j)),
        out_shape=jax.ShapeDtypeStruct((M, N), out_dtype),
        scratch_shapes=[pltpu.VMEM((tm, tn), F32)] if nk > 1 else [],
        name=name,
        compiler_params=_params("parallel", "parallel", "arbitrary"),
    )(*args)


def _lru_kernel(x_ref, gate_ref, cw_ref, cb_ref, wa_ref, ba_ref, wx_ref, bx_ref, lam_ref, o_ref, *, seq):
    R = LRU_ROWS
    cw = cw_ref[...]
    cb = cb_ref[...]
    ba = ba_ref[...]
    bx = bx_ref[...]
    neg_c_softplus = -LRU_C * jax.nn.softplus(-lam_ref[...])
    wa = wa_ref[0]
    wx = wx_ref[0]
    row = lax.broadcasted_iota(jnp.int32, (R, LANES), 0)
    row8 = lax.broadcasted_iota(jnp.int32, (SUBLANES, LANES), 0)

    def body(c, h):
        r0 = pl.multiple_of(c * R, R)
        xt = x_ref[pl.ds(r0, R), :]
        p0 = pl.multiple_of(jnp.maximum(r0 - SUBLANES, 0), SUBLANES)
        prev = jnp.where(c > 0, x_ref[pl.ds(p0, SUBLANES), :], 0.0)
        xc = xt * cw[LRU_CONV - 1:LRU_CONV] + cb
        for d in range(1, LRU_CONV):
            rolled = pltpu.roll(xt, d, 0)
            head = jnp.where(row8 < d, pltpu.roll(prev, d, 0), rolled[:SUBLANES])
            shifted = jnp.concatenate([head, rolled[SUBLANES:]], axis=0)
            xc = xc + shifted * cw[LRU_CONV - 1 - d:LRU_CONV - d]
        xb = xc.astype(BF16)
        r = jax.nn.sigmoid(jnp.dot(xb, wa, preferred_element_type=F32) + ba)
        i = jax.nn.sigmoid(jnp.dot(xb, wx, preferred_element_type=F32) + bx)
        log_a = r * neg_c_softplus
        a = jnp.exp(log_a)
        b = jnp.sqrt(1.0 - a * a) * (i * xc)
        d = 1
        while d < R:
            keep = row >= d
            a_sh = jnp.where(keep, pltpu.roll(a, d, 0), 1.0)
            b_sh = jnp.where(keep, pltpu.roll(b, d, 0), 0.0)
            b = a * b_sh + b
            a = a * a_sh
            d *= 2
        hs = b + a * h
        o_ref[pl.ds(r0, R), :] = (hs * jax.nn.gelu(gate_ref[pl.ds(r0, R), :])).astype(o_ref.dtype)
        return hs[R - 1:R, :]

    lax.fori_loop(0, seq // R, body, jnp.zeros((1, LANES), F32))


def rglru(pm, batch, seq, x_col, gate_col, conv_w, conv_b, wa, ba, wx, bx, lam):
    T = batch * seq
    W = LRU_BLOCKS * LANES
    vec = lambda v: v.reshape(1, W)
    vspec = pl.BlockSpec((1, LANES), lambda b, n: (0, n))
    wspec = pl.BlockSpec((1, LANES, LANES), lambda b, n: (n, 0, 0))
    return pl.pallas_call(
        functools.partial(_lru_kernel, seq=seq),
        grid=(batch, LRU_BLOCKS),
        in_specs=[pl.BlockSpec((seq, LANES), lambda b, n: (b, x_col + n)),
                  pl.BlockSpec((seq, LANES), lambda b, n: (b, gate_col + n)),
                  pl.BlockSpec((LRU_CONV, LANES), lambda b, n: (0, n)),
                  vspec, wspec, vspec, wspec, vspec, vspec],
        out_specs=pl.BlockSpec((seq, LANES), lambda b, n: (b, n)),
        out_shape=jax.ShapeDtypeStruct((T, W), BF16),
        name="rglru",
        compiler_params=_params("parallel", "parallel"),
    )(pm, pm, conv_w, vec(conv_b), wa.astype(BF16), vec(ba), wx.astype(BF16), vec(bx), vec(lam))


def _gla_kernel(q_ref, k_ref, v_ref, og_ref, lr_ref, wg_ref, bg_ref, ng_ref, o_ref, st_ref):
    C = GLA_CHUNK
    NH = GLA_HEADS_PER_STEP
    dk = q_ref.shape[-1] // NH
    dv = v_ref.shape[-1] // NH

    @pl.when(pl.program_id(2) == 0)
    def _():
        st_ref[...] = jnp.zeros_like(st_ref)

    ri = lax.broadcasted_iota(jnp.int32, (C, C), 0)
    ci = lax.broadcasted_iota(jnp.int32, (C, C), 1)
    causal = ri >= ci
    tril = causal.astype(F32)
    scale = dk ** -0.5
    nt = (((1,), (1,)), ((), ()))
    tn = (((0,), (0,)), ((), ()))

    for c in range(GLA_TILE // C):
        rows = pl.ds(c * C, C)
        lr = lr_ref[rows, :]
        for hh in range(NH):
            kcols = slice(hh * dk, (hh + 1) * dk)
            vcols = slice(hh * dv, (hh + 1) * dv)
            q = q_ref[rows, kcols]
            k = k_ref[rows, kcols]
            v = v_ref[rows, vcols].astype(BF16)
            pre = jnp.dot(lr, wg_ref[hh], precision=lax.Precision.HIGHEST, preferred_element_type=F32) + bg_ref[hh]
            g = jax.nn.log_sigmoid(pre) / GLA_TAU
            bc = jnp.dot(tril, g, precision=lax.Precision.HIGHEST, preferred_element_type=F32)
            b_last = bc[C - 1:C, :]
            qe = (q * scale * jnp.exp(bc)).astype(BF16)
            ke = (k * jnp.exp(-bc)).astype(BF16)
            kd = (k * jnp.exp(b_last - bc)).astype(BF16)
            att = lax.dot_general(qe, ke, nt, preferred_element_type=F32)
            att = jnp.where(causal, att, 0.0).astype(BF16)
            st = st_ref[hh]
            o = jnp.dot(att, v, preferred_element_type=F32)
            o = o + lax.dot_general(qe, st.astype(BF16), nt, preferred_element_type=F32)
            st_ref[hh] = st * jnp.exp(b_last) + lax.dot_general(v, kd, tn, preferred_element_type=F32)
            o = o * lax.rsqrt(jnp.mean(o * o, axis=-1, keepdims=True) + RMS_EPS)
            o_ref[rows, vcols] = (o * ng_ref[hh] * jax.nn.silu(og_ref[rows, vcols])).astype(o_ref.dtype)


def gla(pm, pm_og, plr, batch, seq, q_col, k_col, v_col, og_col, wg2, bg, norm_g):
    T = batch * seq
    H = GLA_HEADS
    dk = wg2.shape[-1] // H
    dv = norm_g.shape[-1] // H
    assert dk == LANES and dv == 2 * LANES
    nt = seq // GLA_TILE
    wg = jnp.zeros((LANES, H * dk), F32).at[:GLA_GATE_RANK].set(wg2)
    wg = wg.reshape(LANES, H, dk).transpose(1, 0, 2)
    tok = lambda b, h, t: b * nt + t
    NH = GLA_HEADS_PER_STEP
    kb, vb = NH * dk // LANES, NH * dv // LANES
    assert H % NH == 0 and q_col % kb == 0 and k_col % kb == 0 and v_col % vb == 0 and og_col % vb == 0
    return pl.pallas_call(
        _gla_kernel,
        grid=(batch, H // NH, nt),
        in_specs=[pl.BlockSpec((GLA_TILE, NH * dk), lambda b, h, t: (tok(b, h, t), q_col // kb + h)),
                  pl.BlockSpec((GLA_TILE, NH * dk), lambda b, h, t: (tok(b, h, t), k_col // kb + h)),
                  pl.BlockSpec((GLA_TILE, NH * dv), lambda b, h, t: (tok(b, h, t), v_col // vb + h)),
                  pl.BlockSpec((GLA_TILE, NH * dv), lambda b, h, t: (tok(b, h, t), og_col // vb + h)),
                  pl.BlockSpec((GLA_TILE, LANES), lambda b, h, t: (tok(b, h, t), 0)),
                  pl.BlockSpec((NH, LANES, dk), lambda b, h, t: (h, 0, 0)),
                  pl.BlockSpec((NH, 1, dk), lambda b, h, t: (h, 0, 0)),
                  pl.BlockSpec((NH, 1, dv), lambda b, h, t: (h, 0, 0))],
        out_specs=pl.BlockSpec((GLA_TILE, NH * dv), lambda b, h, t: (tok(b, h, t), h)),
        out_shape=jax.ShapeDtypeStruct((T, H * dv), BF16),
        scratch_shapes=[pltpu.VMEM((NH, dv, dk), F32)],
        name="gla",
        compiler_params=_params("parallel", "parallel", "arbitrary"),
    )(pm, pm, pm, pm_og, plr, wg, bg.reshape(H, 1, dk), norm_g.reshape(H, 1, dv))


def _s5_kernel(u_ref, bcat_ref, ccat_ref, pr_ref, pi_ref, d_ref, z_ref, x_ref, carry_ref):
    TT = S5_TILE
    NS = 2 * S5_SLAB_STATES

    @pl.when(pl.program_id(2) == 0)
    def _():
        carry_ref[...] = jnp.zeros_like(carry_ref)

    u = u_ref[...]
    pr = pr_ref[0]
    pi = pi_ref[0]

    def swap(t):
        return pltpu.roll(t, S5_SLAB_STATES, 1)

    x = jnp.dot(u.astype(BF16), bcat_ref[0], preferred_element_type=F32)
    sub = lax.broadcasted_iota(jnp.int32, (TT, NS), 0) % SUBLANES
    for d in (1, 2, 4):
        sh = jnp.where(sub >= d, pltpu.roll(x, d, 0), 0.0)
        x = x + pr[d - 1:d] * sh + pi[d - 1:d] * swap(sh)
    x_ref[...] = x

    def carry_step(i, carry):
        rows = pl.ds(pl.multiple_of(i * SUBLANES, SUBLANES), SUBLANES)
        cb = jnp.broadcast_to(carry, (SUBLANES, NS))
        blk = x_ref[rows, :] + pr * cb + pi * swap(cb)
        x_ref[rows, :] = blk
        return blk[SUBLANES - 1:SUBLANES, :]

    carry_ref[...] = lax.fori_loop(0, TT // SUBLANES, carry_step, carry_ref[...])
    y = jnp.dot(x_ref[...].astype(BF16), ccat_ref[0], preferred_element_type=F32) + d_ref[...] * u
    z_ref[...] = jax.nn.gelu(y)


def s5_prepare(a_re, a_im, log_step, b_re, b_im, c_re, c_im):
    G, P = a_re.shape
    H = S5_GROUP
    ns = G // S5_SLAB_GROUPS
    step = jnp.exp(log_step)[:, None]
    mag = jnp.exp(a_re * step)
    ang = a_im * step
    abar_re, abar_im = mag * jnp.cos(ang), mag * jnp.sin(ang)
    den = a_re * a_re + a_im * a_im
    f_re = ((abar_re - 1.0) * a_re + abar_im * a_im) / den
    f_im = (abar_im * a_re - (abar_re - 1.0) * a_im) / den
    bb_re = f_re[..., None] * b_re - f_im[..., None] * b_im
    bb_im = f_re[..., None] * b_im + f_im[..., None] * b_re
    eye = jnp.eye(S5_SLAB_GROUPS, dtype=F32)

    def in_slab(bb):
        t = bb.reshape(ns, S5_SLAB_GROUPS, P, H)
        return jnp.einsum('sgph,gk->sghkp', t, eye).reshape(ns, LANES, S5_SLAB_STATES)

    def out_slab(cc):
        t = cc.reshape(ns, S5_SLAB_GROUPS, H, P)
        return jnp.einsum('sghp,gk->sgpkh', t, eye).reshape(ns, S5_SLAB_STATES, LANES)

    bcat = jnp.concatenate([in_slab(bb_re), in_slab(bb_im)], axis=-1).astype(BF16)
    ccat = jnp.concatenate([out_slab(c_re), -out_slab(c_im)], axis=1).astype(BF16)
    pw_re, pw_im = [abar_re], [abar_im]
    for _ in range(SUBLANES - 1):
        r, i = pw_re[-1], pw_im[-1]
        pw_re.append(r * abar_re - i * abar_im)
        pw_im.append(r * abar_im + i * abar_re)
    slab = lambda t: jnp.stack(t, 0).reshape(SUBLANES, ns, S5_SLAB_STATES).transpose(1, 0, 2)
    pr, pi = slab(pw_re), slab(pw_im)
    return bcat, ccat, jnp.concatenate([pr, pr], -1), jnp.concatenate([-pi, pi], -1)


def s5_ssm(pm, batch, seq, u_col, prep, d):
    T = batch * seq
    bcat, ccat, pr, pi = prep
    ns = bcat.shape[0]
    NS = 2 * S5_SLAB_STATES
    nt = seq // S5_TILE
    return pl.pallas_call(
        _s5_kernel,
        grid=(batch, ns, nt),
        in_specs=[pl.BlockSpec((S5_TILE, LANES), lambda b, s, t: (b * nt + t, u_col + s)),
                  pl.BlockSpec((1, LANES, NS), lambda b, s, t: (s, 0, 0)),
                  pl.BlockSpec((1, NS, LANES), lambda b, s, t: (s, 0, 0)),
                  pl.BlockSpec((1, SUBLANES, NS), lambda b, s, t: (s, 0, 0)),
                  pl.BlockSpec((1, SUBLANES, NS), lambda b, s, t: (s, 0, 0)),
                  pl.BlockSpec((1, LANES), lambda b, s, t: (0, s))],
        out_specs=pl.BlockSpec((S5_TILE, LANES), lambda b, s, t: (b * nt + t, s)),
        out_shape=jax.ShapeDtypeStruct((T, ns * LANES), F32),
        scratch_shapes=[pltpu.VMEM((S5_TILE, NS), F32), pltpu.VMEM((1, NS), F32)],
        name="s5_ssm",
        compiler_params=_params("parallel", "parallel", "arbitrary"),
    )(pm, bcat, ccat, pr, pi, d.reshape(1, -1))


def _glu_kernel(z_ref, w_ref, b_ref, o_ref):
    z = z_ref[...]
    y = jnp.dot(z.astype(BF16), w_ref[...], preferred_element_type=F32) + b_ref[...]
    o_ref[...] = (z * jax.nn.sigmoid(y)).astype(o_ref.dtype)


def glu(z, w, b, tm=512):
    T, W = z.shape
    return pl.pallas_call(
        _glu_kernel,
        grid=(T // tm,),
        in_specs=[pl.BlockSpec((tm, W), lambda i: (i, 0)),
                  pl.BlockSpec((W, W), lambda i: (0, 0)),
                  pl.BlockSpec((1, W), lambda i: (0, 0))],
        out_specs=pl.BlockSpec((tm, W), lambda i: (i, 0)),
        out_shape=jax.ShapeDtypeStruct((T, W), BF16),
        name="s5_glu",
        compiler_params=_params("parallel"),
    )(z, w.astype(BF16), b.reshape(1, W))


def _rope(t, cos, sin_lo, sin_hi, rope_half):
    return (t * cos + pltpu.roll(t, LANES - rope_half, 1) * sin_lo + pltpu.roll(t, rope_half, 1) * sin_hi)


def rope_tables(seq, head_dim):
    rope_dims = head_dim // 4
    half = rope_dims // 2
    inv = jnp.power(ROPE_THETA, -jnp.arange(half, dtype=F32) / half)
    ang = jnp.arange(seq).astype(F32)[:, None] * inv[None, :]
    cos, sin = jnp.cos(ang), jnp.sin(ang)
    pad = head_dim - rope_dims
    zeros = jnp.zeros((seq, half), F32)
    cos_t = jnp.concatenate([cos, cos, jnp.ones((seq, pad), F32)], axis=-1)
    sin_lo = jnp.concatenate([-sin, zeros, jnp.zeros((seq, pad), F32)], axis=-1)
    sin_hi = jnp.concatenate([zeros, sin, jnp.zeros((seq, pad), F32)], axis=-1)
    return cos_t, sin_lo, sin_hi, half


def _moba_kv_kernel(k_ref, v_ref, cos_ref, slo_ref, shi_ref, kr_ref, km_ref, vt_ref, *, nb, rope_half):
    for n in range(nb):
        rows = pl.ds(n * MOBA_BLOCK, MOBA_BLOCK)
        kr = _rope(k_ref[rows, :], cos_ref[rows, :], slo_ref[rows, :], shi_ref[rows, :], rope_half)
        kr_ref[rows, :] = kr.astype(kr_ref.dtype)
        km_ref[0, 0, n:n + 1, :] = jnp.mean(kr, axis=0, keepdims=True)
        vt_ref[0, 0, n] = v_ref[rows, :].T.astype(vt_ref.dtype)


def _moba_attn_kernel(q_ref, cos_ref, slo_ref, shi_ref, kr_ref, vt_ref, km_ref, o_ref, sel_ref, acc_ref,
                      *, nb, rope_half):
    BLK = MOBA_BLOCK
    NH = MOBA_HEADS_PER_STEP
    i = pl.program_id(2)
    hd = q_ref.shape[-1] // NH
    log2_scale = hd ** -0.5 * math.log2(math.e)
    blk_id = lax.broadcasted_iota(jnp.int32, (nb, BLK), 0)
    key = lax.broadcasted_iota(jnp.int32, (BLK, BLK), 0)
    qry = lax.broadcasted_iota(jnp.int32, (BLK, BLK), 1)
    own = pl.ds(pl.multiple_of(i * BLK, BLK), BLK)
    cos, slo, shi = cos_ref[...], slo_ref[...], shi_ref[...]
    lanes = [slice(hh * hd, (hh + 1) * hd) for hh in range(NH)]

    qbs, stats = [], []
    for hh in range(NH):
        qt = _rope(q_ref[:, lanes[hh]], cos, slo, shi, rope_half).T
        gate = jnp.dot(km_ref[0, hh], qt, precision=lax.Precision.HIGHEST, preferred_element_type=F32)
        rank = jnp.zeros((nb, BLK), jnp.int32)
        for m in range(nb):
            gm = gate[m:m + 1, :]
            beats = (gm > gate) | ((gm == gate) & (m < blk_id))
            rank = rank + jnp.where(beats & (m < i), 1, 0)
        sel_ref[hh] = ((blk_id < i) & (rank < MOBA_TOPK)).astype(F32)
        qb = (qt * log2_scale).astype(BF16)
        s = jnp.dot(kr_ref[own, lanes[hh]], qb, preferred_element_type=F32)
        s = jnp.where(key <= qry, s, MASK_VALUE)
        m0 = jnp.max(s, axis=0, keepdims=True)
        p = jnp.exp2(s - m0)
        acc_ref[hh] = jnp.dot(vt_ref[0, hh, i], p.astype(BF16), preferred_element_type=F32)
        qbs.append(qb)
        stats += [m0, jnp.sum(p, axis=0, keepdims=True)]

    def body(g, carry):
        out = []
        for hh in range(NH):
            m_prev, l_prev = carry[2 * hh], carry[2 * hh + 1]
            scores = []
            for b in range(MOBA_GROUP):
                n = g * MOBA_GROUP + b
                rows = pl.ds(pl.multiple_of(n * BLK, BLK), BLK)
                sn = jnp.dot(kr_ref[rows, lanes[hh]], qbs[hh], preferred_element_type=F32)
                scores.append(jnp.where(sel_ref[hh, pl.ds(n, 1), :] > 0.0, sn, MASK_VALUE))
            m_new = m_prev
            for sn in scores:
                m_new = jnp.maximum(m_new, jnp.max(sn, axis=0, keepdims=True))
            alpha = jnp.exp2(m_prev - m_new)
            l_new = alpha * l_prev
            acc = alpha * acc_ref[hh]
            for b, sn in enumerate(scores):
                pn = jnp.exp2(sn - m_new)
                l_new = l_new + jnp.sum(pn, axis=0, keepdims=True)
                acc = acc + jnp.dot(vt_ref[0, hh, g * MOBA_GROUP + b], pn.astype(BF16), preferred_element_type=F32)
            acc_ref[hh] = acc
            out += [m_new, l_new]
        return tuple(out)

    groups = (i + MOBA_GROUP - 1) // MOBA_GROUP
    stats = lax.fori_loop(0, groups, body, tuple(stats))
    for hh in range(NH):
        o_ref[:, lanes[hh]] = (acc_ref[hh] / stats[2 * hh + 1]).T.astype(o_ref.dtype)


def moba(pm, batch, seq, q_col, k_col, v_col):
    T = batch * seq
    H, hd, BLK = MOBA_HEADS, LANES, MOBA_BLOCK
    nb = seq // BLK
    assert seq % BLK == 0 and nb % MOBA_GROUP == 0
    cos_t, sin_lo, sin_hi, half = rope_tables(seq, hd)
    full = pl.BlockSpec((seq, hd), lambda b, h: (0, 0))
    kr, kmean, vt = pl.pallas_call(
        functools.partial(_moba_kv_kernel, nb=nb, rope_half=half),
        grid=(batch, H),
        in_specs=[pl.BlockSpec((seq, hd), lambda b, h: (b, k_col + h)),
                  pl.BlockSpec((seq, hd), lambda b, h: (b, v_col + h)), full, full, full],
        out_specs=[pl.BlockSpec((seq, hd), lambda b, h: (b, h)),
                   pl.BlockSpec((1, 1, nb, hd), lambda b, h: (b, h, 0, 0)),
                   pl.BlockSpec((1, 1, nb, hd, BLK), lambda b, h: (b, h, 0, 0, 0))],
        out_shape=[jax.ShapeDtypeStruct((T, H * hd), BF16), jax.ShapeDtypeStruct((batch, H, nb, hd), F32),
                   jax.ShapeDtypeStruct((batch, H, nb, hd, BLK), BF16)],
        name="moba_kv",
        compiler_params=_params("parallel", "parallel"),
    )(pm, pm, cos_t, sin_lo, sin_hi)
    tab = pl.BlockSpec((BLK, hd), lambda b, h, i: (i, 0))
    NH = MOBA_HEADS_PER_STEP
    assert H % NH == 0 and q_col % NH == 0
    return pl.pallas_call(
        functools.partial(_moba_attn_kernel, nb=nb, rope_half=half),
        grid=(batch, H // NH, nb),
        in_specs=[pl.BlockSpec((BLK, NH * hd), lambda b, h, i: (b * nb + i, q_col // NH + h)), tab, tab, tab,
                  pl.BlockSpec((seq, NH * hd), lambda b, h, i: (b, h)),
                  pl.BlockSpec((1, NH, nb, hd, BLK), lambda b, h, i: (b, h, 0, 0, 0)),
                  pl.BlockSpec((1, NH, nb, hd), lambda b, h, i: (b, h, 0, 0))],
        out_specs=pl.BlockSpec((BLK, NH * hd), lambda b, h, i: (b * nb + i, h)),
        out_shape=jax.ShapeDtypeStruct((T, H * hd), BF16),
        scratch_shapes=[pltpu.VMEM((NH, nb, BLK), F32), pltpu.VMEM((NH, hd, BLK), F32)],
        name="moba_attn",
        compiler_params=_params("parallel", "parallel", "arbitrary"),
    )(pm, cos_t, sin_lo, sin_hi, kr, vt, kmean)


def _gelu_erf(x):
    return 0.5 * x * (1.0 + lax.erf(x * (2.0 ** -0.5)))


def _top_values(x, count):
    vals = []
    for _ in range(count):
        m = jnp.max(x, axis=0, keepdims=True)
        vals.append(m)
        x = jnp.where(x >= m, -jnp.inf, x)
    return vals


def _peer_topk_kernel(q_ref, sk_ref, thr_ref, e1_ref, s2_ref, e2_ref):
    nt = (((1,), (1,)), ((), ()))
    K = PEER_TOPK
    q = q_ref[...]
    half = q.shape[-1] // 2
    s1 = lax.dot_general(sk_ref[0, 0], q[:, :half], nt, precision=lax.Precision.HIGHEST, preferred_element_type=F32)
    s2 = lax.dot_general(sk_ref[0, 1], q[:, half:], nt, precision=lax.Precision.HIGHEST, preferred_element_type=F32)
    v1 = _top_values(s1, K)
    v2 = _top_values(s2, K)
    v2_all = jnp.concatenate(v2, axis=0)
    cand = jnp.concatenate([v1[0] + v2_all] + [v1[a] + v2_all[:K // 2] for a in range(1, K)], axis=0)
    tau = _top_values(cand, K)[-1]
    top = v1[0] + v2[0]
    z = jnp.sum(jnp.where(cand >= tau, jnp.exp(cand - top), 0.0), axis=0, keepdims=True)
    thr = jnp.full_like(s1, jnp.inf)
    for b in range(K):
        thr = jnp.where(s1 + v2[b] >= tau, v2[b], thr)
    thr_ref[0] = jnp.where(s1 >= v1[K - 1], thr, jnp.inf)
    e1_ref[0] = jnp.exp(s1 - v1[0])
    s2_ref[0] = s2
    e2_ref[0] = jnp.exp(s2 - v2[0]) / z


def peer_topk(q, subkeys, tt=256):
    T = q.shape[0]
    H, _, NK, dh = subkeys.shape
    assert NK == PEER_NKEYS
    spec = pl.BlockSpec((1, NK, tt), lambda i, h: (h, 0, i))
    shape = jax.ShapeDtypeStruct((H, NK, T), F32)
    return pl.pallas_call(
        _peer_topk_kernel,
        grid=(T // tt, H),
        in_specs=[pl.BlockSpec((tt, 2 * dh), lambda i, h: (i, h)),
                  pl.BlockSpec((1, 2, NK, dh), lambda i, h: (h, 0, 0, 0))],
        out_specs=[spec] * 4,
        out_shape=[shape] * 4,
        name="peer_topk",
        compiler_params=_params("parallel", "parallel"),
    )(q, subkeys)


def _peer_gate_kernel(ht_ref, u_ref, scale_ref, thr_ref, e1_ref, s2_ref, e2_ref, w_ref, act_ref, g_ref,
                      *, rows_per_tile):
    NK = PEER_NKEYS
    nchunks = ht_ref.shape[1] // LANES
    for s in range(rows_per_tile // PEER_SLICE_ROWS):
        rows = slice(s * PEER_SLICE_ROWS * NK, (s + 1) * PEER_SLICE_ROWS * NK)
        act_ref[rows, :] = jnp.dot(u_ref[rows, :], ht_ref[...], preferred_element_type=F32)
        for k1 in range(s * PEER_SLICE_ROWS, (s + 1) * PEER_SLICE_ROWS):
            for c in range(nchunks):
                cols = slice(c * LANES, (c + 1) * LANES)
                g = None
                for h in range(PEER_HEADS):
                    thr = thr_ref[h, k1:k1 + 1, cols]
                    e1 = e1_ref[h, k1:k1 + 1, cols]
                    term = jnp.where(s2_ref[h, :, cols] >= thr, e2_ref[h, :, cols], 0.0) * e1
                    g = term if g is None else g + term
                g_ref[k1 * NK:(k1 + 1) * NK, cols] = g
    for k1 in range(rows_per_tile):
        erows = slice(k1 * NK, (k1 + 1) * NK)
        for c in range(nchunks):
            cols = slice(c * LANES, (c + 1) * LANES)
            w = g_ref[erows, cols] * _gelu_erf(act_ref[erows, cols] * scale_ref[:, cols])
            w_ref[cols, erows] = w.T.astype(w_ref.dtype)


def peer_gate(ht, u, layer, scale, tables, tt=PEER_TOKEN_TILE, te=PEER_EXPERT_TILE):
    D, T = ht.shape
    E = u.shape[1]
    H, NK, _ = tables[0].shape
    rows_per_tile = te // NK
    assert rows_per_tile % SUBLANES == 0
    k1spec = pl.BlockSpec((H, rows_per_tile, tt), lambda i, j: (0, j, i))
    k2spec = pl.BlockSpec((H, NK, tt), lambda i, j: (0, 0, i))
    return pl.pallas_call(
        functools.partial(_peer_gate_kernel, rows_per_tile=rows_per_tile),
        grid=(T // tt, E // te),
        in_specs=[pl.BlockSpec((D, tt), lambda i, j: (0, i)),
                  pl.BlockSpec((None, te, D), lambda i, j: (layer, j, 0)),
                  pl.BlockSpec((1, tt), lambda i, j: (0, i)),
                  k1spec, k1spec, k2spec, k2spec],
        out_specs=pl.BlockSpec((tt, te), lambda i, j: (i, j)),
        out_shape=jax.ShapeDtypeStruct((T, E), BF16),
        scratch_shapes=[pltpu.VMEM((te, tt), F32), pltpu.VMEM((te, tt), F32)],
        name="peer_gate",
        compiler_params=_params("parallel", "arbitrary"),
    )(ht, u, scale, *tables)


def quantise_table(w):
    amax = jnp.maximum(jnp.max(jnp.abs(w), axis=(1, 2)), FP8_TINY)
    return (w * (FP8_TARGET_MAX / amax)[:, None, None]).astype(FP8), amax * (1.0 / FP8_TARGET_MAX)


def peer_ffn(x, layer, norm_g, wq, subkeys, u_fp8, u_scale, v_tab):
    h, ht, t_scale = rmsnorm(x, norm_g, BF16, with_fp8_transpose=True)
    q = matmul(h, wq, layer=layer, name="peer_query")
    tables = peer_topk(q, subkeys)
    w = peer_gate(ht, u_fp8, layer, t_scale * u_scale[layer], tables)
    return matmul(w, v_tab, res=x, layer=layer, tk=2048, name="peer_out")


def _in_proj_layout(group_width):
    gw = group_width
    names = ('lru_x', 'lru_gate', 'gla_q', 'gla_k', 'gla_v', 'gla_og', 's5_u', 'moba_q', 'moba_k', 'moba_v')
    sizes = (gw, gw, gw // 2, gw // 2, gw, gw, gw, gw, gw, gw)
    cols, c = {}, 0
    for name, size in zip(names, sizes):
        cols[name] = c // LANES
        c += size
    return sum(sizes[:5]), cols


def kernel(x, norm1_g, w_in, lru_conv_w, lru_conv_b, lru_wa, lru_ba, lru_wx, lru_bx, lru_lambda, gla_wg2, gla_bg, gla_norm_g, s5_a_re, s5_a_im, s5_log_step, s5_b_re, s5_b_im, s5_c_re, s5_c_im, s5_d, s5_w_glu, s5_b_glu, w_out, norm2_g, peer_wq, peer_subkeys, peer_u, peer_v, final_norm_g):
    B, S, D = x.shape
    T = B * S
    depth = norm1_g.shape[0]
    gw = D // N_MIXERS
    x = x.reshape(T, D)
    lr0, col = _in_proj_layout(gw)
    w_in_t = jnp.transpose(w_in, (0, 2, 1))
    w_main = jnp.concatenate([w_in_t[:, :lr0], w_in_t[:, lr0 + GLA_GATE_RANK:]], axis=1).astype(BF16)
    w_lr = jnp.zeros((depth, LANES, D), BF16).at[:, :GLA_GATE_RANK].set(
        w_in_t[:, lr0:lr0 + GLA_GATE_RANK].astype(BF16))
    w_out_bf16 = w_out.astype(BF16)
    peer_wq_bf16 = peer_wq.astype(BF16)
    peer_u_fp8, peer_u_scale = quantise_table(peer_u)
    peer_v_bf16 = peer_v.astype(BF16)
    for l in range(depth):
        h = rmsnorm(x, norm1_g[l], BF16)
        pm = matmul(h, w_main, layer=l, w_transposed=True, name="in_proj")
        plr = matmul(h, w_lr, layer=l, w_transposed=True, name="in_proj_lr")
        y_a = rglru(pm, B, S, col['lru_x'], col['lru_gate'], lru_conv_w[l], lru_conv_b[l], lru_wa[l], lru_ba[l],
                    lru_wx[l], lru_bx[l], lru_lambda[l])
        y_b = gla(pm, pm, plr, B, S, col['gla_q'], col['gla_k'], col['gla_v'], col['gla_og'],
                  gla_wg2[l], gla_bg[l], gla_norm_g[l])
        prep = s5_prepare(s5_a_re[l], s5_a_im[l], s5_log_step[l], s5_b_re[l], s5_b_im[l], s5_c_re[l], s5_c_im[l])
        z = s5_ssm(pm, B, S, col['s5_u'], prep, s5_d[l])
        y_c = glu(z, s5_w_glu[l], s5_b_glu[l])
        y_d = moba(pm, B, S, col['moba_q'], col['moba_k'], col['moba_v'])
        mixed = jnp.concatenate([y_a, y_b, y_c, y_d], axis=-1)
        x = matmul(mixed, w_out_bf16, res=x, layer=l, name="out_proj")
        x = peer_ffn(x, l, norm2_g[l], peer_wq_bf16, peer_subkeys[l], peer_u_fp8, peer_u_scale, peer_v_bf16)
    return rmsnorm(x, final_norm_g, F32).reshape(B, S, D)
```

```python
import functools
import math

import jax
import jax.numpy as jnp
from jax import lax
from jax.experimental import pallas as pl
from jax.experimental.pallas import tpu as pltpu

F32 = jnp.float32
BF16 = jnp.bfloat16
FP8 = jnp.float8_e4m3fn
FP8_TARGET_MAX = 256.0
FP8_TINY = 1e-30

LANES = 128
SUBLANES = 8
VMEM_LIMIT_BYTES = 56 * 2**20

RMS_EPS = 1e-6
N_MIXERS = 4

LRU_BLOCKS = 8
LRU_CONV = 4
LRU_C = 8.0
LRU_ROWS = 256

GLA_HEADS = 4
GLA_GATE_RANK = 16
GLA_TAU = 16.0
GLA_CHUNK = 64
GLA_TILE = 512
GLA_HEADS_PER_STEP = 2

S5_GROUP = 16
S5_STATE = 64
S5_SLAB_GROUPS = LANES // S5_GROUP
S5_SLAB_STATES = S5_SLAB_GROUPS * S5_STATE
S5_TILE = 256

MOBA_HEADS = 8
MOBA_BLOCK = 256
MOBA_TOPK = 3
MOBA_GROUP = 4
MOBA_HEADS_PER_STEP = 2
ROPE_THETA = 500000.0
MASK_VALUE = -1e30

PEER_HEADS = 8
PEER_NKEYS = 128
PEER_TOPK = 16
PEER_TOKEN_TILE = 512
PEER_EXPERT_TILE = 1024
PEER_SLICE_ROWS = 2


def _params(*semantics):
    return pltpu.CompilerParams(dimension_semantics=semantics, vmem_limit_bytes=VMEM_LIMIT_BYTES)


def _rmsnorm_kernel(x_ref, g_ref, o_ref, *maybe_quantised):
    x = x_ref[...]
    y = x * lax.rsqrt(jnp.mean(x * x, axis=-1, keepdims=True) + RMS_EPS) * g_ref[...]
    o_ref[...] = y.astype(o_ref.dtype)
    if maybe_quantised:
        qt_ref, scale_ref, norm_ref = maybe_quantised
        yt = y.T
        amax = jnp.maximum(jnp.max(jnp.abs(yt), axis=0, keepdims=True), FP8_TINY)
        qt_ref[...] = (yt * (FP8_TARGET_MAX / amax)).astype(qt_ref.dtype)
        scale_ref[...] = amax * (1.0 / FP8_TARGET_MAX)
        norm_ref[...] = jnp.sqrt(jnp.sum(yt * yt, axis=0, keepdims=True))


def rmsnorm(x, g, out_dtype, with_fp8_transpose=False, tm=256):
    T, D = x.shape
    out_shape = [jax.ShapeDtypeStruct((T, D), out_dtype)]
    out_specs = [pl.BlockSpec((tm, D), lambda i: (i, 0))]
    with_transpose = with_fp8_transpose
    if with_transpose:
        row = pl.BlockSpec((1, tm), lambda i: (0, i))
        out_shape += [jax.ShapeDtypeStruct((D, T), FP8)] + [jax.ShapeDtypeStruct((1, T), F32)] * 2
        out_specs += [pl.BlockSpec((D, tm), lambda i: (0, i)), row, row]
    res = pl.pallas_call(
        _rmsnorm_kernel,
        grid=(T // tm,),
        in_specs=[pl.BlockSpec((tm, D), lambda i: (i, 0)), pl.BlockSpec((1, D), lambda i: (0, 0))],
        out_specs=out_specs,
        out_shape=out_shape,
        name="rmsnorm",
        compiler_params=_params("parallel"),
    )(x, g.reshape(1, D))
    return res if with_transpose else res[0]


def _mm_kernel(x_ref, w_ref, *rest, nk, has_res, has_scale, w_transposed):
    rest = list(rest)
    s_ref = rest.pop(0) if has_scale else None
    r_ref = rest.pop(0) if has_res else None
    o_ref = rest.pop(0)
    contract = (((1,), (1 if w_transposed else 0,)), ((), ()))
    part = lax.dot_general(x_ref[...], w_ref[...], contract, preferred_element_type=F32)

    def finish(acc):
        if has_scale:
            acc = acc * s_ref[...]
        if has_res:
            acc = acc + r_ref[...]
        o_ref[...] = acc.astype(o_ref.dtype)

    if nk == 1:
        finish(part)
        return
    acc_ref = rest.pop(0)
    k = pl.program_id(2)

    @pl.when(k == 0)
    def _():
        acc_ref[...] = part

    @pl.when(k > 0)
    def _():
        acc_ref[...] += part

    @pl.when(k == nk - 1)
    def _():
        finish(acc_ref[...])


def matmul(x, w, res=None, *, row_scale=None, layer=None, w_transposed=False, tm=1024, tn=1024, tk=4096,
           out_dtype=F32, name="matmul"):
    M, K = x.shape
    k_axis, n_axis = (-1, -2) if w_transposed else (-2, -1)
    N = w.shape[n_axis]
    tm, tn, tk = min(tm, M), min(tn, N), min(tk, K)
    assert M % tm == 0 and N % tn == 0 and K % tk == 0 and w.shape[k_axis] == K
    nk = K // tk
    wblock = (tn, tk) if w_transposed else (tk, tn)
    windex = (lambda i, j, k: (j, k)) if w_transposed else (lambda i, j, k: (k, j))
    if layer is None:
        wspec = pl.BlockSpec(wblock, windex)
    else:
        wspec = pl.BlockSpec((None,) + wblock, lambda i, j, k: (layer,) + windex(i, j, k))
    in_specs = [pl.BlockSpec((tm, tk), lambda i, j, k: (i, k)), wspec]
    args = [x, w]
    if row_scale is not None:
        in_specs.append(pl.BlockSpec((tm, 1), lambda i, j, k: (i, 0)))
        args.append(row_scale)
    if res is not None:
        in_specs.append(pl.BlockSpec((tm, tn), lambda i, j, k: (i, j)))
        args.append(res)
    return pl.pallas_call(
        functools.partial(_mm_kernel, nk=nk, has_res=res is not None, has_scale=row_scale is not None,
                          w_transposed=w_transposed),
        grid=(M // tm, N // tn, nk),
        in_specs=in_specs,
        out_specs=pl.BlockSpec((tm, tn), lambda i, j, k: (i, j)),
        out_shape=jax.ShapeDtypeStruct((M, N), out_dtype),
        scratch_shapes=[pltpu.VMEM((tm, tn), F32)] if nk > 1 else [],
        name=name,
        compiler_params=_params("parallel", "parallel", "arbitrary"),
    )(*args)


def _lru_kernel(x_ref, gate_ref, cw_ref, cb_ref, wa_ref, ba_ref, wx_ref, bx_ref, lam_ref, o_ref, *, seq):
    R = LRU_ROWS
    cw = cw_ref[...]
    cb = cb_ref[...]
    ba = ba_ref[...]
    bx = bx_ref[...]
    neg_c_softplus = -LRU_C * jax.nn.softplus(-lam_ref[...])
    wa = wa_ref[0]
    wx = wx_ref[0]
    row = lax.broadcasted_iota(jnp.int32, (R, LANES), 0)
    row8 = lax.broadcasted_iota(jnp.int32, (SUBLANES, LANES), 0)

    def body(c, h):
        r0 = pl.multiple_of(c * R, R)
        xt = x_ref[pl.ds(r0, R), :]
        p0 = pl.multiple_of(jnp.maximum(r0 - SUBLANES, 0), SUBLANES)
        prev = jnp.where(c > 0, x_ref[pl.ds(p0, SUBLANES), :], 0.0)
        xc = xt * cw[LRU_CONV - 1:LRU_CONV] + cb
        for d in range(1, LRU_CONV):
            rolled = pltpu.roll(xt, d, 0)
            head = jnp.where(row8 < d, pltpu.roll(prev, d, 0), rolled[:SUBLANES])
            shifted = jnp.concatenate([head, rolled[SUBLANES:]], axis=0)
            xc = xc + shifted * cw[LRU_CONV - 1 - d:LRU_CONV - d]
        xb = xc.astype(BF16)
        r = jax.nn.sigmoid(jnp.dot(xb, wa, preferred_element_type=F32) + ba)
        i = jax.nn.sigmoid(jnp.dot(xb, wx, preferred_element_type=F32) + bx)
        log_a = r * neg_c_softplus
        a = jnp.exp(log_a)
        b = jnp.sqrt(1.0 - a * a) * (i * xc)
        d = 1
        while d < R:
            keep = row >= d
            a_sh = jnp.where(keep, pltpu.roll(a, d, 0), 1.0)
            b_sh = jnp.where(keep, pltpu.roll(b, d, 0), 0.0)
            b = a * b_sh + b
            a = a * a_sh
            d *= 2
        hs = b + a * h
        o_ref[pl.ds(r0, R), :] = (hs * jax.nn.gelu(gate_ref[pl.ds(r0, R), :])).astype(o_ref.dtype)
        return hs[R - 1:R, :]

    lax.fori_loop(0, seq // R, body, jnp.zeros((1, LANES), F32))


def rglru(pm, batch, seq, x_col, gate_col, conv_w, conv_b, wa, ba, wx, bx, lam):
    T = batch * seq
    W = LRU_BLOCKS * LANES
    vec = lambda v: v.reshape(1, W)
    vspec = pl.BlockSpec((1, LANES), lambda b, n: (0, n))
    wspec = pl.BlockSpec((1, LANES, LANES), lambda b, n: (n, 0, 0))
    return pl.pallas_call(
        functools.partial(_lru_kernel, seq=seq),
        grid=(batch, LRU_BLOCKS),
        in_specs=[pl.BlockSpec((seq, LANES), lambda b, n: (b, x_col + n)),
                  pl.BlockSpec((seq, LANES), lambda b, n: (b, gate_col + n)),
                  pl.BlockSpec((LRU_CONV, LANES), lambda b, n: (0, n)),
                  vspec, wspec, vspec, wspec, vspec, vspec],
        out_specs=pl.BlockSpec((seq, LANES), lambda b, n: (b, n)),
        out_shape=jax.ShapeDtypeStruct((T, W), BF16),
        name="rglru",
        compiler_params=_params("parallel", "parallel"),
    )(pm, pm, conv_w, vec(conv_b), wa.astype(BF16), vec(ba), wx.astype(BF16), vec(bx), vec(lam))


def _gla_kernel(q_ref, k_ref, v_ref, og_ref, lr_ref, wg_ref, bg_ref, ng_ref, o_ref, st_ref):
    C = GLA_CHUNK
    NH = GLA_HEADS_PER_STEP
    dk = q_ref.shape[-1] // NH
    dv = v_ref.shape[-1] // NH

    @pl.when(pl.program_id(2) == 0)
    def _():
        st_ref[...] = jnp.zeros_like(st_ref)

    ri = lax.broadcasted_iota(jnp.int32, (C, C), 0)
    ci = lax.broadcasted_iota(jnp.int32, (C, C), 1)
    causal = ri >= ci
    tril = causal.astype(F32)
    scale = dk ** -0.5
    nt = (((1,), (1,)), ((), ()))
    tn = (((0,), (0,)), ((), ()))

    for c in range(GLA_TILE // C):
        rows = pl.ds(c * C, C)
        lr = lr_ref[rows, :]
        for hh in range(NH):
            kcols = slice(hh * dk, (hh + 1) * dk)
            vcols = slice(hh * dv, (hh + 1) * dv)
            q = q_ref[rows, kcols]
            k = k_ref[rows, kcols]
            v = v_ref[rows, vcols].astype(BF16)
            pre = jnp.dot(lr, wg_ref[hh], precision=lax.Precision.HIGHEST, preferred_element_type=F32) + bg_ref[hh]
            g = jax.nn.log_sigmoid(pre) / GLA_TAU
            bc = jnp.dot(tril, g, precision=lax.Precision.HIGHEST, preferred_element_type=F32)
            b_last = bc[C - 1:C, :]
            qe = (q * scale * jnp.exp(bc)).astype(BF16)
            ke = (k * jnp.exp(-bc)).astype(BF16)
            kd = (k * jnp.exp(b_last - bc)).astype(BF16)
            att = lax.dot_general(qe, ke, nt, preferred_element_type=F32)
            att = jnp.where(causal, att, 0.0).astype(BF16)
            st = st_ref[hh]
            o = jnp.dot(att, v, preferred_element_type=F32)
            o = o + lax.dot_general(qe, st.astype(BF16), nt, preferred_element_type=F32)
            st_ref[hh] = st * jnp.exp(b_last) + lax.dot_general(v, kd, tn, preferred_element_type=F32)
            o = o * lax.rsqrt(jnp.mean(o * o, axis=-1, keepdims=True) + RMS_EPS)
            o_ref[rows, vcols] = (o * ng_ref[hh] * jax.nn.silu(og_ref[rows, vcols])).astype(o_ref.dtype)


def gla(pm, pm_og, plr, batch, seq, q_col, k_col, v_col, og_col, wg2, bg, norm_g):
    T = batch * seq
    H = GLA_HEADS
    dk = wg2.shape[-1] // H
    dv = norm_g.shape[-1] // H
    assert dk == LANES and dv == 2 * LANES
    nt = seq // GLA_TILE
    wg = jnp.zeros((LANES, H * dk), F32).at[:GLA_GATE_RANK].set(wg2)
    wg = wg.reshape(LANES, H, dk).transpose(1, 0, 2)
    tok = lambda b, h, t: b * nt + t
    NH = GLA_HEADS_PER_STEP
    kb, vb = NH * dk // LANES, NH * dv // LANES
    assert H % NH == 0 and q_col % kb == 0 and k_col % kb == 0 and v_col % vb == 0 and og_col % vb == 0
    return pl.pallas_call(
        _gla_kernel,
        grid=(batch, H // NH, nt),
        in_specs=[pl.BlockSpec((GLA_TILE, NH * dk), lambda b, h, t: (tok(b, h, t), q_col // kb + h)),
                  pl.BlockSpec((GLA_TILE, NH * dk), lambda b, h, t: (tok(b, h, t), k_col // kb + h)),
                  pl.BlockSpec((GLA_TILE, NH * dv), lambda b, h, t: (tok(b, h, t), v_col // vb + h)),
                  pl.BlockSpec((GLA_TILE, NH * dv), lambda b, h, t: (tok(b, h, t), og_col // vb + h)),
                  pl.BlockSpec((GLA_TILE, LANES), lambda b, h, t: (tok(b, h, t), 0)),
                  pl.BlockSpec((NH, LANES, dk), lambda b, h, t: (h, 0, 0)),
                  pl.BlockSpec((NH, 1, dk), lambda b, h, t: (h, 0, 0)),
                  pl.BlockSpec((NH, 1, dv), lambda b, h, t: (h, 0, 0))],
        out_specs=pl.BlockSpec((GLA_TILE, NH * dv), lambda b, h, t: (tok(b, h, t), h)),
        out_shape=jax.ShapeDtypeStruct((T, H * dv), BF16),
        scratch_shapes=[pltpu.VMEM((NH, dv, dk), F32)],
        name="gla",
        compiler_params=_params("parallel", "parallel", "arbitrary"),
    )(pm, pm, pm, pm_og, plr, wg, bg.reshape(H, 1, dk), norm_g.reshape(H, 1, dv))


def _s5_kernel(u_ref, bcat_ref, ccat_ref, pr_ref, pi_ref, d_ref, z_ref, x_ref, carry_ref):
    TT = S5_TILE
    NS = 2 * S5_SLAB_STATES

    @pl.when(pl.program_id(2) == 0)
    def _():
        carry_ref[...] = jnp.zeros_like(carry_ref)

    u = u_ref[...]
    pr = pr_ref[0]
    pi = pi_ref[0]

    def swap(t):
        return pltpu.roll(t, S5_SLAB_STATES, 1)

    x = jnp.dot(u.astype(BF16), bcat_ref[0], preferred_element_type=F32)
    sub = lax.broadcasted_iota(jnp.int32, (TT, NS), 0) % SUBLANES
    for d in (1, 2, 4):
        sh = jnp.where(sub >= d, pltpu.roll(x, d, 0), 0.0)
        x = x + pr[d - 1:d] * sh + pi[d - 1:d] * swap(sh)
    x_ref[...] = x

    def carry_step(i, carry):
        rows = pl.ds(pl.multiple_of(i * SUBLANES, SUBLANES), SUBLANES)
        cb = jnp.broadcast_to(carry, (SUBLANES, NS))
        blk = x_ref[rows, :] + pr * cb + pi * swap(cb)
        x_ref[rows, :] = blk
        return blk[SUBLANES - 1:SUBLANES, :]

    carry_ref[...] = lax.fori_loop(0, TT // SUBLANES, carry_step, carry_ref[...])
    y = jnp.dot(x_ref[...].astype(BF16), ccat_ref[0], preferred_element_type=F32) + d_ref[...] * u
    z_ref[...] = jax.nn.gelu(y)


def s5_prepare(a_re, a_im, log_step, b_re, b_im, c_re, c_im):
    G, P = a_re.shape
    H = S5_GROUP
    ns = G // S5_SLAB_GROUPS
    step = jnp.exp(log_step)[:, None]
    mag = jnp.exp(a_re * step)
    ang = a_im * step
    abar_re, abar_im = mag * jnp.cos(ang), mag * jnp.sin(ang)
    den = a_re * a_re + a_im * a_im
    f_re = ((abar_re - 1.0) * a_re + abar_im * a_im) / den
    f_im = (abar_im * a_re - (abar_re - 1.0) * a_im) / den
    bb_re = f_re[..., None] * b_re - f_im[..., None] * b_im
    bb_im = f_re[..., None] * b_im + f_im[..., None] * b_re
    eye = jnp.eye(S5_SLAB_GROUPS, dtype=F32)

    def in_slab(bb):
        t = bb.reshape(ns, S5_SLAB_GROUPS, P, H)
        return jnp.einsum('sgph,gk->sghkp', t, eye).reshape(ns, LANES, S5_SLAB_STATES)

    def out_slab(cc):
        t = cc.reshape(ns, S5_SLAB_GROUPS, H, P)
        return jnp.einsum('sghp,gk->sgpkh', t, eye).reshape(ns, S5_SLAB_STATES, LANES)

    bcat = jnp.concatenate([in_slab(bb_re), in_slab(bb_im)], axis=-1).astype(BF16)
    ccat = jnp.concatenate([out_slab(c_re), -out_slab(c_im)], axis=1).astype(BF16)
    pw_re, pw_im = [abar_re], [abar_im]
    for _ in range(SUBLANES - 1):
        r, i = pw_re[-1], pw_im[-1]
        pw_re.append(r * abar_re - i * abar_im)
        pw_im.append(r * abar_im + i * abar_re)
    slab = lambda t: jnp.stack(t, 0).reshape(SUBLANES, ns, S5_SLAB_STATES).transpose(1, 0, 2)
    pr, pi = slab(pw_re), slab(pw_im)
    return bcat, ccat, jnp.concatenate([pr, pr], -1), jnp.concatenate([-pi, pi], -1)


def s5_ssm(pm, batch, seq, u_col, prep, d):
    T = batch * seq
    bcat, ccat, pr, pi = prep
    ns = bcat.shape[0]
    NS = 2 * S5_SLAB_STATES
    nt = seq // S5_TILE
    return pl.pallas_call(
        _s5_kernel,
        grid=(batch, ns, nt),
        in_specs=[pl.BlockSpec((S5_TILE, LANES), lambda b, s, t: (b * nt + t, u_col + s)),
                  pl.BlockSpec((1, LANES, NS), lambda b, s, t: (s, 0, 0)),
                  pl.BlockSpec((1, NS, LANES), lambda b, s, t: (s, 0, 0)),
                  pl.BlockSpec((1, SUBLANES, NS), lambda b, s, t: (s, 0, 0)),
                  pl.BlockSpec((1, SUBLANES, NS), lambda b, s, t: (s, 0, 0)),
                  pl.BlockSpec((1, LANES), lambda b, s, t: (0, s))],
        out_specs=pl.BlockSpec((S5_TILE, LANES), lambda b, s, t: (b * nt + t, s)),
        out_shape=jax.ShapeDtypeStruct((T, ns * LANES), F32),
        scratch_shapes=[pltpu.VMEM((S5_TILE, NS), F32), pltpu.VMEM((1, NS), F32)],
        name="s5_ssm",
        compiler_params=_params("parallel", "parallel", "arbitrary"),
    )(pm, bcat, ccat, pr, pi, d.reshape(1, -1))


def _glu_kernel(z_ref, w_ref, b_ref, o_ref):
    z = z_ref[...]
    y = jnp.dot(z.astype(BF16), w_ref[...], preferred_element_type=F32) + b_ref[...]
    o_ref[...] = (z * jax.nn.sigmoid(y)).astype(o_ref.dtype)


def glu(z, w, b, tm=512):
    T, W = z.shape
    return pl.pallas_call(
        _glu_kernel,
        grid=(T // tm,),
        in_specs=[pl.BlockSpec((tm, W), lambda i: (i, 0)),
                  pl.BlockSpec((W, W), lambda i: (0, 0)),
                  pl.BlockSpec((1, W), lambda i: (0, 0))],
        out_specs=pl.BlockSpec((tm, W), lambda i: (i, 0)),
        out_shape=jax.ShapeDtypeStruct((T, W), BF16),
        name="s5_glu",
        compiler_params=_params("parallel"),
    )(z, w.astype(BF16), b.reshape(1, W))


def _rope(t, cos, sin_lo, sin_hi, rope_half):
    return (t * cos + pltpu.roll(t, LANES - rope_half, 1) * sin_lo + pltpu.roll(t, rope_half, 1) * sin_hi)


def rope_tables(seq, head_dim):
    rope_dims = head_dim // 4
    half = rope_dims // 2
    inv = jnp.power(ROPE_THETA, -jnp.arange(half, dtype=F32) / half)
    ang = jnp.arange(seq).astype(F32)[:, None] * inv[None, :]
    cos, sin = jnp.cos(ang), jnp.sin(ang)
    pad = head_dim - rope_dims
    zeros = jnp.zeros((seq, half), F32)
    cos_t = jnp.concatenate([cos, cos, jnp.ones((seq, pad), F32)], axis=-1)
    sin_lo = jnp.concatenate([-sin, zeros, jnp.zeros((seq, pad), F32)], axis=-1)
    sin_hi = jnp.concatenate([zeros, sin, jnp.zeros((seq, pad), F32)], axis=-1)
    return cos_t, sin_lo, sin_hi, half


def _moba_kv_kernel(k_ref, v_ref, cos_ref, slo_ref, shi_ref, kr_ref, km_ref, vt_ref, *, nb, rope_half):
    for n in range(nb):
        rows = pl.ds(n * MOBA_BLOCK, MOBA_BLOCK)
        kr = _rope(k_ref[rows, :], cos_ref[rows, :], slo_ref[rows, :], shi_ref[rows, :], rope_half)
        kr_ref[rows, :] = kr.astype(kr_ref.dtype)
        km_ref[0, 0, n:n + 1, :] = jnp.mean(kr, axis=0, keepdims=True)
        vt_ref[0, 0, n] = v_ref[rows, :].T.astype(vt_ref.dtype)


def _moba_attn_kernel(q_ref, cos_ref, slo_ref, shi_ref, kr_ref, vt_ref, km_ref, o_ref, sel_ref, acc_ref,
                      *, nb, rope_half):
    BLK = MOBA_BLOCK
    NH = MOBA_HEADS_PER_STEP
    i = pl.program_id(2)
    hd = q_ref.shape[-1] // NH
    log2_scale = hd ** -0.5 * math.log2(math.e)
    blk_id = lax.broadcasted_iota(jnp.int32, (nb, BLK), 0)
    key = lax.broadcasted_iota(jnp.int32, (BLK, BLK), 0)
    qry = lax.broadcasted_iota(jnp.int32, (BLK, BLK), 1)
    own = pl.ds(pl.multiple_of(i * BLK, BLK), BLK)
    cos, slo, shi = cos_ref[...], slo_ref[...], shi_ref[...]
    lanes = [slice(hh * hd, (hh + 1) * hd) for hh in range(NH)]

    qbs, stats = [], []
    for hh in range(NH):
        qt = _rope(q_ref[:, lanes[hh]], cos, slo, shi, rope_half).T
        gate = jnp.dot(km_ref[0, hh], qt, precision=lax.Precision.HIGHEST, preferred_element_type=F32)
        rank = jnp.zeros((nb, BLK), jnp.int32)
        for m in range(nb):
            gm = gate[m:m + 1, :]
            beats = (gm > gate) | ((gm == gate) & (m < blk_id))
            rank = rank + jnp.where(beats & (m < i), 1, 0)
        sel_ref[hh] = ((blk_id < i) & (rank < MOBA_TOPK)).astype(F32)
        qb = (qt * log2_scale).astype(BF16)
        s = jnp.dot(kr_ref[own, lanes[hh]], qb, preferred_element_type=F32)
        s = jnp.where(key <= qry, s, MASK_VALUE)
        m0 = jnp.max(s, axis=0, keepdims=True)
        p = jnp.exp2(s - m0)
        acc_ref[hh] = jnp.dot(vt_ref[0, hh, i], p.astype(BF16), preferred_element_type=F32)
        qbs.append(qb)
        stats += [m0, jnp.sum(p, axis=0, keepdims=True)]

    def body(g, carry):
        out = []
        for hh in range(NH):
            m_prev, l_prev = carry[2 * hh], carry[2 * hh + 1]
            scores = []
            for b in range(MOBA_GROUP):
                n = g * MOBA_GROUP + b
                rows = pl.ds(pl.multiple_of(n * BLK, BLK), BLK)
                sn = jnp.dot(kr_ref[rows, lanes[hh]], qbs[hh], preferred_element_type=F32)
                scores.append(jnp.where(sel_ref[hh, pl.ds(n, 1), :] > 0.0, sn, MASK_VALUE))
            m_new = m_prev
            for sn in scores:
                m_new = jnp.maximum(m_new, jnp.max(sn, axis=0, keepdims=True))
            alpha = jnp.exp2(m_prev - m_new)
            l_new = alpha * l_prev
            acc = alpha * acc_ref[hh]
            for b, sn in enumerate(scores):
                pn = jnp.exp2(sn - m_new)
                l_new = l_new + jnp.sum(pn, axis=0, keepdims=True)
                acc = acc + jnp.dot(vt_ref[0, hh, g * MOBA_GROUP + b], pn.astype(BF16), preferred_element_type=F32)
            acc_ref[hh] = acc
            out += [m_new, l_new]
        return tuple(out)

    groups = (i + MOBA_GROUP - 1) // MOBA_GROUP
    stats = lax.fori_loop(0, groups, body, tuple(stats))
    for hh in range(NH):
        o_ref[:, lanes[hh]] = (acc_ref[hh] / stats[2 * hh + 1]).T.astype(o_ref.dtype)


def moba(pm, batch, seq, q_col, k_col, v_col):
    T = batch * seq
    H, hd, BLK = MOBA_HEADS, LANES, MOBA_BLOCK
    nb = seq // BLK
    assert seq % BLK == 0 and nb % MOBA_GROUP == 0
    cos_t, sin_lo, sin_hi, half = rope_tables(seq, hd)
    full = pl.BlockSpec((seq, hd), lambda b, h: (0, 0))
    kr, kmean, vt = pl.pallas_call(
        functools.partial(_moba_kv_kernel, nb=nb, rope_half=half),
        grid=(batch, H),
        in_specs=[pl.BlockSpec((seq, hd), lambda b, h: (b, k_col + h)),
                  pl.BlockSpec((seq, hd), lambda b, h: (b, v_col + h)), full, full, full],
        out_specs=[pl.BlockSpec((seq, hd), lambda b, h: (b, h)),
                   pl.BlockSpec((1, 1, nb, hd), lambda b, h: (b, h, 0, 0)),
                   pl.BlockSpec((1, 1, nb, hd, BLK), lambda b, h: (b, h, 0, 0, 0))],
        out_shape=[jax.ShapeDtypeStruct((T, H * hd), BF16), jax.ShapeDtypeStruct((batch, H, nb, hd), F32),
                   jax.ShapeDtypeStruct((batch, H, nb, hd, BLK), BF16)],
        name="moba_kv",
        compiler_params=_params("parallel", "parallel"),
    )(pm, pm, cos_t, sin_lo, sin_hi)
    tab = pl.BlockSpec((BLK, hd), lambda b, h, i: (i, 0))
    NH = MOBA_HEADS_PER_STEP
    assert H % NH == 0 and q_col % NH == 0
    return pl.pallas_call(
        functools.partial(_moba_attn_kernel, nb=nb, rope_half=half),
        grid=(batch, H // NH, nb),
        in_specs=[pl.BlockSpec((BLK, NH * hd), lambda b, h, i: (b * nb + i, q_col // NH + h)), tab, tab, tab,
                  pl.BlockSpec((seq, NH * hd), lambda b, h, i: (b, h)),
                  pl.BlockSpec((1, NH, nb, hd, BLK), lambda b, h, i: (b, h, 0, 0, 0)),
                  pl.BlockSpec((1, NH, nb, hd), lambda b, h, i: (b, h, 0, 0))],
        out_specs=pl.BlockSpec((BLK, NH * hd), lambda b, h, i: (b * nb + i, h)),
        out_shape=jax.ShapeDtypeStruct((T, H * hd), BF16),
        scratch_shapes=[pltpu.VMEM((NH, nb, BLK), F32), pltpu.VMEM((NH, hd, BLK), F32)],
        name="moba_attn",
        compiler_params=_params("parallel", "parallel", "arbitrary"),
    )(pm, cos_t, sin_lo, sin_hi, kr, vt, kmean)


def _gelu_erf(x):
    return 0.5 * x * (1.0 + lax.erf(x * (2.0 ** -0.5)))


def _top_values(x, count):
    vals = []
    for _ in range(count):
        m = jnp.max(x, axis=0, keepdims=True)
        vals.append(m)
        x = jnp.where(x >= m, -jnp.inf, x)
    return vals


def _peer_topk_kernel(q_ref, sk_ref, thr_ref, e1_ref, s2_ref, e2_ref):
    nt = (((1,), (1,)), ((), ()))
    K = PEER_TOPK
    q = q_ref[...]
    half = q.shape[-1] // 2
    s1 = lax.dot_general(sk_ref[0, 0], q[:, :half], nt, precision=lax.Precision.HIGHEST, preferred_element_type=F32)
    s2 = lax.dot_general(sk_ref[0, 1], q[:, half:], nt, precision=lax.Precision.HIGHEST, preferred_element_type=F32)
    v1 = _top_values(s1, K)
    v2 = _top_values(s2, K)
    v2_all = jnp.concatenate(v2, axis=0)
    cand = jnp.concatenate([v1[0] + v2_all] + [v1[a] + v2_all[:K // 2] for a in range(1, K)], axis=0)
    tau = _top_values(cand, K)[-1]
    top = v1[0] + v2[0]
    z = jnp.sum(jnp.where(cand >= tau, jnp.exp(cand - top), 0.0), axis=0, keepdims=True)
    thr = jnp.full_like(s1, jnp.inf)
    for b in range(K):
        thr = jnp.where(s1 + v2[b] >= tau, v2[b], thr)
    thr_ref[0] = jnp.where(s1 >= v1[K - 1], thr, jnp.inf)
    e1_ref[0] = jnp.exp(s1 - v1[0])
    s2_ref[0] = s2
    e2_ref[0] = jnp.exp(s2 - v2[0]) / z


def peer_topk(q, subkeys, tt=256):
    T = q.shape[0]
    H, _, NK, dh = subkeys.shape
    assert NK == PEER_NKEYS
    spec = pl.BlockSpec((1, NK, tt), lambda i, h: (h, 0, i))
    shape = jax.ShapeDtypeStruct((H, NK, T), F32)
    return pl.pallas_call(
        _peer_topk_kernel,
        grid=(T // tt, H),
        in_specs=[pl.BlockSpec((tt, 2 * dh), lambda i, h: (i, h)),
                  pl.BlockSpec((1, 2, NK, dh), lambda i, h: (h, 0, 0, 0))],
        out_specs=[spec] * 4,
        out_shape=[shape] * 4,
        name="peer_topk",
        compiler_params=_params("parallel", "parallel"),
    )(q, subkeys)


def _peer_gate_kernel(ht_ref, u_ref, scale_ref, wscale_ref, thr_ref, e1_ref, s2_ref, e2_ref, w_ref, act_ref, g_ref,
                      *, rows_per_tile):
    NK = PEER_NKEYS
    nchunks = ht_ref.shape[1] // LANES
    for s in range(rows_per_tile // PEER_SLICE_ROWS):
        rows = slice(s * PEER_SLICE_ROWS * NK, (s + 1) * PEER_SLICE_ROWS * NK)
        act_ref[rows, :] = jnp.dot(u_ref[rows, :], ht_ref[...], preferred_element_type=F32)
        for k1 in range(s * PEER_SLICE_ROWS, (s + 1) * PEER_SLICE_ROWS):
            for c in range(nchunks):
                cols = slice(c * LANES, (c + 1) * LANES)
                g = None
                for h in range(PEER_HEADS):
                    thr = thr_ref[h, k1:k1 + 1, cols]
                    e1 = e1_ref[h, k1:k1 + 1, cols]
                    term = jnp.where(s2_ref[h, :, cols] >= thr, e2_ref[h, :, cols], 0.0) * e1
                    g = term if g is None else g + term
                g_ref[k1 * NK:(k1 + 1) * NK, cols] = g
    for k1 in range(rows_per_tile):
        erows = slice(k1 * NK, (k1 + 1) * NK)
        for c in range(nchunks):
            cols = slice(c * LANES, (c + 1) * LANES)
            w = g_ref[erows, cols] * _gelu_erf(act_ref[erows, cols] * scale_ref[:, cols]) * wscale_ref[:, cols]
            w_ref[cols, erows] = w.T.astype(w_ref.dtype)


def peer_gate(ht, u, layer, scale, wscale, tables, tt=PEER_TOKEN_TILE, te=PEER_EXPERT_TILE):
    D, T = ht.shape
    E = u.shape[1]
    H, NK, _ = tables[0].shape
    rows_per_tile = te // NK
    assert rows_per_tile % SUBLANES == 0
    k1spec = pl.BlockSpec((H, rows_per_tile, tt), lambda i, j: (0, j, i))
    k2spec = pl.BlockSpec((H, NK, tt), lambda i, j: (0, 0, i))
    return pl.pallas_call(
        functools.partial(_peer_gate_kernel, rows_per_tile=rows_per_tile),
        grid=(T // tt, E // te),
        in_specs=[pl.BlockSpec((D, tt), lambda i, j: (0, i)),
                  pl.BlockSpec((None, te, D), lambda i, j: (layer, j, 0)),
                  pl.BlockSpec((1, tt), lambda i, j: (0, i)),
                  pl.BlockSpec((1, tt), lambda i, j: (0, i)),
                  k1spec, k1spec, k2spec, k2spec],
        out_specs=pl.BlockSpec((tt, te), lambda i, j: (i, j)),
        out_shape=jax.ShapeDtypeStruct((T, E), FP8),
        scratch_shapes=[pltpu.VMEM((te, tt), F32), pltpu.VMEM((te, tt), F32)],
        name="peer_gate",
        compiler_params=_params("parallel", "arbitrary"),
    )(ht, u, scale, wscale, *tables)


def quantise_table(w):
    amax = jnp.maximum(jnp.max(jnp.abs(w), axis=(1, 2)), FP8_TINY)
    row_norm = jnp.sqrt(jnp.max(jnp.sum(w * w, axis=2), axis=1))
    return (w * (FP8_TARGET_MAX / amax)[:, None, None]).astype(FP8), amax * (1.0 / FP8_TARGET_MAX), row_norm


def peer_ffn(x, layer, norm_g, wq, subkeys, u_quant, v_quant):
    u_fp8, u_scale, u_norm = u_quant
    v_fp8, v_scale, _ = v_quant
    h, ht, t_scale, h_norm = rmsnorm(x, norm_g, BF16, with_fp8_transpose=True)
    q = matmul(h, wq, layer=layer, name="peer_query")
    tables = peer_topk(q, subkeys)
    bound = jnp.maximum(PEER_HEADS * h_norm * u_norm[layer], FP8_TINY)
    w = peer_gate(ht, u_fp8, layer, t_scale * u_scale[layer], FP8_TARGET_MAX / bound, tables)
    undo = (bound * (v_scale[layer] / FP8_TARGET_MAX)).reshape(-1, 1)
    return matmul(w, v_fp8, res=x, row_scale=undo, layer=layer, name="peer_out")


def _in_proj_layout(group_width):
    gw = group_width
    names = ('lru_x', 'lru_gate', 'gla_q', 'gla_k', 'gla_v', 'gla_og', 's5_u', 'moba_q', 'moba_k', 'moba_v')
    sizes = (gw, gw, gw // 2, gw // 2, gw, gw, gw, gw, gw, gw)
    cols, c = {}, 0
    for name, size in zip(names, sizes):
        cols[name] = c // LANES
        c += size
    return sum(sizes[:5]), cols


def kernel(x, norm1_g, w_in, lru_conv_w, lru_conv_b, lru_wa, lru_ba, lru_wx, lru_bx, lru_lambda, gla_wg2, gla_bg, gla_norm_g, s5_a_re, s5_a_im, s5_log_step, s5_b_re, s5_b_im, s5_c_re, s5_c_im, s5_d, s5_w_glu, s5_b_glu, w_out, norm2_g, peer_wq, peer_subkeys, peer_u, peer_v, final_norm_g):
    B, S, D = x.shape
    T = B * S
    depth = norm1_g.shape[0]
    gw = D // N_MIXERS
    x = x.reshape(T, D)
    lr0, col = _in_proj_layout(gw)
    w_in_t = jnp.transpose(w_in, (0, 2, 1))
    w_main = jnp.concatenate([w_in_t[:, :lr0], w_in_t[:, lr0 + GLA_GATE_RANK:]], axis=1).astype(BF16)
    w_lr = jnp.zeros((depth, LANES, D), BF16).at[:, :GLA_GATE_RANK].set(
        w_in_t[:, lr0:lr0 + GLA_GATE_RANK].astype(BF16))
    w_out_bf16 = w_out.astype(BF16)
    peer_wq_bf16 = peer_wq.astype(BF16)
    peer_u_quant = quantise_table(peer_u)
    peer_v_quant = quantise_table(peer_v)
    for l in range(depth):
        h = rmsnorm(x, norm1_g[l], BF16)
        pm = matmul(h, w_main, layer=l, w_transposed=True, name="in_proj")
        plr = matmul(h, w_lr, layer=l, w_transposed=True, name="in_proj_lr")
        y_a = rglru(pm, B, S, col['lru_x'], col['lru_gate'], lru_conv_w[l], lru_conv_b[l], lru_wa[l], lru_ba[l],
                    lru_wx[l], lru_bx[l], lru_lambda[l])
        y_b = gla(pm, pm, plr, B, S, col['gla_q'], col['gla_k'], col['gla_v'], col['gla_og'],
                  gla_wg2[l], gla_bg[l], gla_norm_g[l])
        prep = s5_prepare(s5_a_re[l], s5_a_im[l], s5_log_step[l], s5_b_re[l], s5_b_im[l], s5_c_re[l], s5_c_im[l])
        z = s5_ssm(pm, B, S, col['s5_u'], prep, s5_d[l])
        y_c = glu(z, s5_w_glu[l], s5_b_glu[l])
        y_d = moba(pm, B, S, col['moba_q'], col['moba_k'], col['moba_v'])
        mixed = jnp.concatenate([y_a, y_b, y_c, y_d], axis=-1)
        x = matmul(mixed, w_out_bf16, res=x, layer=l, name="out_proj")
        x = peer_ffn(x, l, norm2_g[l], peer_wq_bf16, peer_subkeys[l], peer_u_quant, peer_v_quant)
    return rmsnorm(x, final_norm_g, F32).reshape(B, S, D)
```

```python
import functools
import math

import jax
import jax.numpy as jnp
from jax import lax
from jax.experimental import pallas as pl
from jax.experimental.pallas import tpu as pltpu

F32 = jnp.float32
BF16 = jnp.bfloat16
FP8 = jnp.float8_e4m3fn
FP8_TARGET_MAX = 256.0
FP8_TINY = 1e-30

LANES = 128
SUBLANES = 8
VMEM_LIMIT_BYTES = 56 * 2**20

RMS_EPS = 1e-6
N_MIXERS = 4

LRU_BLOCKS = 8
LRU_CONV = 4
LRU_C = 8.0
LRU_ROWS = 256

GLA_HEADS = 4
GLA_GATE_RANK = 16
GLA_TAU = 16.0
GLA_CHUNK = 64
GLA_TILE = 512
GLA_HEADS_PER_STEP = 2

S5_GROUP = 16
S5_STATE = 64
S5_SLAB_GROUPS = LANES // S5_GROUP
S5_SLAB_STATES = S5_SLAB_GROUPS * S5_STATE
S5_TILE = 256

MOBA_HEADS = 8
MOBA_BLOCK = 256
MOBA_TOPK = 3
MOBA_GROUP = 4
MOBA_HEADS_PER_STEP = 2
ROPE_THETA = 500000.0
MASK_VALUE = -1e30

PEER_HEADS = 8
PEER_NKEYS = 128
PEER_TOPK = 16
PEER_TOKEN_TILE = 512
PEER_EXPERT_TILE = 1024
PEER_SLICE_ROWS = 2


def _params(*semantics):
    return pltpu.CompilerParams(dimension_semantics=semantics, vmem_limit_bytes=VMEM_LIMIT_BYTES)


def _rmsnorm_kernel(x_ref, g_ref, o_ref, *maybe_quantised):
    x = x_ref[...]
    y = x * lax.rsqrt(jnp.mean(x * x, axis=-1, keepdims=True) + RMS_EPS) * g_ref[...]
    o_ref[...] = y.astype(o_ref.dtype)
    if maybe_quantised:
        qt_ref, scale_ref, norm_ref = maybe_quantised
        yt = y.T
        amax = jnp.maximum(jnp.max(jnp.abs(yt), axis=0, keepdims=True), FP8_TINY)
        qt_ref[...] = (yt * (FP8_TARGET_MAX / amax)).astype(qt_ref.dtype)
        scale_ref[...] = amax * (1.0 / FP8_TARGET_MAX)
        norm_ref[...] = jnp.sqrt(jnp.sum(yt * yt, axis=0, keepdims=True))


def rmsnorm(x, g, out_dtype, with_fp8_transpose=False, tm=256):
    T, D = x.shape
    out_shape = [jax.ShapeDtypeStruct((T, D), out_dtype)]
    out_specs = [pl.BlockSpec((tm, D), lambda i: (i, 0))]
    with_transpose = with_fp8_transpose
    if with_transpose:
        row = pl.BlockSpec((1, tm), lambda i: (0, i))
        out_shape += [jax.ShapeDtypeStruct((D, T), FP8)] + [jax.ShapeDtypeStruct((1, T), F32)] * 2
        out_specs += [pl.BlockSpec((D, tm), lambda i: (0, i)), row, row]
    res = pl.pallas_call(
        _rmsnorm_kernel,
        grid=(T // tm,),
        in_specs=[pl.BlockSpec((tm, D), lambda i: (i, 0)), pl.BlockSpec((1, D), lambda i: (0, 0))],
        out_specs=out_specs,
        out_shape=out_shape,
        name="rmsnorm",
        compiler_params=_params("parallel"),
    )(x, g.reshape(1, D))
    return res if with_transpose else res[0]


def _mm_kernel(x_ref, w_ref, *rest, nk, has_res, has_scale, w_transposed):
    rest = list(rest)
    s_ref = rest.pop(0) if has_scale else None
    r_ref = rest.pop(0) if has_res else None
    o_ref = rest.pop(0)
    contract = (((1,), (1 if w_transposed else 0,)), ((), ()))
    part = lax.dot_general(x_ref[...], w_ref[...], contract, preferred_element_type=F32)

    def finish(acc):
        if has_scale:
            acc = acc * s_ref[...]
        if has_res:
            acc = acc + r_ref[...]
        o_ref[...] = acc.astype(o_ref.dtype)

    if nk == 1:
        finish(part)
        return
    acc_ref = rest.pop(0)
    k = pl.program_id(2)

    @pl.when(k == 0)
    def _():
        acc_ref[...] = part

    @pl.when(k > 0)
    def _():
        acc_ref[...] += part

    @pl.when(k == nk - 1)
    def _():
        finish(acc_ref[...])


def matmul(x, w, res=None, *, row_scale=None, layer=None, n_out=None, w_transposed=False, tm=1024, tn=1024,
           tk=4096, out_dtype=F32, name="matmul"):
    M, K = x.shape
    k_axis, n_axis = (-1, -2) if w_transposed else (-2, -1)
    N = w.shape[n_axis] if n_out is None else n_out
    tm, tn, tk = min(tm, M), min(tn, N), min(tk, K)
    assert M % tm == 0 and N % tn == 0 and K % tk == 0 and w.shape[k_axis] == K
    nk = K // tk
    wblock = (tn, tk) if w_transposed else (tk, tn)
    windex = (lambda i, j, k: (j, k)) if w_transposed else (lambda i, j, k: (k, j))
    if layer is None:
        wspec = pl.BlockSpec(wblock, windex)
    else:
        wspec = pl.BlockSpec((None,) + wblock, lambda i, j, k: (layer,) + windex(i, j, k))
    in_specs = [pl.BlockSpec((tm, tk), lambda i, j, k: (i, k)), wspec]
    args = [x, w]
    if row_scale is not None:
        in_specs.append(pl.BlockSpec((tm, 1), lambda i, j, k: (i, 0)))
        args.append(row_scale)
    if res is not None:
        in_specs.append(pl.BlockSpec((tm, tn), lambda i, j, k: (i, j)))
        args.append(res)
    return pl.pallas_call(
        functools.partial(_mm_kernel, nk=nk, has_res=res is not None, has_scale=row_scale is not None,
                          w_transposed=w_transposed),
        grid=(M // tm, N // tn, nk),
        in_specs=in_specs,
        out_specs=pl.BlockSpec((tm, tn), lambda i, j, k: (i, j)),
        out_shape=jax.ShapeDtypeStruct((M, N), out_dtype),
        scratch_shapes=[pltpu.VMEM((tm, tn), F32)] if nk > 1 else [],
        name=name,
        compiler_params=_params("parallel", "parallel", "arbitrary"),
    )(*args)


def _lru_kernel(x_ref, gate_ref, cw_ref, cb_ref, wa_ref, ba_ref, wx_ref, bx_ref, lam_ref, o_ref, *, seq):
    R = LRU_ROWS
    cw = cw_ref[...]
    cb = cb_ref[...]
    ba = ba_ref[...]
    bx = bx_ref[...]
    neg_c_softplus = -LRU_C * jax.nn.softplus(-lam_ref[...])
    wa = wa_ref[0]
    wx = wx_ref[0]
    row = lax.broadcasted_iota(jnp.int32, (R, LANES), 0)
    row8 = lax.broadcasted_iota(jnp.int32, (SUBLANES, LANES), 0)

    def body(c, h):
        r0 = pl.multiple_of(c * R, R)
        xt = x_ref[pl.ds(r0, R), :]
        p0 = pl.multiple_of(jnp.maximum(r0 - SUBLANES, 0), SUBLANES)
        prev = jnp.where(c > 0, x_ref[pl.ds(p0, SUBLANES), :], 0.0)
        xc = xt * cw[LRU_CONV - 1:LRU_CONV] + cb
        for d in range(1, LRU_CONV):
            rolled = pltpu.roll(xt, d, 0)
            head = jnp.where(row8 < d, pltpu.roll(prev, d, 0), rolled[:SUBLANES])
            shifted = jnp.concatenate([head, rolled[SUBLANES:]], axis=0)
            xc = xc + shifted * cw[LRU_CONV - 1 - d:LRU_CONV - d]
        xb = xc.astype(BF16)
        r = jax.nn.sigmoid(jnp.dot(xb, wa, preferred_element_type=F32) + ba)
        i = jax.nn.sigmoid(jnp.dot(xb, wx, preferred_element_type=F32) + bx)
        log_a = r * neg_c_softplus
        a = jnp.exp(log_a)
        b = jnp.sqrt(1.0 - a * a) * (i * xc)
        d = 1
        while d < R:
            keep = row >= d
            a_sh = jnp.where(keep, pltpu.roll(a, d, 0), 1.0)
            b_sh = jnp.where(keep, pltpu.roll(b, d, 0), 0.0)
            b = a * b_sh + b
            a = a * a_sh
            d *= 2
        hs = b + a * h
        o_ref[pl.ds(r0, R), :] = (hs * jax.nn.gelu(gate_ref[pl.ds(r0, R), :])).astype(o_ref.dtype)
        return hs[R - 1:R, :]

    lax.fori_loop(0, seq // R, body, jnp.zeros((1, LANES), F32))


def rglru(pm, batch, seq, x_col, gate_col, conv_w, conv_b, wa, ba, wx, bx, lam):
    T = batch * seq
    W = LRU_BLOCKS * LANES
    vec = lambda v: v.reshape(1, W)
    vspec = pl.BlockSpec((1, LANES), lambda b, n: (0, n))
    wspec = pl.BlockSpec((1, LANES, LANES), lambda b, n: (n, 0, 0))
    return pl.pallas_call(
        functools.partial(_lru_kernel, seq=seq),
        grid=(batch, LRU_BLOCKS),
        in_specs=[pl.BlockSpec((seq, LANES), lambda b, n: (b, x_col + n)),
                  pl.BlockSpec((seq, LANES), lambda b, n: (b, gate_col + n)),
                  pl.BlockSpec((LRU_CONV, LANES), lambda b, n: (0, n)),
                  vspec, wspec, vspec, wspec, vspec, vspec],
        out_specs=pl.BlockSpec((seq, LANES), lambda b, n: (b, n)),
        out_shape=jax.ShapeDtypeStruct((T, W), BF16),
        name="rglru",
        compiler_params=_params("parallel", "parallel"),
    )(pm, pm, conv_w, vec(conv_b), wa.astype(BF16), vec(ba), wx.astype(BF16), vec(bx), vec(lam))


def _gla_kernel(q_ref, k_ref, v_ref, og_ref, lr_ref, wg_ref, bg_ref, ng_ref, o_ref, st_ref):
    C = GLA_CHUNK
    NH = GLA_HEADS_PER_STEP
    dk = q_ref.shape[-1] // NH
    dv = v_ref.shape[-1] // NH

    @pl.when(pl.program_id(2) == 0)
    def _():
        st_ref[...] = jnp.zeros_like(st_ref)

    ri = lax.broadcasted_iota(jnp.int32, (C, C), 0)
    ci = lax.broadcasted_iota(jnp.int32, (C, C), 1)
    causal = ri >= ci
    tril = causal.astype(F32)
    scale = dk ** -0.5
    nt = (((1,), (1,)), ((), ()))
    tn = (((0,), (0,)), ((), ()))

    for c in range(GLA_TILE // C):
        rows = pl.ds(c * C, C)
        lr = lr_ref[rows, :]
        for hh in range(NH):
            kcols = slice(hh * dk, (hh + 1) * dk)
            vcols = slice(hh * dv, (hh + 1) * dv)
            q = q_ref[rows, kcols]
            k = k_ref[rows, kcols]
            v = v_ref[rows, vcols].astype(BF16)
            pre = jnp.dot(lr, wg_ref[hh], precision=lax.Precision.HIGHEST, preferred_element_type=F32) + bg_ref[hh]
            g = jax.nn.log_sigmoid(pre) / GLA_TAU
            bc = jnp.dot(tril, g, precision=lax.Precision.HIGHEST, preferred_element_type=F32)
            b_last = bc[C - 1:C, :]
            qe = (q * scale * jnp.exp(bc)).astype(BF16)
            ke = (k * jnp.exp(-bc)).astype(BF16)
            kd = (k * jnp.exp(b_last - bc)).astype(BF16)
            att = lax.dot_general(qe, ke, nt, preferred_element_type=F32)
            att = jnp.where(causal, att, 0.0).astype(BF16)
            st = st_ref[hh]
            o = jnp.dot(att, v, preferred_element_type=F32)
            o = o + lax.dot_general(qe, st.astype(BF16), nt, preferred_element_type=F32)
            st_ref[hh] = st * jnp.exp(b_last) + lax.dot_general(v, kd, tn, preferred_element_type=F32)
            o = o * lax.rsqrt(jnp.mean(o * o, axis=-1, keepdims=True) + RMS_EPS)
            o_ref[rows, vcols] = (o * ng_ref[hh] * jax.nn.silu(og_ref[rows, vcols])).astype(o_ref.dtype)


def gla(pm, pm_og, plr, batch, seq, q_col, k_col, v_col, og_col, wg2, bg, norm_g):
    T = batch * seq
    H = GLA_HEADS
    dk = wg2.shape[-1] // H
    dv = norm_g.shape[-1] // H
    assert dk == LANES and dv == 2 * LANES
    nt = seq // GLA_TILE
    wg = jnp.zeros((LANES, H * dk), F32).at[:GLA_GATE_RANK].set(wg2)
    wg = wg.reshape(LANES, H, dk).transpose(1, 0, 2)
    tok = lambda b, h, t: b * nt + t
    NH = GLA_HEADS_PER_STEP
    kb, vb = NH * dk // LANES, NH * dv // LANES
    assert H % NH == 0 and q_col % kb == 0 and k_col % kb == 0 and v_col % vb == 0 and og_col % vb == 0
    return pl.pallas_call(
        _gla_kernel,
        grid=(batch, H // NH, nt),
        in_specs=[pl.BlockSpec((GLA_TILE, NH * dk), lambda b, h, t: (tok(b, h, t), q_col // kb + h)),
                  pl.BlockSpec((GLA_TILE, NH * dk), lambda b, h, t: (tok(b, h, t), k_col // kb + h)),
                  pl.BlockSpec((GLA_TILE, NH * dv), lambda b, h, t: (tok(b, h, t), v_col // vb + h)),
                  pl.BlockSpec((GLA_TILE, NH * dv), lambda b, h, t: (tok(b, h, t), og_col // vb + h)),
                  pl.BlockSpec((GLA_TILE, LANES), lambda b, h, t: (tok(b, h, t), 0)),
                  pl.BlockSpec((NH, LANES, dk), lambda b, h, t: (h, 0, 0)),
                  pl.BlockSpec((NH, 1, dk), lambda b, h, t: (h, 0, 0)),
                  pl.BlockSpec((NH, 1, dv), lambda b, h, t: (h, 0, 0))],
        out_specs=pl.BlockSpec((GLA_TILE, NH * dv), lambda b, h, t: (tok(b, h, t), h)),
        out_shape=jax.ShapeDtypeStruct((T, H * dv), BF16),
        scratch_shapes=[pltpu.VMEM((NH, dv, dk), F32)],
        name="gla",
        compiler_params=_params("parallel", "parallel", "arbitrary"),
    )(pm, pm, pm, pm_og, plr, wg, bg.reshape(H, 1, dk), norm_g.reshape(H, 1, dv))


def _s5_kernel(u_ref, bcat_ref, ccat_ref, pr_ref, pi_ref, d_ref, z_ref, x_ref, carry_ref):
    TT = S5_TILE
    NS = 2 * S5_SLAB_STATES

    @pl.when(pl.program_id(2) == 0)
    def _():
        carry_ref[...] = jnp.zeros_like(carry_ref)

    u = u_ref[...]
    pr = pr_ref[0]
    pi = pi_ref[0]

    def swap(t):
        return pltpu.roll(t, S5_SLAB_STATES, 1)

    sub = lax.broadcasted_iota(jnp.int32, u.shape, 0) % SUBLANES
    lagged = [u.astype(BF16)]
    for d in range(1, SUBLANES):
        lagged.append(jnp.where(sub >= d, pltpu.roll(u, d, 0), 0.0).astype(BF16))
    x_ref[...] = jnp.dot(jnp.concatenate(lagged, axis=1), bcat_ref[0], preferred_element_type=F32)

    def carry_step(i, carry):
        rows = pl.ds(pl.multiple_of(i * SUBLANES, SUBLANES), SUBLANES)
        cb = jnp.broadcast_to(carry, (SUBLANES, NS))
        blk = x_ref[rows, :] + pr * cb + pi * swap(cb)
        x_ref[rows, :] = blk
        return blk[SUBLANES - 1:SUBLANES, :]

    carry_ref[...] = lax.fori_loop(0, TT // SUBLANES, carry_step, carry_ref[...])
    y = jnp.dot(x_ref[...].astype(BF16), ccat_ref[0], preferred_element_type=F32) + d_ref[...] * u
    z_ref[...] = jax.nn.gelu(y)


def s5_prepare(a_re, a_im, log_step, b_re, b_im, c_re, c_im):
    G, P = a_re.shape
    H = S5_GROUP
    ns = G // S5_SLAB_GROUPS
    step = jnp.exp(log_step)[:, None]
    mag = jnp.exp(a_re * step)
    ang = a_im * step
    abar_re, abar_im = mag * jnp.cos(ang), mag * jnp.sin(ang)
    den = a_re * a_re + a_im * a_im
    f_re = ((abar_re - 1.0) * a_re + abar_im * a_im) / den
    f_im = (abar_im * a_re - (abar_re - 1.0) * a_im) / den
    bb_re = f_re[..., None] * b_re - f_im[..., None] * b_im
    bb_im = f_re[..., None] * b_im + f_im[..., None] * b_re
    eye = jnp.eye(S5_SLAB_GROUPS, dtype=F32)

    def in_slab(bb):
        t = bb.reshape(ns, S5_SLAB_GROUPS, P, H)
        return jnp.einsum('sgph,gk->sghkp', t, eye).reshape(ns, LANES, S5_SLAB_STATES)

    def out_slab(cc):
        t = cc.reshape(ns, S5_SLAB_GROUPS, H, P)
        return jnp.einsum('sghp,gk->sgpkh', t, eye).reshape(ns, S5_SLAB_STATES, LANES)

    ccat = jnp.concatenate([out_slab(c_re), -out_slab(c_im)], axis=1).astype(BF16)
    pw_re, pw_im = [abar_re], [abar_im]
    for _ in range(SUBLANES - 1):
        r, i = pw_re[-1], pw_im[-1]
        pw_re.append(r * abar_re - i * abar_im)
        pw_im.append(r * abar_im + i * abar_re)
    slab = lambda t: jnp.stack(t, 0).reshape(SUBLANES, ns, S5_SLAB_STATES).transpose(1, 0, 2)
    pr, pi = slab(pw_re), slab(pw_im)
    bre, bim = in_slab(bb_re), in_slab(bb_im)
    lag_re = [bre] + [bre * pr[:, j:j + 1] - bim * pi[:, j:j + 1] for j in range(SUBLANES - 1)]
    lag_im = [bim] + [bre * pi[:, j:j + 1] + bim * pr[:, j:j + 1] for j in range(SUBLANES - 1)]
    bcat = jnp.concatenate([jnp.concatenate(lag_re, axis=1), jnp.concatenate(lag_im, axis=1)], axis=-1).astype(BF16)
    return bcat, ccat, jnp.concatenate([pr, pr], -1), jnp.concatenate([-pi, pi], -1)


def s5_ssm(pm, batch, seq, u_col, prep, d):
    T = batch * seq
    bcat, ccat, pr, pi = prep
    ns = bcat.shape[0]
    NS = 2 * S5_SLAB_STATES
    nt = seq // S5_TILE
    return pl.pallas_call(
        _s5_kernel,
        grid=(batch, ns, nt),
        in_specs=[pl.BlockSpec((S5_TILE, LANES), lambda b, s, t: (b * nt + t, u_col + s)),
                  pl.BlockSpec((1, SUBLANES * LANES, NS), lambda b, s, t: (s, 0, 0)),
                  pl.BlockSpec((1, NS, LANES), lambda b, s, t: (s, 0, 0)),
                  pl.BlockSpec((1, SUBLANES, NS), lambda b, s, t: (s, 0, 0)),
                  pl.BlockSpec((1, SUBLANES, NS), lambda b, s, t: (s, 0, 0)),
                  pl.BlockSpec((1, LANES), lambda b, s, t: (0, s))],
        out_specs=pl.BlockSpec((S5_TILE, LANES), lambda b, s, t: (b * nt + t, s)),
        out_shape=jax.ShapeDtypeStruct((T, ns * LANES), F32),
        scratch_shapes=[pltpu.VMEM((S5_TILE, NS), F32), pltpu.VMEM((1, NS), F32)],
        name="s5_ssm",
        compiler_params=_params("parallel", "parallel", "arbitrary"),
    )(pm, bcat, ccat, pr, pi, d.reshape(1, -1))


def _glu_kernel(z_ref, w_ref, b_ref, o_ref):
    z = z_ref[...]
    y = jnp.dot(z.astype(BF16), w_ref[...], preferred_element_type=F32) + b_ref[...]
    o_ref[...] = (z * jax.nn.sigmoid(y)).astype(o_ref.dtype)


def glu(z, w, b, tm=512):
    T, W = z.shape
    return pl.pallas_call(
        _glu_kernel,
        grid=(T // tm,),
        in_specs=[pl.BlockSpec((tm, W), lambda i: (i, 0)),
                  pl.BlockSpec((W, W), lambda i: (0, 0)),
                  pl.BlockSpec((1, W), lambda i: (0, 0))],
        out_specs=pl.BlockSpec((tm, W), lambda i: (i, 0)),
        out_shape=jax.ShapeDtypeStruct((T, W), BF16),
        name="s5_glu",
        compiler_params=_params("parallel"),
    )(z, w.astype(BF16), b.reshape(1, W))


def _rope(t, cos, sin_lo, sin_hi, rope_half):
    return (t * cos + pltpu.roll(t, LANES - rope_half, 1) * sin_lo + pltpu.roll(t, rope_half, 1) * sin_hi)


def rope_tables(seq, head_dim):
    rope_dims = head_dim // 4
    half = rope_dims // 2
    inv = jnp.power(ROPE_THETA, -jnp.arange(half, dtype=F32) / half)
    ang = jnp.arange(seq).astype(F32)[:, None] * inv[None, :]
    cos, sin = jnp.cos(ang), jnp.sin(ang)
    pad = head_dim - rope_dims
    zeros = jnp.zeros((seq, half), F32)
    cos_t = jnp.concatenate([cos, cos, jnp.ones((seq, pad), F32)], axis=-1)
    sin_lo = jnp.concatenate([-sin, zeros, jnp.zeros((seq, pad), F32)], axis=-1)
    sin_hi = jnp.concatenate([zeros, sin, jnp.zeros((seq, pad), F32)], axis=-1)
    return cos_t, sin_lo, sin_hi, half


def _moba_kv_kernel(k_ref, v_ref, cos_ref, slo_ref, shi_ref, kr_ref, km_ref, vt_ref, *, nb, rope_half):
    for n in range(nb):
        rows = pl.ds(n * MOBA_BLOCK, MOBA_BLOCK)
        kr = _rope(k_ref[rows, :], cos_ref[rows, :], slo_ref[rows, :], shi_ref[rows, :], rope_half)
        kr_ref[rows, :] = kr.astype(kr_ref.dtype)
        km_ref[0, 0, n:n + 1, :] = jnp.mean(kr, axis=0, keepdims=True)
        vt_ref[0, 0, n] = v_ref[rows, :].T.astype(vt_ref.dtype)


def _moba_attn_kernel(q_ref, cos_ref, slo_ref, shi_ref, kr_ref, vt_ref, km_ref, o_ref, sel_ref, acc_ref,
                      *, nb, rope_half):
    BLK = MOBA_BLOCK
    NH = MOBA_HEADS_PER_STEP
    i = pl.program_id(2)
    hd = q_ref.shape[-1] // NH
    log2_scale = hd ** -0.5 * math.log2(math.e)
    blk_id = lax.broadcasted_iota(jnp.int32, (nb, BLK), 0)
    key = lax.broadcasted_iota(jnp.int32, (BLK, BLK), 0)
    qry = lax.broadcasted_iota(jnp.int32, (BLK, BLK), 1)
    own = pl.ds(pl.multiple_of(i * BLK, BLK), BLK)
    cos, slo, shi = cos_ref[...], slo_ref[...], shi_ref[...]
    lanes = [slice(hh * hd, (hh + 1) * hd) for hh in range(NH)]

    qbs, stats = [], []
    for hh in range(NH):
        qt = _rope(q_ref[:, lanes[hh]], cos, slo, shi, rope_half).T
        gate = jnp.dot(km_ref[0, hh], qt, precision=lax.Precision.HIGHEST, preferred_element_type=F32)
        rank = jnp.zeros((nb, BLK), jnp.int32)
        for m in range(nb):
            gm = gate[m:m + 1, :]
            beats = (gm > gate) | ((gm == gate) & (m < blk_id))
            rank = rank + jnp.where(beats & (m < i), 1, 0)
        sel_ref[hh] = ((blk_id < i) & (rank < MOBA_TOPK)).astype(F32)
        qb = (qt * log2_scale).astype(BF16)
        s = jnp.dot(kr_ref[own, lanes[hh]], qb, preferred_element_type=F32)
        s = jnp.where(key <= qry, s, MASK_VALUE)
        m0 = jnp.max(s, axis=0, keepdims=True)
        p = jnp.exp2(s - m0)
        acc_ref[hh] = jnp.dot(vt_ref[0, hh, i], p.astype(BF16), preferred_element_type=F32)
        qbs.append(qb)
        stats += [m0, jnp.sum(p, axis=0, keepdims=True)]

    def body(g, carry):
        out = []
        for hh in range(NH):
            m_prev, l_prev = carry[2 * hh], carry[2 * hh + 1]
            scores = []
            for b in range(MOBA_GROUP):
                n = g * MOBA_GROUP + b
                rows = pl.ds(pl.multiple_of(n * BLK, BLK), BLK)
                sn = jnp.dot(kr_ref[rows, lanes[hh]], qbs[hh], preferred_element_type=F32)
                scores.append(jnp.where(sel_ref[hh, pl.ds(n, 1), :] > 0.0, sn, MASK_VALUE))
            m_new = m_prev
            for sn in scores:
                m_new = jnp.maximum(m_new, jnp.max(sn, axis=0, keepdims=True))
            alpha = jnp.exp2(m_prev - m_new)
            l_new = alpha * l_prev
            acc = alpha * acc_ref[hh]
            for b, sn in enumerate(scores):
                pn = jnp.exp2(sn - m_new)
                l_new = l_new + jnp.sum(pn, axis=0, keepdims=True)
                acc = acc + jnp.dot(vt_ref[0, hh, g * MOBA_GROUP + b], pn.astype(BF16), preferred_element_type=F32)
            acc_ref[hh] = acc
            out += [m_new, l_new]
        return tuple(out)

    groups = (i + MOBA_GROUP - 1) // MOBA_GROUP
    stats = lax.fori_loop(0, groups, body, tuple(stats))
    for hh in range(NH):
        o_ref[:, lanes[hh]] = (acc_ref[hh] / stats[2 * hh + 1]).T.astype(o_ref.dtype)


def moba(pm, batch, seq, q_col, k_col, v_col):
    T = batch * seq
    H, hd, BLK = MOBA_HEADS, LANES, MOBA_BLOCK
    nb = seq // BLK
    assert seq % BLK == 0 and nb % MOBA_GROUP == 0
    cos_t, sin_lo, sin_hi, half = rope_tables(seq, hd)
    full = pl.BlockSpec((seq, hd), lambda b, h: (0, 0))
    kr, kmean, vt = pl.pallas_call(
        functools.partial(_moba_kv_kernel, nb=nb, rope_half=half),
        grid=(batch, H),
        in_specs=[pl.BlockSpec((seq, hd), lambda b, h: (b, k_col + h)),
                  pl.BlockSpec((seq, hd), lambda b, h: (b, v_col + h)), full, full, full],
        out_specs=[pl.BlockSpec((seq, hd), lambda b, h: (b, h)),
                   pl.BlockSpec((1, 1, nb, hd), lambda b, h: (b, h, 0, 0)),
                   pl.BlockSpec((1, 1, nb, hd, BLK), lambda b, h: (b, h, 0, 0, 0))],
        out_shape=[jax.ShapeDtypeStruct((T, H * hd), BF16), jax.ShapeDtypeStruct((batch, H, nb, hd), F32),
                   jax.ShapeDtypeStruct((batch, H, nb, hd, BLK), BF16)],
        name="moba_kv",
        compiler_params=_params("parallel", "parallel"),
    )(pm, pm, cos_t, sin_lo, sin_hi)
    tab = pl.BlockSpec((BLK, hd), lambda b, h, i: (i, 0))
    NH = MOBA_HEADS_PER_STEP
    assert H % NH == 0 and q_col % NH == 0
    return pl.pallas_call(
        functools.partial(_moba_attn_kernel, nb=nb, rope_half=half),
        grid=(batch, H // NH, nb),
        in_specs=[pl.BlockSpec((BLK, NH * hd), lambda b, h, i: (b * nb + i, q_col // NH + h)), tab, tab, tab,
                  pl.BlockSpec((seq, NH * hd), lambda b, h, i: (b, h)),
                  pl.BlockSpec((1, NH, nb, hd, BLK), lambda b, h, i: (b, h, 0, 0, 0)),
                  pl.BlockSpec((1, NH, nb, hd), lambda b, h, i: (b, h, 0, 0))],
        out_specs=pl.BlockSpec((BLK, NH * hd), lambda b, h, i: (b * nb + i, h)),
        out_shape=jax.ShapeDtypeStruct((T, H * hd), BF16),
        scratch_shapes=[pltpu.VMEM((NH, nb, BLK), F32), pltpu.VMEM((NH, hd, BLK), F32)],
        name="moba_attn",
        compiler_params=_params("parallel", "parallel", "arbitrary"),
    )(pm, cos_t, sin_lo, sin_hi, kr, vt, kmean)


def _gelu_erf(x):
    return 0.5 * x * (1.0 + lax.erf(x * (2.0 ** -0.5)))


def _top_values(x, count):
    vals = []
    for _ in range(count):
        m = jnp.max(x, axis=0, keepdims=True)
        vals.append(m)
        x = jnp.where(x >= m, -jnp.inf, x)
    return vals


def _peer_topk_kernel(q_ref, sk_ref, thr_ref, e1_ref, s2_ref, e2_ref):
    nt = (((1,), (1,)), ((), ()))
    K = PEER_TOPK
    q = q_ref[...]
    half = q.shape[-1] // 2
    s1 = lax.dot_general(sk_ref[0, 0], q[:, :half], nt, precision=lax.Precision.HIGHEST, preferred_element_type=F32)
    s2 = lax.dot_general(sk_ref[0, 1], q[:, half:], nt, precision=lax.Precision.HIGHEST, preferred_element_type=F32)
    v1 = _top_values(s1, K)
    v2 = _top_values(s2, K)
    v2_all = jnp.concatenate(v2, axis=0)
    cand = jnp.concatenate([v1[0] + v2_all] + [v1[a] + v2_all[:K // 2] for a in range(1, K)], axis=0)
    tau = _top_values(cand, K)[-1]
    top = v1[0] + v2[0]
    z = jnp.sum(jnp.where(cand >= tau, jnp.exp(cand - top), 0.0), axis=0, keepdims=True)
    thr = jnp.full_like(s1, jnp.inf)
    for b in range(K):
        thr = jnp.where(s1 + v2[b] >= tau, v2[b], thr)
    thr_ref[0] = jnp.where(s1 >= v1[K - 1], thr, jnp.inf)
    e1_ref[0] = jnp.exp(s1 - v1[0])
    s2_ref[0] = s2
    e2_ref[0] = jnp.exp(s2 - v2[0]) / z


def peer_topk(q, subkeys, tt=256):
    T = q.shape[0]
    H, _, NK, dh = subkeys.shape
    assert NK == PEER_NKEYS
    spec = pl.BlockSpec((1, NK, tt), lambda i, h: (h, 0, i))
    shape = jax.ShapeDtypeStruct((H, NK, T), F32)
    return pl.pallas_call(
        _peer_topk_kernel,
        grid=(T // tt, H),
        in_specs=[pl.BlockSpec((tt, 2 * dh), lambda i, h: (i, h)),
                  pl.BlockSpec((1, 2, NK, dh), lambda i, h: (h, 0, 0, 0))],
        out_specs=[spec] * 4,
        out_shape=[shape] * 4,
        name="peer_topk",
        compiler_params=_params("parallel", "parallel"),
    )(q, subkeys)


def _peer_gate_kernel(ht_ref, u_ref, scale_ref, wscale_ref, thr_ref, e1_ref, s2_ref, e2_ref, w_ref, act_ref, g_ref,
                      *, rows_per_tile):
    NK = PEER_NKEYS
    nchunks = ht_ref.shape[1] // LANES
    for s in range(rows_per_tile // PEER_SLICE_ROWS):
        rows = slice(s * PEER_SLICE_ROWS * NK, (s + 1) * PEER_SLICE_ROWS * NK)
        act_ref[rows, :] = jnp.dot(u_ref[rows, :], ht_ref[...], preferred_element_type=F32)
        for k1 in range(s * PEER_SLICE_ROWS, (s + 1) * PEER_SLICE_ROWS):
            for c in range(nchunks):
                cols = slice(c * LANES, (c + 1) * LANES)
                g = None
                for h in range(PEER_HEADS):
                    thr = thr_ref[h, k1:k1 + 1, cols]
                    e1 = e1_ref[h, k1:k1 + 1, cols]
                    term = jnp.where(s2_ref[h, :, cols] >= thr, e2_ref[h, :, cols], 0.0) * e1
                    g = term if g is None else g + term
                g_ref[k1 * NK:(k1 + 1) * NK, cols] = g
    for k1 in range(rows_per_tile):
        erows = slice(k1 * NK, (k1 + 1) * NK)
        for c in range(nchunks):
            cols = slice(c * LANES, (c + 1) * LANES)
            w = g_ref[erows, cols] * _gelu_erf(act_ref[erows, cols] * scale_ref[:, cols]) * wscale_ref[:, cols]
            w_ref[cols, erows] = w.T.astype(w_ref.dtype)


def peer_gate(ht, u, layer, scale, wscale, tables, tt=PEER_TOKEN_TILE, te=PEER_EXPERT_TILE):
    D, T = ht.shape
    E = u.shape[1]
    H, NK, _ = tables[0].shape
    rows_per_tile = te // NK
    assert rows_per_tile % SUBLANES == 0
    k1spec = pl.BlockSpec((H, rows_per_tile, tt), lambda i, j: (0, j, i))
    k2spec = pl.BlockSpec((H, NK, tt), lambda i, j: (0, 0, i))
    return pl.pallas_call(
        functools.partial(_peer_gate_kernel, rows_per_tile=rows_per_tile),
        grid=(T // tt, E // te),
        in_specs=[pl.BlockSpec((D, tt), lambda i, j: (0, i)),
                  pl.BlockSpec((None, te, D), lambda i, j: (layer, j, 0)),
                  pl.BlockSpec((1, tt), lambda i, j: (0, i)),
                  pl.BlockSpec((1, tt), lambda i, j: (0, i)),
                  k1spec, k1spec, k2spec, k2spec],
        out_specs=pl.BlockSpec((tt, te), lambda i, j: (i, j)),
        out_shape=jax.ShapeDtypeStruct((T, E), FP8),
        scratch_shapes=[pltpu.VMEM((te, tt), F32), pltpu.VMEM((te, tt), F32)],
        name="peer_gate",
        compiler_params=_params("parallel", "arbitrary"),
    )(ht, u, scale, wscale, *tables)


def quantise_table(w):
    amax = jnp.maximum(jnp.max(jnp.abs(w), axis=(1, 2)), FP8_TINY)
    row_norm = jnp.sqrt(jnp.max(jnp.sum(w * w, axis=2), axis=1))
    return (w * (FP8_TARGET_MAX / amax)[:, None, None]).astype(FP8), amax * (1.0 / FP8_TARGET_MAX), row_norm


def peer_ffn(x, layer, norm_g, wq, subkeys, u_quant, v_quant):
    u_fp8, u_scale, u_norm = u_quant
    v_fp8, v_scale, _ = v_quant
    h, ht, t_scale, h_norm = rmsnorm(x, norm_g, BF16, with_fp8_transpose=True)
    q = matmul(h, wq, layer=layer, name="peer_query")
    tables = peer_topk(q, subkeys)
    bound = jnp.maximum(PEER_HEADS * h_norm * u_norm[layer], FP8_TINY)
    w = peer_gate(ht, u_fp8, layer, t_scale * u_scale[layer], FP8_TARGET_MAX / bound, tables)
    undo = (bound * (v_scale[layer] / FP8_TARGET_MAX)).reshape(-1, 1)
    return matmul(w, v_fp8, res=x, row_scale=undo, layer=layer, name="peer_out")


def _in_proj_layout(group_width):
    gw = group_width
    parts = ((('lru_x', 'lru_gate', 'gla_q', 'gla_k', 'gla_v'), (gw, gw, gw // 2, gw // 2, gw)),
             (('gla_og', 's5_u', 'moba_q', 'moba_k', 'moba_v'), (gw,) * 5))
    cols = {}
    for names, sizes in parts:
        c = 0
        for name, size in zip(names, sizes):
            cols[name] = c // LANES
            c += size
    return sum(parts[0][1]), cols


def kernel(x, norm1_g, w_in, lru_conv_w, lru_conv_b, lru_wa, lru_ba, lru_wx, lru_bx, lru_lambda, gla_wg2, gla_bg, gla_norm_g, s5_a_re, s5_a_im, s5_log_step, s5_b_re, s5_b_im, s5_c_re, s5_c_im, s5_d, s5_w_glu, s5_b_glu, w_out, norm2_g, peer_wq, peer_subkeys, peer_u, peer_v, final_norm_g):
    B, S, D = x.shape
    T = B * S
    depth = norm1_g.shape[0]
    gw = D // N_MIXERS
    x = x.reshape(T, D)
    lr0, col = _in_proj_layout(gw)
    w_in_t = jnp.transpose(w_in, (0, 2, 1)).astype(BF16)
    w_b = w_in_t[:, lr0 + GLA_GATE_RANK:]
    w_lr = jnp.zeros((depth, LANES, D), BF16).at[:, :GLA_GATE_RANK].set(w_in_t[:, lr0:lr0 + GLA_GATE_RANK])
    w_out_bf16 = w_out.astype(BF16)
    peer_wq_bf16 = peer_wq.astype(BF16)
    peer_u_quant = quantise_table(peer_u)
    peer_v_quant = quantise_table(peer_v)
    for l in range(depth):
        h = rmsnorm(x, norm1_g[l], BF16)
        pa = matmul(h, w_in_t, layer=l, n_out=lr0, w_transposed=True, name="in_proj_a")
        pb = matmul(h, w_b, layer=l, w_transposed=True, name="in_proj_b")
        plr = matmul(h, w_lr, layer=l, w_transposed=True, name="in_proj_lr")
        y_a = rglru(pa, B, S, col['lru_x'], col['lru_gate'], lru_conv_w[l], lru_conv_b[l], lru_wa[l], lru_ba[l],
                    lru_wx[l], lru_bx[l], lru_lambda[l])
        y_b = gla(pa, pb, plr, B, S, col['gla_q'], col['gla_k'], col['gla_v'], col['gla_og'],
                  gla_wg2[l], gla_bg[l], gla_norm_g[l])
        prep = s5_prepare(s5_a_re[l], s5_a_im[l], s5_log_step[l], s5_b_re[l], s5_b_im[l], s5_c_re[l], s5_c_im[l])
        z = s5_ssm(pb, B, S, col['s5_u'], prep, s5_d[l])
        y_c = glu(z, s5_w_glu[l], s5_b_glu[l])
        y_d = moba(pb, B, S, col['moba_q'], col['moba_k'], col['moba_v'])
        mixed = jnp.concatenate([y_a, y_b, y_c, y_d], axis=-1)
        x = matmul(mixed, w_out_bf16, res=x, layer=l, name="out_proj")
        x = peer_ffn(x, l, norm2_g[l], peer_wq_bf16, peer_subkeys[l], peer_u_quant, peer_v_quant)
    return rmsnorm(x, final_norm_g, F32).reshape(B, S, D)
```

```python
import functools
import math

import jax
import jax.numpy as jnp
from jax import lax
from jax.experimental import pallas as pl
from jax.experimental.pallas import tpu as pltpu

F32 = jnp.float32
BF16 = jnp.bfloat16
FP8 = jnp.float8_e4m3fn
FP8_TARGET_MAX = 256.0
FP8_TINY = 1e-30

LANES = 128
SUBLANES = 8
VMEM_LIMIT_BYTES = 56 * 2**20

RMS_EPS = 1e-6
N_MIXERS = 4

LRU_BLOCKS = 8
LRU_CONV = 4
LRU_C = 8.0
LRU_ROWS = 256

GLA_HEADS = 4
GLA_GATE_RANK = 16
GLA_TAU = 16.0
GLA_CHUNK = 64
GLA_TILE = 512
GLA_HEADS_PER_STEP = 2

S5_GROUP = 16
S5_STATE = 64
S5_SLAB_GROUPS = LANES // S5_GROUP
S5_SLAB_STATES = S5_SLAB_GROUPS * S5_STATE
S5_TILE = 256

MOBA_HEADS = 8
MOBA_BLOCK = 256
MOBA_TOPK = 3
MOBA_GROUP = 4
MOBA_HEADS_PER_STEP = 2
ROPE_THETA = 500000.0
MASK_VALUE = -1e30

PEER_HEADS = 8
PEER_NKEYS = 128
PEER_TOPK = 16
PEER_TOKEN_TILE = 512
PEER_EXPERT_TILE = 1024
PEER_SLICE_ROWS = 2


def _params(*semantics):
    return pltpu.CompilerParams(dimension_semantics=semantics, vmem_limit_bytes=VMEM_LIMIT_BYTES)


def _rmsnorm_kernel(x_ref, g_ref, o_ref, *maybe_quantised):
    x = x_ref[...]
    y = x * lax.rsqrt(jnp.mean(x * x, axis=-1, keepdims=True) + RMS_EPS) * g_ref[...]
    o_ref[...] = y.astype(o_ref.dtype)
    if maybe_quantised:
        qt_ref, scale_ref, norm_ref = maybe_quantised
        yt = y.T
        amax = jnp.maximum(jnp.max(jnp.abs(yt), axis=0, keepdims=True), FP8_TINY)
        qt_ref[...] = (yt * (FP8_TARGET_MAX / amax)).astype(qt_ref.dtype)
        scale_ref[...] = amax * (1.0 / FP8_TARGET_MAX)
        norm_ref[...] = jnp.sqrt(jnp.sum(yt * yt, axis=0, keepdims=True))


def rmsnorm(x, g, out_dtype, with_fp8_transpose=False, tm=256):
    T, D = x.shape
    out_shape = [jax.ShapeDtypeStruct((T, D), out_dtype)]
    out_specs = [pl.BlockSpec((tm, D), lambda i: (i, 0))]
    with_transpose = with_fp8_transpose
    if with_transpose:
        row = pl.BlockSpec((1, tm), lambda i: (0, i))
        out_shape += [jax.ShapeDtypeStruct((D, T), FP8)] + [jax.ShapeDtypeStruct((1, T), F32)] * 2
        out_specs += [pl.BlockSpec((D, tm), lambda i: (0, i)), row, row]
    res = pl.pallas_call(
        _rmsnorm_kernel,
        grid=(T // tm,),
        in_specs=[pl.BlockSpec((tm, D), lambda i: (i, 0)), pl.BlockSpec((1, D), lambda i: (0, 0))],
        out_specs=out_specs,
        out_shape=out_shape,
        name="rmsnorm",
        compiler_params=_params("parallel"),
    )(x, g.reshape(1, D))
    return res if with_transpose else res[0]


def _mm_kernel(x_ref, w_ref, *rest, nk, has_res, has_scale, w_transposed):
    rest = list(rest)
    s_ref = rest.pop(0) if has_scale else None
    r_ref = rest.pop(0) if has_res else None
    o_ref = rest.pop(0)
    contract = (((1,), (1 if w_transposed else 0,)), ((), ()))
    part = lax.dot_general(x_ref[...], w_ref[...], contract, preferred_element_type=F32)

    def finish(acc):
        if has_scale:
            acc = acc * s_ref[...]
        if has_res:
            acc = acc + r_ref[...]
        o_ref[...] = acc.astype(o_ref.dtype)

    if nk == 1:
        finish(part)
        return
    acc_ref = rest.pop(0)
    k = pl.program_id(2)

    @pl.when(k == 0)
    def _():
        acc_ref[...] = part

    @pl.when(k > 0)
    def _():
        acc_ref[...] += part

    @pl.when(k == nk - 1)
    def _():
        finish(acc_ref[...])


def matmul(x, w, res=None, *, row_scale=None, layer=None, n_out=None, w_transposed=False, tm=1024, tn=1024,
           tk=4096, out_dtype=F32, name="matmul"):
    M, K = x.shape
    k_axis, n_axis = (-1, -2) if w_transposed else (-2, -1)
    N = w.shape[n_axis] if n_out is None else n_out
    tm, tn, tk = min(tm, M), min(tn, N), min(tk, K)
    assert M % tm == 0 and N % tn == 0 and K % tk == 0 and w.shape[k_axis] == K
    nk = K // tk
    wblock = (tn, tk) if w_transposed else (tk, tn)
    windex = (lambda i, j, k: (j, k)) if w_transposed else (lambda i, j, k: (k, j))
    if layer is None:
        wspec = pl.BlockSpec(wblock, windex)
    else:
        wspec = pl.BlockSpec((None,) + wblock, lambda i, j, k: (layer,) + windex(i, j, k))
    in_specs = [pl.BlockSpec((tm, tk), lambda i, j, k: (i, k)), wspec]
    args = [x, w]
    if row_scale is not None:
        in_specs.append(pl.BlockSpec((tm, 1), lambda i, j, k: (i, 0)))
        args.append(row_scale)
    if res is not None:
        in_specs.append(pl.BlockSpec((tm, tn), lambda i, j, k: (i, j)))
        args.append(res)
    return pl.pallas_call(
        functools.partial(_mm_kernel, nk=nk, has_res=res is not None, has_scale=row_scale is not None,
                          w_transposed=w_transposed),
        grid=(M // tm, N // tn, nk),
        in_specs=in_specs,
        out_specs=pl.BlockSpec((tm, tn), lambda i, j, k: (i, j)),
        out_shape=jax.ShapeDtypeStruct((M, N), out_dtype),
        scratch_shapes=[pltpu.VMEM((tm, tn), F32)] if nk > 1 else [],
        name=name,
        compiler_params=_params("parallel", "parallel", "arbitrary"),
    )(*args)


def _lru_kernel(x_ref, gate_ref, cw_ref, cb_ref, wa_ref, ba_ref, wx_ref, bx_ref, lam_ref, o_ref, *, seq):
    R = LRU_ROWS
    cw = cw_ref[...]
    cb = cb_ref[...]
    ba = ba_ref[...]
    bx = bx_ref[...]
    neg_c_softplus = -LRU_C * jax.nn.softplus(-lam_ref[...])
    wa = wa_ref[0]
    wx = wx_ref[0]
    row = lax.broadcasted_iota(jnp.int32, (R, LANES), 0)
    row8 = lax.broadcasted_iota(jnp.int32, (SUBLANES, LANES), 0)

    def body(c, h):
        r0 = pl.multiple_of(c * R, R)
        xt = x_ref[pl.ds(r0, R), :]
        p0 = pl.multiple_of(jnp.maximum(r0 - SUBLANES, 0), SUBLANES)
        prev = jnp.where(c > 0, x_ref[pl.ds(p0, SUBLANES), :], 0.0)
        xc = xt * cw[LRU_CONV - 1:LRU_CONV] + cb
        for d in range(1, LRU_CONV):
            rolled = pltpu.roll(xt, d, 0)
            head = jnp.where(row8 < d, pltpu.roll(prev, d, 0), rolled[:SUBLANES])
            shifted = jnp.concatenate([head, rolled[SUBLANES:]], axis=0)
            xc = xc + shifted * cw[LRU_CONV - 1 - d:LRU_CONV - d]
        xb = xc.astype(BF16)
        r = jax.nn.sigmoid(jnp.dot(xb, wa, preferred_element_type=F32) + ba)
        i = jax.nn.sigmoid(jnp.dot(xb, wx, preferred_element_type=F32) + bx)
        log_a = r * neg_c_softplus
        a = jnp.exp(log_a)
        b = jnp.sqrt(1.0 - a * a) * (i * xc)
        d = 1
        while d < R:
            keep = row >= d
            a_sh = jnp.where(keep, pltpu.roll(a, d, 0), 1.0)
            b_sh = jnp.where(keep, pltpu.roll(b, d, 0), 0.0)
            b = a * b_sh + b
            a = a * a_sh
            d *= 2
        hs = b + a * h
        o_ref[pl.ds(r0, R), :] = (hs * jax.nn.gelu(gate_ref[pl.ds(r0, R), :])).astype(o_ref.dtype)
        return hs[R - 1:R, :]

    lax.fori_loop(0, seq // R, body, jnp.zeros((1, LANES), F32))


def rglru(pm, batch, seq, x_col, gate_col, conv_w, conv_b, wa, ba, wx, bx, lam):
    T = batch * seq
    W = LRU_BLOCKS * LANES
    vec = lambda v: v.reshape(1, W)
    vspec = pl.BlockSpec((1, LANES), lambda b, n: (0, n))
    wspec = pl.BlockSpec((1, LANES, LANES), lambda b, n: (n, 0, 0))
    return pl.pallas_call(
        functools.partial(_lru_kernel, seq=seq),
        grid=(batch, LRU_BLOCKS),
        in_specs=[pl.BlockSpec((seq, LANES), lambda b, n: (b, x_col + n)),
                  pl.BlockSpec((seq, LANES), lambda b, n: (b, gate_col + n)),
                  pl.BlockSpec((LRU_CONV, LANES), lambda b, n: (0, n)),
                  vspec, wspec, vspec, wspec, vspec, vspec],
        out_specs=pl.BlockSpec((seq, LANES), lambda b, n: (b, n)),
        out_shape=jax.ShapeDtypeStruct((T, W), BF16),
        name="rglru",
        compiler_params=_params("parallel", "parallel"),
    )(pm, pm, conv_w, vec(conv_b), wa.astype(BF16), vec(ba), wx.astype(BF16), vec(bx), vec(lam))


def _gla_kernel(q_ref, k_ref, v_ref, og_ref, lr_ref, wg_ref, bg_ref, ng_ref, o_ref, st_ref):
    C = GLA_CHUNK
    NH = GLA_HEADS_PER_STEP
    dk = q_ref.shape[-1] // NH
    dv = v_ref.shape[-1] // NH

    @pl.when(pl.program_id(2) == 0)
    def _():
        st_ref[...] = jnp.zeros_like(st_ref)

    TT = GLA_TILE
    NC = TT // C
    ri = lax.broadcasted_iota(jnp.int32, (TT, TT), 0)
    ci = lax.broadcasted_iota(jnp.int32, (TT, TT), 1)
    causal = ((ri // C) == (ci // C)) & (ri >= ci)
    tril = jnp.where(causal, 1.0, 0.0).astype(BF16)
    scale = dk ** -0.5
    nt = (((1,), (1,)), ((), ()))
    tn = (((0,), (0,)), ((), ()))
    lr = lr_ref[...]

    for hh in range(NH):
        kcols = slice(hh * dk, (hh + 1) * dk)
        vcols = slice(hh * dv, (hh + 1) * dv)
        q = q_ref[:, kcols]
        k = k_ref[:, kcols]
        v = v_ref[:, vcols].astype(BF16)
        pre = jnp.dot(lr, wg_ref[hh], precision=lax.Precision.HIGHEST, preferred_element_type=F32) + bg_ref[hh]
        g = jax.nn.log_sigmoid(pre) / GLA_TAU
        g_hi = g.astype(BF16)
        g_lo = (g - g_hi.astype(F32)).astype(BF16)
        bc = (jnp.dot(tril, g_hi, preferred_element_type=F32) + jnp.dot(tril, g_lo, preferred_element_type=F32))
        b_last = jnp.broadcast_to(bc.reshape(NC, C, dk)[:, C - 1:C, :], (NC, C, dk)).reshape(TT, dk)
        qe = (q * scale * jnp.exp(bc)).astype(BF16)
        ke = (k * jnp.exp(-bc)).astype(BF16)
        kd = (k * jnp.exp(b_last - bc)).astype(BF16)
        att = lax.dot_general(qe, ke, nt, preferred_element_type=F32)
        att = jnp.where(causal, att, 0.0).astype(BF16)
        o = jnp.dot(att, v, preferred_element_type=F32)
        st = st_ref[hh]
        carried = []
        for c in range(NC):
            rows = slice(c * C, (c + 1) * C)
            carried.append(lax.dot_general(qe[rows], st.astype(BF16), nt, preferred_element_type=F32))
            decay = jnp.exp(b_last[c * C:c * C + 1, :])
            st = st * decay + lax.dot_general(v[rows], kd[rows], tn, preferred_element_type=F32)
        st_ref[hh] = st
        o = o + jnp.concatenate(carried, axis=0)
        o = o * lax.rsqrt(jnp.mean(o * o, axis=-1, keepdims=True) + RMS_EPS)
        o_ref[:, vcols] = (o * ng_ref[hh] * jax.nn.silu(og_ref[:, vcols])).astype(o_ref.dtype)


def gla(pm, pm_og, plr, batch, seq, q_col, k_col, v_col, og_col, wg2, bg, norm_g):
    T = batch * seq
    H = GLA_HEADS
    dk = wg2.shape[-1] // H
    dv = norm_g.shape[-1] // H
    assert dk == LANES and dv == 2 * LANES
    nt = seq // GLA_TILE
    wg = jnp.zeros((LANES, H * dk), F32).at[:GLA_GATE_RANK].set(wg2)
    wg = wg.reshape(LANES, H, dk).transpose(1, 0, 2)
    tok = lambda b, h, t: b * nt + t
    NH = GLA_HEADS_PER_STEP
    kb, vb = NH * dk // LANES, NH * dv // LANES
    assert H % NH == 0 and q_col % kb == 0 and k_col % kb == 0 and v_col % vb == 0 and og_col % vb == 0
    return pl.pallas_call(
        _gla_kernel,
        grid=(batch, H // NH, nt),
        in_specs=[pl.BlockSpec((GLA_TILE, NH * dk), lambda b, h, t: (tok(b, h, t), q_col // kb + h)),
                  pl.BlockSpec((GLA_TILE, NH * dk), lambda b, h, t: (tok(b, h, t), k_col // kb + h)),
                  pl.BlockSpec((GLA_TILE, NH * dv), lambda b, h, t: (tok(b, h, t), v_col // vb + h)),
                  pl.BlockSpec((GLA_TILE, NH * dv), lambda b, h, t: (tok(b, h, t), og_col // vb + h)),
                  pl.BlockSpec((GLA_TILE, LANES), lambda b, h, t: (tok(b, h, t), 0)),
                  pl.BlockSpec((NH, LANES, dk), lambda b, h, t: (h, 0, 0)),
                  pl.BlockSpec((NH, 1, dk), lambda b, h, t: (h, 0, 0)),
                  pl.BlockSpec((NH, 1, dv), lambda b, h, t: (h, 0, 0))],
        out_specs=pl.BlockSpec((GLA_TILE, NH * dv), lambda b, h, t: (tok(b, h, t), h)),
        out_shape=jax.ShapeDtypeStruct((T, H * dv), BF16),
        scratch_shapes=[pltpu.VMEM((NH, dv, dk), F32)],
        name="gla",
        compiler_params=_params("parallel", "parallel", "arbitrary"),
    )(pm, pm, pm, pm_og, plr, wg, bg.reshape(H, 1, dk), norm_g.reshape(H, 1, dv))


def _s5_kernel(u_ref, bcat_ref, ccat_ref, pr_ref, pi_ref, d_ref, z_ref, x_ref, carry_ref):
    TT = S5_TILE
    NS = 2 * S5_SLAB_STATES

    @pl.when(pl.program_id(2) == 0)
    def _():
        carry_ref[...] = jnp.zeros_like(carry_ref)

    u = u_ref[...]
    pr = pr_ref[0]
    pi = pi_ref[0]

    def swap(t):
        return pltpu.roll(t, S5_SLAB_STATES, 1)

    sub = lax.broadcasted_iota(jnp.int32, u.shape, 0) % SUBLANES
    lagged = [u.astype(BF16)]
    for d in range(1, SUBLANES):
        lagged.append(jnp.where(sub >= d, pltpu.roll(u, d, 0), 0.0).astype(BF16))
    x_ref[...] = jnp.dot(jnp.concatenate(lagged, axis=1), bcat_ref[0], preferred_element_type=F32)

    def carry_step(i, carry):
        rows = pl.ds(pl.multiple_of(i * SUBLANES, SUBLANES), SUBLANES)
        cb = jnp.broadcast_to(carry, (SUBLANES, NS))
        blk = x_ref[rows, :] + pr * cb + pi * swap(cb)
        x_ref[rows, :] = blk
        return blk[SUBLANES - 1:SUBLANES, :]

    carry_ref[...] = lax.fori_loop(0, TT // SUBLANES, carry_step, carry_ref[...])
    y = jnp.dot(x_ref[...].astype(BF16), ccat_ref[0], preferred_element_type=F32) + d_ref[...] * u
    z_ref[...] = jax.nn.gelu(y)


def s5_prepare(a_re, a_im, log_step, b_re, b_im, c_re, c_im):
    G, P = a_re.shape
    H = S5_GROUP
    ns = G // S5_SLAB_GROUPS
    step = jnp.exp(log_step)[:, None]
    mag = jnp.exp(a_re * step)
    ang = a_im * step
    abar_re, abar_im = mag * jnp.cos(ang), mag * jnp.sin(ang)
    den = a_re * a_re + a_im * a_im
    f_re = ((abar_re - 1.0) * a_re + abar_im * a_im) / den
    f_im = (abar_im * a_re - (abar_re - 1.0) * a_im) / den
    bb_re = f_re[..., None] * b_re - f_im[..., None] * b_im
    bb_im = f_re[..., None] * b_im + f_im[..., None] * b_re
    eye = jnp.eye(S5_SLAB_GROUPS, dtype=F32)

    def in_slab(bb):
        t = bb.reshape(ns, S5_SLAB_GROUPS, P, H)
        return jnp.einsum('sgph,gk->sghkp', t, eye).reshape(ns, LANES, S5_SLAB_STATES)

    def out_slab(cc):
        t = cc.reshape(ns, S5_SLAB_GROUPS, H, P)
        return jnp.einsum('sghp,gk->sgpkh', t, eye).reshape(ns, S5_SLAB_STATES, LANES)

    ccat = jnp.concatenate([out_slab(c_re), -out_slab(c_im)], axis=1).astype(BF16)
    pw_re, pw_im = [abar_re], [abar_im]
    for _ in range(SUBLANES - 1):
        r, i = pw_re[-1], pw_im[-1]
        pw_re.append(r * abar_re - i * abar_im)
        pw_im.append(r * abar_im + i * abar_re)
    slab = lambda t: jnp.stack(t, 0).reshape(SUBLANES, ns, S5_SLAB_STATES).transpose(1, 0, 2)
    pr, pi = slab(pw_re), slab(pw_im)
    bre, bim = in_slab(bb_re), in_slab(bb_im)
    lag_re = [bre] + [bre * pr[:, j:j + 1] - bim * pi[:, j:j + 1] for j in range(SUBLANES - 1)]
    lag_im = [bim] + [bre * pi[:, j:j + 1] + bim * pr[:, j:j + 1] for j in range(SUBLANES - 1)]
    bcat = jnp.concatenate([jnp.concatenate(lag_re, axis=1), jnp.concatenate(lag_im, axis=1)], axis=-1).astype(BF16)
    return bcat, ccat, jnp.concatenate([pr, pr], -1), jnp.concatenate([-pi, pi], -1)


def s5_ssm(pm, batch, seq, u_col, prep, d):
    T = batch * seq
    bcat, ccat, pr, pi = prep
    ns = bcat.shape[0]
    NS = 2 * S5_SLAB_STATES
    nt = seq // S5_TILE
    return pl.pallas_call(
        _s5_kernel,
        grid=(batch, ns, nt),
        in_specs=[pl.BlockSpec((S5_TILE, LANES), lambda b, s, t: (b * nt + t, u_col + s)),
                  pl.BlockSpec((1, SUBLANES * LANES, NS), lambda b, s, t: (s, 0, 0)),
                  pl.BlockSpec((1, NS, LANES), lambda b, s, t: (s, 0, 0)),
                  pl.BlockSpec((1, SUBLANES, NS), lambda b, s, t: (s, 0, 0)),
                  pl.BlockSpec((1, SUBLANES, NS), lambda b, s, t: (s, 0, 0)),
                  pl.BlockSpec((1, LANES), lambda b, s, t: (0, s))],
        out_specs=pl.BlockSpec((S5_TILE, LANES), lambda b, s, t: (b * nt + t, s)),
        out_shape=jax.ShapeDtypeStruct((T, ns * LANES), F32),
        scratch_shapes=[pltpu.VMEM((S5_TILE, NS), F32), pltpu.VMEM((1, NS), F32)],
        name="s5_ssm",
        compiler_params=_params("parallel", "parallel", "arbitrary"),
    )(pm, bcat, ccat, pr, pi, d.reshape(1, -1))


def _glu_kernel(z_ref, w_ref, b_ref, o_ref):
    z = z_ref[...]
    y = jnp.dot(z.astype(BF16), w_ref[...], preferred_element_type=F32) + b_ref[...]
    o_ref[...] = (z * jax.nn.sigmoid(y)).astype(o_ref.dtype)


def glu(z, w, b, tm=512):
    T, W = z.shape
    return pl.pallas_call(
        _glu_kernel,
        grid=(T // tm,),
        in_specs=[pl.BlockSpec((tm, W), lambda i: (i, 0)),
                  pl.BlockSpec((W, W), lambda i: (0, 0)),
                  pl.BlockSpec((1, W), lambda i: (0, 0))],
        out_specs=pl.BlockSpec((tm, W), lambda i: (i, 0)),
        out_shape=jax.ShapeDtypeStruct((T, W), BF16),
        name="s5_glu",
        compiler_params=_params("parallel"),
    )(z, w.astype(BF16), b.reshape(1, W))


def _rope(t, cos, sin_lo, sin_hi, rope_half):
    return (t * cos + pltpu.roll(t, LANES - rope_half, 1) * sin_lo + pltpu.roll(t, rope_half, 1) * sin_hi)


def rope_tables(seq, head_dim):
    rope_dims = head_dim // 4
    half = rope_dims // 2
    inv = jnp.power(ROPE_THETA, -jnp.arange(half, dtype=F32) / half)
    ang = jnp.arange(seq).astype(F32)[:, None] * inv[None, :]
    cos, sin = jnp.cos(ang), jnp.sin(ang)
    pad = head_dim - rope_dims
    zeros = jnp.zeros((seq, half), F32)
    cos_t = jnp.concatenate([cos, cos, jnp.ones((seq, pad), F32)], axis=-1)
    sin_lo = jnp.concatenate([-sin, zeros, jnp.zeros((seq, pad), F32)], axis=-1)
    sin_hi = jnp.concatenate([zeros, sin, jnp.zeros((seq, pad), F32)], axis=-1)
    return cos_t, sin_lo, sin_hi, half


def _moba_kv_kernel(k_ref, v_ref, cos_ref, slo_ref, shi_ref, kr_ref, km_ref, vt_ref, *, nb, rope_half):
    for n in range(nb):
        rows = pl.ds(n * MOBA_BLOCK, MOBA_BLOCK)
        kr = _rope(k_ref[rows, :], cos_ref[rows, :], slo_ref[rows, :], shi_ref[rows, :], rope_half)
        kr_ref[rows, :] = kr.astype(kr_ref.dtype)
        km_ref[0, 0, n:n + 1, :] = jnp.mean(kr, axis=0, keepdims=True)
        vt_ref[0, 0, n] = v_ref[rows, :].T.astype(vt_ref.dtype)


def _moba_attn_kernel(q_ref, cos_ref, slo_ref, shi_ref, kr_ref, vt_ref, km_ref, o_ref, sel_ref, acc_ref,
                      *, nb, rope_half):
    BLK = MOBA_BLOCK
    NH = MOBA_HEADS_PER_STEP
    i = pl.program_id(2)
    hd = q_ref.shape[-1] // NH
    log2_scale = hd ** -0.5 * math.log2(math.e)
    blk_id = lax.broadcasted_iota(jnp.int32, (nb, BLK), 0)
    key = lax.broadcasted_iota(jnp.int32, (BLK, BLK), 0)
    qry = lax.broadcasted_iota(jnp.int32, (BLK, BLK), 1)
    own = pl.ds(pl.multiple_of(i * BLK, BLK), BLK)
    cos, slo, shi = cos_ref[...], slo_ref[...], shi_ref[...]
    lanes = [slice(hh * hd, (hh + 1) * hd) for hh in range(NH)]

    qbs, stats = [], []
    for hh in range(NH):
        qt = _rope(q_ref[:, lanes[hh]], cos, slo, shi, rope_half).T
        gate = jnp.dot(km_ref[0, hh], qt, precision=lax.Precision.HIGHEST, preferred_element_type=F32)
        rank = jnp.zeros((nb, BLK), jnp.int32)
        for m in range(nb):
            gm = gate[m:m + 1, :]
            beats = (gm > gate) | ((gm == gate) & (m < blk_id))
            rank = rank + jnp.where(beats & (m < i), 1, 0)
        sel_ref[hh] = ((blk_id < i) & (rank < MOBA_TOPK)).astype(F32)
        qb = (qt * log2_scale).astype(BF16)
        s = jnp.dot(kr_ref[own, lanes[hh]], qb, preferred_element_type=F32)
        s = jnp.where(key <= qry, s, MASK_VALUE)
        m0 = jnp.max(s, axis=0, keepdims=True)
        p = jnp.exp2(s - m0)
        acc_ref[hh] = jnp.dot(vt_ref[0, hh, i], p.astype(BF16), preferred_element_type=F32)
        qbs.append(qb)
        stats += [m0, jnp.sum(p, axis=0, keepdims=True)]

    def body(g, carry):
        out = []
        for hh in range(NH):
            m_prev, l_prev = carry[2 * hh], carry[2 * hh + 1]
            scores = []
            for b in range(MOBA_GROUP):
                n = g * MOBA_GROUP + b
                rows = pl.ds(pl.multiple_of(n * BLK, BLK), BLK)
                sn = jnp.dot(kr_ref[rows, lanes[hh]], qbs[hh], preferred_element_type=F32)
                scores.append(jnp.where(sel_ref[hh, pl.ds(n, 1), :] > 0.0, sn, MASK_VALUE))
            m_new = m_prev
            for sn in scores:
                m_new = jnp.maximum(m_new, jnp.max(sn, axis=0, keepdims=True))
            alpha = jnp.exp2(m_prev - m_new)
            l_new = alpha * l_prev
            acc = alpha * acc_ref[hh]
            for b, sn in enumerate(scores):
                pn = jnp.exp2(sn - m_new)
                l_new = l_new + jnp.sum(pn, axis=0, keepdims=True)
                acc = acc + jnp.dot(vt_ref[0, hh, g * MOBA_GROUP + b], pn.astype(BF16), preferred_element_type=F32)
            acc_ref[hh] = acc
            out += [m_new, l_new]
        return tuple(out)

    groups = (i + MOBA_GROUP - 1) // MOBA_GROUP
    stats = lax.fori_loop(0, groups, body, tuple(stats))
    for hh in range(NH):
        o_ref[:, lanes[hh]] = (acc_ref[hh] / stats[2 * hh + 1]).T.astype(o_ref.dtype)


def moba(pm, batch, seq, q_col, k_col, v_col):
    T = batch * seq
    H, hd, BLK = MOBA_HEADS, LANES, MOBA_BLOCK
    nb = seq // BLK
    assert seq % BLK == 0 and nb % MOBA_GROUP == 0
    cos_t, sin_lo, sin_hi, half = rope_tables(seq, hd)
    full = pl.BlockSpec((seq, hd), lambda b, h: (0, 0))
    kr, kmean, vt = pl.pallas_call(
        functools.partial(_moba_kv_kernel, nb=nb, rope_half=half),
        grid=(batch, H),
        in_specs=[pl.BlockSpec((seq, hd), lambda b, h: (b, k_col + h)),
                  pl.BlockSpec((seq, hd), lambda b, h: (b, v_col + h)), full, full, full],
        out_specs=[pl.BlockSpec((seq, hd), lambda b, h: (b, h)),
                   pl.BlockSpec((1, 1, nb, hd), lambda b, h: (b, h, 0, 0)),
                   pl.BlockSpec((1, 1, nb, hd, BLK), lambda b, h: (b, h, 0, 0, 0))],
        out_shape=[jax.ShapeDtypeStruct((T, H * hd), BF16), jax.ShapeDtypeStruct((batch, H, nb, hd), F32),
                   jax.ShapeDtypeStruct((batch, H, nb, hd, BLK), BF16)],
        name="moba_kv",
        compiler_params=_params("parallel", "parallel"),
    )(pm, pm, cos_t, sin_lo, sin_hi)
    tab = pl.BlockSpec((BLK, hd), lambda b, h, i: (i, 0))
    NH = MOBA_HEADS_PER_STEP
    assert H % NH == 0 and q_col % NH == 0
    return pl.pallas_call(
        functools.partial(_moba_attn_kernel, nb=nb, rope_half=half),
        grid=(batch, H // NH, nb),
        in_specs=[pl.BlockSpec((BLK, NH * hd), lambda b, h, i: (b * nb + i, q_col // NH + h)), tab, tab, tab,
                  pl.BlockSpec((seq, NH * hd), lambda b, h, i: (b, h)),
                  pl.BlockSpec((1, NH, nb, hd, BLK), lambda b, h, i: (b, h, 0, 0, 0)),
                  pl.BlockSpec((1, NH, nb, hd), lambda b, h, i: (b, h, 0, 0))],
        out_specs=pl.BlockSpec((BLK, NH * hd), lambda b, h, i: (b * nb + i, h)),
        out_shape=jax.ShapeDtypeStruct((T, H * hd), BF16),
        scratch_shapes=[pltpu.VMEM((NH, nb, BLK), F32), pltpu.VMEM((NH, hd, BLK), F32)],
        name="moba_attn",
        compiler_params=_params("parallel", "parallel", "arbitrary"),
    )(pm, cos_t, sin_lo, sin_hi, kr, vt, kmean)


def _gelu_erf(x):
    return 0.5 * x * (1.0 + lax.erf(x * (2.0 ** -0.5)))


def _top_values(xs, count):
    vals = [[] for _ in xs]
    xs = list(xs)
    for _ in range(count):
        for j, x in enumerate(xs):
            m = jnp.max(x, axis=0, keepdims=True)
            vals[j].append(m)
            xs[j] = jnp.where(x >= m, -jnp.inf, x)
    return vals


def _peer_topk_kernel(q_ref, sk_ref, thr_ref, e1_ref, s2_ref, e2_ref):
    nt = (((1,), (1,)), ((), ()))
    K = PEER_TOPK
    q = q_ref[...]
    half = q.shape[-1] // 2
    s1 = lax.dot_general(sk_ref[0, 0], q[:, :half], nt, precision=lax.Precision.HIGHEST, preferred_element_type=F32)
    s2 = lax.dot_general(sk_ref[0, 1], q[:, half:], nt, precision=lax.Precision.HIGHEST, preferred_element_type=F32)
    v1, v2 = _top_values([s1, s2], K)
    v2_all = jnp.concatenate(v2, axis=0)
    cand = jnp.concatenate([v1[0] + v2_all] + [v1[a] + v2_all[:K // 2] for a in range(1, K)], axis=0)
    tau = _top_values([cand], K)[0][-1]
    top = v1[0] + v2[0]
    z = jnp.sum(jnp.where(cand >= tau, jnp.exp(cand - top), 0.0), axis=0, keepdims=True)
    thr = jnp.full_like(s1, jnp.inf)
    for b in range(K):
        thr = jnp.where(s1 + v2[b] >= tau, v2[b], thr)
    thr_ref[0] = jnp.where(s1 >= v1[K - 1], thr, jnp.inf)
    e1_ref[0] = jnp.exp(s1 - v1[0])
    s2_ref[0] = s2
    e2_ref[0] = jnp.exp(s2 - v2[0]) / z


def peer_topk(q, subkeys, tt=256):
    T = q.shape[0]
    H, _, NK, dh = subkeys.shape
    assert NK == PEER_NKEYS
    spec = pl.BlockSpec((1, NK, tt), lambda i, h: (h, 0, i))
    shape = jax.ShapeDtypeStruct((H, NK, T), F32)
    return pl.pallas_call(
        _peer_topk_kernel,
        grid=(T // tt, H),
        in_specs=[pl.BlockSpec((tt, 2 * dh), lambda i, h: (i, h)),
                  pl.BlockSpec((1, 2, NK, dh), lambda i, h: (h, 0, 0, 0))],
        out_specs=[spec] * 4,
        out_shape=[shape] * 4,
        name="peer_topk",
        compiler_params=_params("parallel", "parallel"),
    )(q, subkeys)


def _peer_gate_kernel(ht_ref, u_ref, scale_ref, wscale_ref, thr_ref, e1_ref, s2_ref, e2_ref, w_ref, act_ref, g_ref,
                      *, rows_per_tile):
    NK = PEER_NKEYS
    nchunks = ht_ref.shape[1] // LANES
    for s in range(rows_per_tile // PEER_SLICE_ROWS):
        rows = slice(s * PEER_SLICE_ROWS * NK, (s + 1) * PEER_SLICE_ROWS * NK)
        act_ref[rows, :] = jnp.dot(u_ref[rows, :], ht_ref[...], preferred_element_type=F32)
        for k1 in range(s * PEER_SLICE_ROWS, (s + 1) * PEER_SLICE_ROWS):
            for c in range(nchunks):
                cols = slice(c * LANES, (c + 1) * LANES)
                g = None
                for h in range(PEER_HEADS):
                    thr = thr_ref[h, k1:k1 + 1, cols]
                    e1 = e1_ref[h, k1:k1 + 1, cols]
                    term = jnp.where(s2_ref[h, :, cols] >= thr, e2_ref[h, :, cols], 0.0) * e1
                    g = term if g is None else g + term
                g_ref[k1 * NK:(k1 + 1) * NK, cols] = g
    for k1 in range(rows_per_tile):
        erows = slice(k1 * NK, (k1 + 1) * NK)
        for c in range(nchunks):
            cols = slice(c * LANES, (c + 1) * LANES)
            w = g_ref[erows, cols] * _gelu_erf(act_ref[erows, cols] * scale_ref[:, cols]) * wscale_ref[:, cols]
            w_ref[cols, erows] = w.T.astype(w_ref.dtype)


def peer_gate(ht, u, layer, scale, wscale, tables, tt=PEER_TOKEN_TILE, te=PEER_EXPERT_TILE):
    D, T = ht.shape
    E = u.shape[1]
    H, NK, _ = tables[0].shape
    rows_per_tile = te // NK
    assert rows_per_tile % SUBLANES == 0
    k1spec = pl.BlockSpec((H, rows_per_tile, tt), lambda i, j: (0, j, i))
    k2spec = pl.BlockSpec((H, NK, tt), lambda i, j: (0, 0, i))
    return pl.pallas_call(
        functools.partial(_peer_gate_kernel, rows_per_tile=rows_per_tile),
        grid=(T // tt, E // te),
        in_specs=[pl.BlockSpec((D, tt), lambda i, j: (0, i)),
                  pl.BlockSpec((None, te, D), lambda i, j: (layer, j, 0)),
                  pl.BlockSpec((1, tt), lambda i, j: (0, i)),
                  pl.BlockSpec((1, tt), lambda i, j: (0, i)),
                  k1spec, k1spec, k2spec, k2spec],
        out_specs=pl.BlockSpec((tt, te), lambda i, j: (i, j)),
        out_shape=jax.ShapeDtypeStruct((T, E), FP8),
        scratch_shapes=[pltpu.VMEM((te, tt), F32), pltpu.VMEM((te, tt), F32)],
        name="peer_gate",
        compiler_params=_params("parallel", "arbitrary"),
    )(ht, u, scale, wscale, *tables)


def quantise_table(w):
    amax = jnp.maximum(jnp.max(jnp.abs(w), axis=(1, 2)), FP8_TINY)
    row_norm = jnp.sqrt(jnp.max(jnp.sum(w * w, axis=2), axis=1))
    return (w * (FP8_TARGET_MAX / amax)[:, None, None]).astype(FP8), amax * (1.0 / FP8_TARGET_MAX), row_norm


def peer_ffn(x, layer, norm_g, wq, subkeys, u_quant, v_quant):
    u_fp8, u_scale, u_norm = u_quant
    v_fp8, v_scale, _ = v_quant
    h, ht, t_scale, h_norm = rmsnorm(x, norm_g, BF16, with_fp8_transpose=True)
    q = matmul(h, wq, layer=layer, name="peer_query")
    tables = peer_topk(q, subkeys)
    bound = jnp.maximum(PEER_HEADS * h_norm * u_norm[layer], FP8_TINY)
    w = peer_gate(ht, u_fp8, layer, t_scale * u_scale[layer], FP8_TARGET_MAX / bound, tables)
    undo = (bound * (v_scale[layer] / FP8_TARGET_MAX)).reshape(-1, 1)
    return matmul(w, v_fp8, res=x, row_scale=undo, layer=layer, name="peer_out")


def _in_proj_layout(group_width):
    gw = group_width
    parts = ((('lru_x', 'lru_gate', 'gla_q', 'gla_k', 'gla_v'), (gw, gw, gw // 2, gw // 2, gw)),
             (('gla_og', 's5_u', 'moba_q', 'moba_k', 'moba_v'), (gw,) * 5))
    cols = {}
    for names, sizes in parts:
        c = 0
        for name, size in zip(names, sizes):
            cols[name] = c // LANES
            c += size
    return sum(parts[0][1]), cols


def kernel(x, norm1_g, w_in, lru_conv_w, lru_conv_b, lru_wa, lru_ba, lru_wx, lru_bx, lru_lambda, gla_wg2, gla_bg, gla_norm_g, s5_a_re, s5_a_im, s5_log_step, s5_b_re, s5_b_im, s5_c_re, s5_c_im, s5_d, s5_w_glu, s5_b_glu, w_out, norm2_g, peer_wq, peer_subkeys, peer_u, peer_v, final_norm_g):
    B, S, D = x.shape
    T = B * S
    depth = norm1_g.shape[0]
    gw = D // N_MIXERS
    x = x.reshape(T, D)
    lr0, col = _in_proj_layout(gw)
    w_in_t = jnp.transpose(w_in, (0, 2, 1)).astype(BF16)
    w_b = w_in_t[:, lr0 + GLA_GATE_RANK:]
    w_lr = jnp.zeros((depth, LANES, D), BF16).at[:, :GLA_GATE_RANK].set(w_in_t[:, lr0:lr0 + GLA_GATE_RANK])
    w_out_bf16 = w_out.astype(BF16)
    peer_wq_bf16 = peer_wq.astype(BF16)
    peer_u_quant = quantise_table(peer_u)
    peer_v_quant = quantise_table(peer_v)
    for l in range(depth):
        h = rmsnorm(x, norm1_g[l], BF16)
        pa = matmul(h, w_in_t, layer=l, n_out=lr0, w_transposed=True, name="in_proj_a")
        pb = matmul(h, w_b, layer=l, w_transposed=True, name="in_proj_b")
        plr = matmul(h, w_lr, layer=l, w_transposed=True, name="in_proj_lr")
        y_a = rglru(pa, B, S, col['lru_x'], col['lru_gate'], lru_conv_w[l], lru_conv_b[l], lru_wa[l], lru_ba[l],
                    lru_wx[l], lru_bx[l], lru_lambda[l])
        y_b = gla(pa, pb, plr, B, S, col['gla_q'], col['gla_k'], col['gla_v'], col['gla_og'],
                  gla_wg2[l], gla_bg[l], gla_norm_g[l])
        prep = s5_prepare(s5_a_re[l], s5_a_im[l], s5_log_step[l], s5_b_re[l], s5_b_im[l], s5_c_re[l], s5_c_im[l])
        z = s5_ssm(pb, B, S, col['s5_u'], prep, s5_d[l])
        y_c = glu(z, s5_w_glu[l], s5_b_glu[l])
        y_d = moba(pb, B, S, col['moba_q'], col['moba_k'], col['moba_v'])
        mixed = jnp.concatenate([y_a, y_b, y_c, y_d], axis=-1)
        x = matmul(mixed, w_out_bf16, res=x, layer=l, name="out_proj")
        x = peer_ffn(x, l, norm2_g[l], peer_wq_bf16, peer_subkeys[l], peer_u_quant, peer_v_quant)
    return rmsnorm(x, final_norm_g, F32).reshape(B, S, D)
```

```python
import functools
import math

import jax
import jax.numpy as jnp
from jax import lax
from jax.experimental import pallas as pl
from jax.experimental.pallas import tpu as pltpu

F32 = jnp.float32
BF16 = jnp.bfloat16
FP8 = jnp.float8_e4m3fn
FP8_TARGET_MAX = 256.0
FP8_TINY = 1e-30

LANES = 128
SUBLANES = 8
VMEM_LIMIT_BYTES = 56 * 2**20

RMS_EPS = 1e-6
N_MIXERS = 4

LRU_BLOCKS = 8
LRU_CONV = 4
LRU_C = 8.0
LRU_ROWS = 256

GLA_HEADS = 4
GLA_GATE_RANK = 16
GLA_TAU = 16.0
GLA_CHUNK = 64
GLA_TILE = 512
GLA_HEADS_PER_STEP = 2

S5_GROUP = 16
S5_STATE = 64
S5_SLAB_GROUPS = LANES // S5_GROUP
S5_SLAB_STATES = S5_SLAB_GROUPS * S5_STATE
S5_TILE = 256

MOBA_HEADS = 8
MOBA_BLOCK = 256
MOBA_TOPK = 3
MOBA_GROUP = 4
MOBA_HEADS_PER_STEP = 4
ROPE_THETA = 500000.0
MASK_VALUE = -1e30

PEER_HEADS = 8
PEER_NKEYS = 128
PEER_TOPK = 16
PEER_TOKEN_TILE = 512
PEER_EXPERT_TILE = 1024
PEER_SLICE_ROWS = 2


def _params(*semantics):
    return pltpu.CompilerParams(dimension_semantics=semantics, vmem_limit_bytes=VMEM_LIMIT_BYTES)


def _rmsnorm_kernel(x_ref, g_ref, o_ref, *maybe_quantised):
    x = x_ref[...]
    y = x * lax.rsqrt(jnp.mean(x * x, axis=-1, keepdims=True) + RMS_EPS) * g_ref[...]
    o_ref[...] = y.astype(o_ref.dtype)
    if maybe_quantised:
        qt_ref, scale_ref, norm_ref = maybe_quantised
        yt = y.T
        amax = jnp.maximum(jnp.max(jnp.abs(yt), axis=0, keepdims=True), FP8_TINY)
        qt_ref[...] = (yt * (FP8_TARGET_MAX / amax)).astype(qt_ref.dtype)
        scale_ref[...] = amax * (1.0 / FP8_TARGET_MAX)
        norm_ref[...] = jnp.sqrt(jnp.sum(yt * yt, axis=0, keepdims=True))


def rmsnorm(x, g, out_dtype, with_fp8_transpose=False, tm=256):
    T, D = x.shape
    out_shape = [jax.ShapeDtypeStruct((T, D), out_dtype)]
    out_specs = [pl.BlockSpec((tm, D), lambda i: (i, 0))]
    with_transpose = with_fp8_transpose
    if with_transpose:
        row = pl.BlockSpec((1, tm), lambda i: (0, i))
        out_shape += [jax.ShapeDtypeStruct((D, T), FP8)] + [jax.ShapeDtypeStruct((1, T), F32)] * 2
        out_specs += [pl.BlockSpec((D, tm), lambda i: (0, i)), row, row]
    res = pl.pallas_call(
        _rmsnorm_kernel,
        grid=(T // tm,),
        in_specs=[pl.BlockSpec((tm, D), lambda i: (i, 0)), pl.BlockSpec((1, D), lambda i: (0, 0))],
        out_specs=out_specs,
        out_shape=out_shape,
        name="rmsnorm",
        compiler_params=_params("parallel"),
    )(x, g.reshape(1, D))
    return res if with_transpose else res[0]


def _mm_kernel(x_ref, w_ref, *rest, nk, has_res, has_scale, w_transposed):
    rest = list(rest)
    s_ref = rest.pop(0) if has_scale else None
    r_ref = rest.pop(0) if has_res else None
    o_ref = rest.pop(0)
    contract = (((1,), (1 if w_transposed else 0,)), ((), ()))
    part = lax.dot_general(x_ref[...], w_ref[...], contract, preferred_element_type=F32)

    def finish(acc):
        if has_scale:
            acc = acc * s_ref[...]
        if has_res:
            acc = acc + r_ref[...]
        o_ref[...] = acc.astype(o_ref.dtype)

    if nk == 1:
        finish(part)
        return
    acc_ref = rest.pop(0)
    k = pl.program_id(2)

    @pl.when(k == 0)
    def _():
        acc_ref[...] = part

    @pl.when(k > 0)
    def _():
        acc_ref[...] += part

    @pl.when(k == nk - 1)
    def _():
        finish(acc_ref[...])


def matmul(x, w, res=None, *, row_scale=None, layer=None, n_out=None, w_transposed=False, tm=1024, tn=1024,
           tk=4096, out_dtype=F32, name="matmul"):
    M, K = x.shape
    k_axis, n_axis = (-1, -2) if w_transposed else (-2, -1)
    N = w.shape[n_axis] if n_out is None else n_out
    tm, tn, tk = min(tm, M), min(tn, N), min(tk, K)
    assert M % tm == 0 and N % tn == 0 and K % tk == 0 and w.shape[k_axis] == K
    nk = K // tk
    wblock = (tn, tk) if w_transposed else (tk, tn)
    windex = (lambda i, j, k: (j, k)) if w_transposed else (lambda i, j, k: (k, j))
    if layer is None:
        wspec = pl.BlockSpec(wblock, windex)
    else:
        wspec = pl.BlockSpec((None,) + wblock, lambda i, j, k: (layer,) + windex(i, j, k))
    in_specs = [pl.BlockSpec((tm, tk), lambda i, j, k: (i, k)), wspec]
    args = [x, w]
    if row_scale is not None:
        in_specs.append(pl.BlockSpec((tm, 1), lambda i, j, k: (i, 0)))
        args.append(row_scale)
    if res is not None:
        in_specs.append(pl.BlockSpec((tm, tn), lambda i, j, k: (i, j)))
        args.append(res)
    return pl.pallas_call(
        functools.partial(_mm_kernel, nk=nk, has_res=res is not None, has_scale=row_scale is not None,
                          w_transposed=w_transposed),
        grid=(M // tm, N // tn, nk),
        in_specs=in_specs,
        out_specs=pl.BlockSpec((tm, tn), lambda i, j, k: (i, j)),
        out_shape=jax.ShapeDtypeStruct((M, N), out_dtype),
        scratch_shapes=[pltpu.VMEM((tm, tn), F32)] if nk > 1 else [],
        name=name,
        compiler_params=_params("parallel", "parallel", "arbitrary"),
    )(*args)


def _lru_kernel(x_ref, gate_ref, cw_ref, cb_ref, wa_ref, ba_ref, wx_ref, bx_ref, lam_ref, o_ref, *, seq):
    R = LRU_ROWS
    cw = cw_ref[...]
    cb = cb_ref[...]
    ba = ba_ref[...]
    bx = bx_ref[...]
    neg_c_softplus = -LRU_C * jax.nn.softplus(-lam_ref[...])
    wa = wa_ref[0]
    wx = wx_ref[0]
    row = lax.broadcasted_iota(jnp.int32, (R, LANES), 0)
    row8 = lax.broadcasted_iota(jnp.int32, (SUBLANES, LANES), 0)

    def body(c, h):
        r0 = pl.multiple_of(c * R, R)
        xt = x_ref[pl.ds(r0, R), :]
        p0 = pl.multiple_of(jnp.maximum(r0 - SUBLANES, 0), SUBLANES)
        prev = jnp.where(c > 0, x_ref[pl.ds(p0, SUBLANES), :], 0.0)
        xc = xt * cw[LRU_CONV - 1:LRU_CONV] + cb
        for d in range(1, LRU_CONV):
            rolled = pltpu.roll(xt, d, 0)
            head = jnp.where(row8 < d, pltpu.roll(prev, d, 0), rolled[:SUBLANES])
            shifted = jnp.concatenate([head, rolled[SUBLANES:]], axis=0)
            xc = xc + shifted * cw[LRU_CONV - 1 - d:LRU_CONV - d]
        xb = xc.astype(BF16)
        r = jax.nn.sigmoid(jnp.dot(xb, wa, preferred_element_type=F32) + ba)
        i = jax.nn.sigmoid(jnp.dot(xb, wx, preferred_element_type=F32) + bx)
        log_a = r * neg_c_softplus
        a = jnp.exp(log_a)
        b = jnp.sqrt(1.0 - a * a) * (i * xc)
        d = 1
        while d < R:
            keep = row >= d
            a_sh = jnp.where(keep, pltpu.roll(a, d, 0), 1.0)
            b_sh = jnp.where(keep, pltpu.roll(b, d, 0), 0.0)
            b = a * b_sh + b
            a = a * a_sh
            d *= 2
        hs = b + a * h
        o_ref[pl.ds(r0, R), :] = (hs * jax.nn.gelu(gate_ref[pl.ds(r0, R), :])).astype(o_ref.dtype)
        return hs[R - 1:R, :]

    lax.fori_loop(0, seq // R, body, jnp.zeros((1, LANES), F32))


def rglru(pm, batch, seq, x_col, gate_col, conv_w, conv_b, wa, ba, wx, bx, lam):
    T = batch * seq
    W = LRU_BLOCKS * LANES
    vec = lambda v: v.reshape(1, W)
    vspec = pl.BlockSpec((1, LANES), lambda b, n: (0, n))
    wspec = pl.BlockSpec((1, LANES, LANES), lambda b, n: (n, 0, 0))
    return pl.pallas_call(
        functools.partial(_lru_kernel, seq=seq),
        grid=(batch, LRU_BLOCKS),
        in_specs=[pl.BlockSpec((seq, LANES), lambda b, n: (b, x_col + n)),
                  pl.BlockSpec((seq, LANES), lambda b, n: (b, gate_col + n)),
                  pl.BlockSpec((LRU_CONV, LANES), lambda b, n: (0, n)),
                  vspec, wspec, vspec, wspec, vspec, vspec],
        out_specs=pl.BlockSpec((seq, LANES), lambda b, n: (b, n)),
        out_shape=jax.ShapeDtypeStruct((T, W), BF16),
        name="rglru",
        compiler_params=_params("parallel", "parallel"),
    )(pm, pm, conv_w, vec(conv_b), wa.astype(BF16), vec(ba), wx.astype(BF16), vec(bx), vec(lam))


def _gla_kernel(q_ref, k_ref, v_ref, og_ref, lr_ref, wg_ref, bg_ref, ng_ref, o_ref, st_ref):
    C = GLA_CHUNK
    NH = GLA_HEADS_PER_STEP
    dk = q_ref.shape[-1] // NH
    dv = v_ref.shape[-1] // NH

    @pl.when(pl.program_id(2) == 0)
    def _():
        st_ref[...] = jnp.zeros_like(st_ref)

    TT = GLA_TILE
    NC = TT // C
    ri = lax.broadcasted_iota(jnp.int32, (TT, TT), 0)
    ci = lax.broadcasted_iota(jnp.int32, (TT, TT), 1)
    causal = ((ri // C) == (ci // C)) & (ri >= ci)
    tril = jnp.where(causal, 1.0, 0.0).astype(BF16)
    scale = dk ** -0.5
    nt = (((1,), (1,)), ((), ()))
    tn = (((0,), (0,)), ((), ()))
    lr = lr_ref[...]

    for hh in range(NH):
        kcols = slice(hh * dk, (hh + 1) * dk)
        vcols = slice(hh * dv, (hh + 1) * dv)
        q = q_ref[:, kcols]
        k = k_ref[:, kcols]
        v = v_ref[:, vcols].astype(BF16)
        pre = jnp.dot(lr, wg_ref[hh], precision=lax.Precision.HIGHEST, preferred_element_type=F32) + bg_ref[hh]
        g = jax.nn.log_sigmoid(pre) / GLA_TAU
        g_hi = g.astype(BF16)
        g_lo = (g - g_hi.astype(F32)).astype(BF16)
        bc = (jnp.dot(tril, g_hi, preferred_element_type=F32) + jnp.dot(tril, g_lo, preferred_element_type=F32))
        b_last = jnp.broadcast_to(bc.reshape(NC, C, dk)[:, C - 1:C, :], (NC, C, dk)).reshape(TT, dk)
        qe = (q * scale * jnp.exp(bc)).astype(BF16)
        ke = (k * jnp.exp(-bc)).astype(BF16)
        kd = (k * jnp.exp(b_last - bc)).astype(BF16)
        att = lax.dot_general(qe, ke, nt, preferred_element_type=F32)
        att = jnp.where(causal, att, 0.0).astype(BF16)
        o = jnp.dot(att, v, preferred_element_type=F32)
        st = st_ref[hh]
        carried = []
        for c in range(NC):
            rows = slice(c * C, (c + 1) * C)
            carried.append(lax.dot_general(qe[rows], st.astype(BF16), nt, preferred_element_type=F32))
            decay = jnp.exp(b_last[c * C:c * C + 1, :])
            st = st * decay + lax.dot_general(v[rows], kd[rows], tn, preferred_element_type=F32)
        st_ref[hh] = st
        o = o + jnp.concatenate(carried, axis=0)
        o = o * lax.rsqrt(jnp.mean(o * o, axis=-1, keepdims=True) + RMS_EPS)
        o_ref[:, vcols] = (o * ng_ref[hh] * jax.nn.silu(og_ref[:, vcols])).astype(o_ref.dtype)


def gla(pm, pm_og, plr, batch, seq, q_col, k_col, v_col, og_col, wg2, bg, norm_g):
    T = batch * seq
    H = GLA_HEADS
    dk = wg2.shape[-1] // H
    dv = norm_g.shape[-1] // H
    assert dk == LANES and dv == 2 * LANES
    nt = seq // GLA_TILE
    wg = jnp.zeros((LANES, H * dk), F32).at[:GLA_GATE_RANK].set(wg2)
    wg = wg.reshape(LANES, H, dk).transpose(1, 0, 2)
    tok = lambda b, h, t: b * nt + t
    NH = GLA_HEADS_PER_STEP
    kb, vb = NH * dk // LANES, NH * dv // LANES
    assert H % NH == 0 and q_col % kb == 0 and k_col % kb == 0 and v_col % vb == 0 and og_col % vb == 0
    return pl.pallas_call(
        _gla_kernel,
        grid=(batch, H // NH, nt),
        in_specs=[pl.BlockSpec((GLA_TILE, NH * dk), lambda b, h, t: (tok(b, h, t), q_col // kb + h)),
                  pl.BlockSpec((GLA_TILE, NH * dk), lambda b, h, t: (tok(b, h, t), k_col // kb + h)),
                  pl.BlockSpec((GLA_TILE, NH * dv), lambda b, h, t: (tok(b, h, t), v_col // vb + h)),
                  pl.BlockSpec((GLA_TILE, NH * dv), lambda b, h, t: (tok(b, h, t), og_col // vb + h)),
                  pl.BlockSpec((GLA_TILE, LANES), lambda b, h, t: (tok(b, h, t), 0)),
                  pl.BlockSpec((NH, LANES, dk), lambda b, h, t: (h, 0, 0)),
                  pl.BlockSpec((NH, 1, dk), lambda b, h, t: (h, 0, 0)),
                  pl.BlockSpec((NH, 1, dv), lambda b, h, t: (h, 0, 0))],
        out_specs=pl.BlockSpec((GLA_TILE, NH * dv), lambda b, h, t: (tok(b, h, t), h)),
        out_shape=jax.ShapeDtypeStruct((T, H * dv), BF16),
        scratch_shapes=[pltpu.VMEM((NH, dv, dk), F32)],
        name="gla",
        compiler_params=_params("parallel", "parallel", "arbitrary"),
    )(pm, pm, pm, pm_og, plr, wg, bg.reshape(H, 1, dk), norm_g.reshape(H, 1, dv))


def _s5_kernel(u_ref, bcat_ref, ccat_ref, pr_ref, pi_ref, d_ref, z_ref, x_ref, carry_ref):
    TT = S5_TILE
    NS = 2 * S5_SLAB_STATES

    @pl.when(pl.program_id(2) == 0)
    def _():
        carry_ref[...] = jnp.zeros_like(carry_ref)

    u = u_ref[...]
    pr = pr_ref[0]
    pi = pi_ref[0]

    def swap(t):
        return pltpu.roll(t, S5_SLAB_STATES, 1)

    sub = lax.broadcasted_iota(jnp.int32, u.shape, 0) % SUBLANES
    lagged = [u.astype(BF16)]
    for d in range(1, SUBLANES):
        lagged.append(jnp.where(sub >= d, pltpu.roll(u, d, 0), 0.0).astype(BF16))
    x_ref[...] = jnp.dot(jnp.concatenate(lagged, axis=1), bcat_ref[0], preferred_element_type=F32)

    def carry_step(i, carry):
        rows = pl.ds(pl.multiple_of(i * SUBLANES, SUBLANES), SUBLANES)
        cb = jnp.broadcast_to(carry, (SUBLANES, NS))
        blk = x_ref[rows, :] + pr * cb + pi * swap(cb)
        x_ref[rows, :] = blk
        return blk[SUBLANES - 1:SUBLANES, :]

    carry_ref[...] = lax.fori_loop(0, TT // SUBLANES, carry_step, carry_ref[...], unroll=4)
    y = jnp.dot(x_ref[...].astype(BF16), ccat_ref[0], preferred_element_type=F32) + d_ref[...] * u
    z_ref[...] = jax.nn.gelu(y)


def s5_prepare(a_re, a_im, log_step, b_re, b_im, c_re, c_im):
    G, P = a_re.shape
    H = S5_GROUP
    ns = G // S5_SLAB_GROUPS
    step = jnp.exp(log_step)[:, None]
    mag = jnp.exp(a_re * step)
    ang = a_im * step
    abar_re, abar_im = mag * jnp.cos(ang), mag * jnp.sin(ang)
    den = a_re * a_re + a_im * a_im
    f_re = ((abar_re - 1.0) * a_re + abar_im * a_im) / den
    f_im = (abar_im * a_re - (abar_re - 1.0) * a_im) / den
    bb_re = f_re[..., None] * b_re - f_im[..., None] * b_im
    bb_im = f_re[..., None] * b_im + f_im[..., None] * b_re
    eye = jnp.eye(S5_SLAB_GROUPS, dtype=F32)

    def in_slab(bb):
        t = bb.reshape(ns, S5_SLAB_GROUPS, P, H)
        return jnp.einsum('sgph,gk->sghkp', t, eye).reshape(ns, LANES, S5_SLAB_STATES)

    def out_slab(cc):
        t = cc.reshape(ns, S5_SLAB_GROUPS, H, P)
        return jnp.einsum('sghp,gk->sgpkh', t, eye).reshape(ns, S5_SLAB_STATES, LANES)

    ccat = jnp.concatenate([out_slab(c_re), -out_slab(c_im)], axis=1).astype(BF16)
    pw_re, pw_im = [abar_re], [abar_im]
    for _ in range(SUBLANES - 1):
        r, i = pw_re[-1], pw_im[-1]
        pw_re.append(r * abar_re - i * abar_im)
        pw_im.append(r * abar_im + i * abar_re)
    slab = lambda t: jnp.stack(t, 0).reshape(SUBLANES, ns, S5_SLAB_STATES).transpose(1, 0, 2)
    pr, pi = slab(pw_re), slab(pw_im)
    bre, bim = in_slab(bb_re), in_slab(bb_im)
    lag_re = [bre] + [bre * pr[:, j:j + 1] - bim * pi[:, j:j + 1] for j in range(SUBLANES - 1)]
    lag_im = [bim] + [bre * pi[:, j:j + 1] + bim * pr[:, j:j + 1] for j in range(SUBLANES - 1)]
    bcat = jnp.concatenate([jnp.concatenate(lag_re, axis=1), jnp.concatenate(lag_im, axis=1)], axis=-1).astype(BF16)
    return bcat, ccat, jnp.concatenate([pr, pr], -1), jnp.concatenate([-pi, pi], -1)


def s5_ssm(pm, batch, seq, u_col, prep, d):
    T = batch * seq
    bcat, ccat, pr, pi = prep
    ns = bcat.shape[0]
    NS = 2 * S5_SLAB_STATES
    nt = seq // S5_TILE
    return pl.pallas_call(
        _s5_kernel,
        grid=(batch, ns, nt),
        in_specs=[pl.BlockSpec((S5_TILE, LANES), lambda b, s, t: (b * nt + t, u_col + s)),
                  pl.BlockSpec((1, SUBLANES * LANES, NS), lambda b, s, t: (s, 0, 0)),
                  pl.BlockSpec((1, NS, LANES), lambda b, s, t: (s, 0, 0)),
                  pl.BlockSpec((1, SUBLANES, NS), lambda b, s, t: (s, 0, 0)),
                  pl.BlockSpec((1, SUBLANES, NS), lambda b, s, t: (s, 0, 0)),
                  pl.BlockSpec((1, LANES), lambda b, s, t: (0, s))],
        out_specs=pl.BlockSpec((S5_TILE, LANES), lambda b, s, t: (b * nt + t, s)),
        out_shape=jax.ShapeDtypeStruct((T, ns * LANES), F32),
        scratch_shapes=[pltpu.VMEM((S5_TILE, NS), F32), pltpu.VMEM((1, NS), F32)],
        name="s5_ssm",
        compiler_params=_params("parallel", "parallel", "arbitrary"),
    )(pm, bcat, ccat, pr, pi, d.reshape(1, -1))


def _glu_kernel(z_ref, w_ref, b_ref, o_ref):
    z = z_ref[...]
    y = jnp.dot(z.astype(BF16), w_ref[...], preferred_element_type=F32) + b_ref[...]
    o_ref[...] = (z * jax.nn.sigmoid(y)).astype(o_ref.dtype)


def glu(z, w, b, tm=512):
    T, W = z.shape
    return pl.pallas_call(
        _glu_kernel,
        grid=(T // tm,),
        in_specs=[pl.BlockSpec((tm, W), lambda i: (i, 0)),
                  pl.BlockSpec((W, W), lambda i: (0, 0)),
                  pl.BlockSpec((1, W), lambda i: (0, 0))],
        out_specs=pl.BlockSpec((tm, W), lambda i: (i, 0)),
        out_shape=jax.ShapeDtypeStruct((T, W), BF16),
        name="s5_glu",
        compiler_params=_params("parallel"),
    )(z, w.astype(BF16), b.reshape(1, W))


def _rope(t, cos, sin_lo, sin_hi, rope_half):
    return (t * cos + pltpu.roll(t, LANES - rope_half, 1) * sin_lo + pltpu.roll(t, rope_half, 1) * sin_hi)


def rope_tables(seq, head_dim):
    rope_dims = head_dim // 4
    half = rope_dims // 2
    inv = jnp.power(ROPE_THETA, -jnp.arange(half, dtype=F32) / half)
    ang = jnp.arange(seq).astype(F32)[:, None] * inv[None, :]
    cos, sin = jnp.cos(ang), jnp.sin(ang)
    pad = head_dim - rope_dims
    zeros = jnp.zeros((seq, half), F32)
    cos_t = jnp.concatenate([cos, cos, jnp.ones((seq, pad), F32)], axis=-1)
    sin_lo = jnp.concatenate([-sin, zeros, jnp.zeros((seq, pad), F32)], axis=-1)
    sin_hi = jnp.concatenate([zeros, sin, jnp.zeros((seq, pad), F32)], axis=-1)
    return cos_t, sin_lo, sin_hi, half


def _moba_kv_kernel(k_ref, v_ref, cos_ref, slo_ref, shi_ref, kr_ref, km_ref, vt_ref, *, nb, rope_half):
    for n in range(nb):
        rows = pl.ds(n * MOBA_BLOCK, MOBA_BLOCK)
        kr = _rope(k_ref[rows, :], cos_ref[rows, :], slo_ref[rows, :], shi_ref[rows, :], rope_half)
        kr_ref[rows, :] = kr.astype(kr_ref.dtype)
        km_ref[0, 0, n:n + 1, :] = jnp.mean(kr, axis=0, keepdims=True)
        vt_ref[0, 0, n] = v_ref[rows, :].T.astype(vt_ref.dtype)


def _moba_attn_kernel(q_ref, cos_ref, slo_ref, shi_ref, kr_ref, vt_ref, km_ref, o_ref, sel_ref, acc_ref,
                      *, nb, rope_half):
    BLK = MOBA_BLOCK
    NH = MOBA_HEADS_PER_STEP
    i = pl.program_id(2)
    hd = q_ref.shape[-1] // NH
    log2_scale = hd ** -0.5 * math.log2(math.e)
    blk_id = lax.broadcasted_iota(jnp.int32, (nb, BLK), 0)
    key = lax.broadcasted_iota(jnp.int32, (BLK, BLK), 0)
    qry = lax.broadcasted_iota(jnp.int32, (BLK, BLK), 1)
    own = pl.ds(pl.multiple_of(i * BLK, BLK), BLK)
    cos, slo, shi = cos_ref[...], slo_ref[...], shi_ref[...]
    lanes = [slice(hh * hd, (hh + 1) * hd) for hh in range(NH)]

    qbs, stats = [], []
    for hh in range(NH):
        qt = _rope(q_ref[:, lanes[hh]], cos, slo, shi, rope_half).T
        gate = jnp.dot(km_ref[0, hh], qt, precision=lax.Precision.HIGHEST, preferred_element_type=F32)
        rank = jnp.zeros((nb, BLK), jnp.int32)
        for m in range(nb):
            gm = gate[m:m + 1, :]
            beats = (gm > gate) | ((gm == gate) & (m < blk_id))
            rank = rank + jnp.where(beats & (m < i), 1, 0)
        sel_ref[hh] = ((blk_id < i) & (rank < MOBA_TOPK)).astype(F32)
        qb = (qt * log2_scale).astype(BF16)
        s = jnp.dot(kr_ref[own, lanes[hh]], qb, preferred_element_type=F32)
        s = jnp.where(key <= qry, s, MASK_VALUE)
        m0 = jnp.max(s, axis=0, keepdims=True)
        p = jnp.exp2(s - m0)
        acc_ref[hh] = jnp.dot(vt_ref[0, hh, i], p.astype(BF16), preferred_element_type=F32)
        qbs.append(qb)
        stats += [m0, jnp.sum(p, axis=0, keepdims=True)]

    def body(g, carry):
        out = []
        for hh in range(NH):
            m_prev, l_prev = carry[2 * hh], carry[2 * hh + 1]
            scores = []
            for b in range(MOBA_GROUP):
                n = g * MOBA_GROUP + b
                rows = pl.ds(pl.multiple_of(n * BLK, BLK), BLK)
                sn = jnp.dot(kr_ref[rows, lanes[hh]], qbs[hh], preferred_element_type=F32)
                scores.append(jnp.where(sel_ref[hh, pl.ds(n, 1), :] > 0.0, sn, MASK_VALUE))
            m_new = m_prev
            for sn in scores:
                m_new = jnp.maximum(m_new, jnp.max(sn, axis=0, keepdims=True))
            alpha = jnp.exp2(m_prev - m_new)
            l_new = alpha * l_prev
            acc = alpha * acc_ref[hh]
            for b, sn in enumerate(scores):
                pn = jnp.exp2(sn - m_new)
                l_new = l_new + jnp.sum(pn, axis=0, keepdims=True)
                acc = acc + jnp.dot(vt_ref[0, hh, g * MOBA_GROUP + b], pn.astype(BF16), preferred_element_type=F32)
            acc_ref[hh] = acc
            out += [m_new, l_new]
        return tuple(out)

    groups = (i + MOBA_GROUP - 1) // MOBA_GROUP
    stats = lax.fori_loop(0, groups, body, tuple(stats))
    for hh in range(NH):
        o_ref[:, lanes[hh]] = (acc_ref[hh] / stats[2 * hh + 1]).T.astype(o_ref.dtype)


def moba(pm, batch, seq, q_col, k_col, v_col):
    T = batch * seq
    H, hd, BLK = MOBA_HEADS, LANES, MOBA_BLOCK
    nb = seq // BLK
    assert seq % BLK == 0 and nb % MOBA_GROUP == 0
    cos_t, sin_lo, sin_hi, half = rope_tables(seq, hd)
    full = pl.BlockSpec((seq, hd), lambda b, h: (0, 0))
    kr, kmean, vt = pl.pallas_call(
        functools.partial(_moba_kv_kernel, nb=nb, rope_half=half),
        grid=(batch, H),
        in_specs=[pl.BlockSpec((seq, hd), lambda b, h: (b, k_col + h)),
                  pl.BlockSpec((seq, hd), lambda b, h: (b, v_col + h)), full, full, full],
        out_specs=[pl.BlockSpec((seq, hd), lambda b, h: (b, h)),
                   pl.BlockSpec((1, 1, nb, hd), lambda b, h: (b, h, 0, 0)),
                   pl.BlockSpec((1, 1, nb, hd, BLK), lambda b, h: (b, h, 0, 0, 0))],
        out_shape=[jax.ShapeDtypeStruct((T, H * hd), BF16), jax.ShapeDtypeStruct((batch, H, nb, hd), F32),
                   jax.ShapeDtypeStruct((batch, H, nb, hd, BLK), BF16)],
        name="moba_kv",
        compiler_params=_params("parallel", "parallel"),
    )(pm, pm, cos_t, sin_lo, sin_hi)
    tab = pl.BlockSpec((BLK, hd), lambda b, h, i: (i, 0))
    NH = MOBA_HEADS_PER_STEP
    assert H % NH == 0 and q_col % NH == 0
    return pl.pallas_call(
        functools.partial(_moba_attn_kernel, nb=nb, rope_half=half),
        grid=(batch, H // NH, nb),
        in_specs=[pl.BlockSpec((BLK, NH * hd), lambda b, h, i: (b * nb + i, q_col // NH + h)), tab, tab, tab,
                  pl.BlockSpec((seq, NH * hd), lambda b, h, i: (b, h)),
                  pl.BlockSpec((1, NH, nb, hd, BLK), lambda b, h, i: (b, h, 0, 0, 0)),
                  pl.BlockSpec((1, NH, nb, hd), lambda b, h, i: (b, h, 0, 0))],
        out_specs=pl.BlockSpec((BLK, NH * hd), lambda b, h, i: (b * nb + i, h)),
        out_shape=jax.ShapeDtypeStruct((T, H * hd), BF16),
        scratch_shapes=[pltpu.VMEM((NH, nb, BLK), F32), pltpu.VMEM((NH, hd, BLK), F32)],
        name="moba_attn",
        compiler_params=_params("parallel", "parallel", "arbitrary"),
    )(pm, cos_t, sin_lo, sin_hi, kr, vt, kmean)


def _gelu_erf(x):
    return 0.5 * x * (1.0 + lax.erf(x * (2.0 ** -0.5)))


def _top_values(xs, count):
    vals = [[] for _ in xs]
    xs = list(xs)
    for _ in range(count):
        for j, x in enumerate(xs):
            m = jnp.max(x, axis=0, keepdims=True)
            vals[j].append(m)
            xs[j] = jnp.where(x >= m, -jnp.inf, x)
    return vals


def _peer_topk_kernel(q_ref, sk_ref, thr_ref, e1_ref, s2_ref, e2_ref):
    nt = (((1,), (1,)), ((), ()))
    K = PEER_TOPK
    q = q_ref[...]
    half = q.shape[-1] // 2
    s1 = lax.dot_general(sk_ref[0, 0], q[:, :half], nt, precision=lax.Precision.HIGHEST, preferred_element_type=F32)
    s2 = lax.dot_general(sk_ref[0, 1], q[:, half:], nt, precision=lax.Precision.HIGHEST, preferred_element_type=F32)
    v1, v2 = _top_values([s1, s2], K)
    v2_all = jnp.concatenate(v2, axis=0)
    cand = jnp.concatenate([v1[0] + v2_all] + [v1[a] + v2_all[:K // 2] for a in range(1, K)], axis=0)
    tau = _top_values([cand], K)[0][-1]
    top = v1[0] + v2[0]
    z = jnp.sum(jnp.where(cand >= tau, jnp.exp(cand - top), 0.0), axis=0, keepdims=True)
    thr = jnp.full_like(s1, jnp.inf)
    for b in range(K):
        thr = jnp.where(s1 + v2[b] >= tau, v2[b], thr)
    thr_ref[0] = jnp.where(s1 >= v1[K - 1], thr, jnp.inf)
    e1_ref[0] = jnp.exp(s1 - v1[0])
    s2_ref[0] = s2
    e2_ref[0] = jnp.exp(s2 - v2[0]) / z


def peer_topk(q, subkeys, tt=256):
    T = q.shape[0]
    H, _, NK, dh = subkeys.shape
    assert NK == PEER_NKEYS
    spec = pl.BlockSpec((1, NK, tt), lambda i, h: (h, 0, i))
    shape = jax.ShapeDtypeStruct((H, NK, T), F32)
    return pl.pallas_call(
        _peer_topk_kernel,
        grid=(T // tt, H),
        in_specs=[pl.BlockSpec((tt, 2 * dh), lambda i, h: (i, h)),
                  pl.BlockSpec((1, 2, NK, dh), lambda i, h: (h, 0, 0, 0))],
        out_specs=[spec] * 4,
        out_shape=[shape] * 4,
        name="peer_topk",
        compiler_params=_params("parallel", "parallel"),
    )(q, subkeys)


def _peer_gate_kernel(ht_ref, u_ref, scale_ref, wscale_ref, thr_ref, e1_ref, s2_ref, e2_ref, w_ref, act_ref, g_ref,
                      *, rows_per_tile):
    NK = PEER_NKEYS
    nchunks = ht_ref.shape[1] // LANES
    for s in range(rows_per_tile // PEER_SLICE_ROWS):
        rows = slice(s * PEER_SLICE_ROWS * NK, (s + 1) * PEER_SLICE_ROWS * NK)
        act_ref[rows, :] = jnp.dot(u_ref[rows, :], ht_ref[...], preferred_element_type=F32)
        for k1 in range(s * PEER_SLICE_ROWS, (s + 1) * PEER_SLICE_ROWS):
            for c in range(nchunks):
                cols = slice(c * LANES, (c + 1) * LANES)
                g = None
                for h in range(PEER_HEADS):
                    thr = thr_ref[h, k1:k1 + 1, cols]
                    e1 = e1_ref[h, k1:k1 + 1, cols]
                    term = jnp.where(s2_ref[h, :, cols] >= thr, e2_ref[h, :, cols], 0.0) * e1
                    g = term if g is None else g + term
                g_ref[k1 * NK:(k1 + 1) * NK, cols] = g
    for k1 in range(rows_per_tile):
        erows = slice(k1 * NK, (k1 + 1) * NK)
        for c in range(nchunks):
            cols = slice(c * LANES, (c + 1) * LANES)
            w = g_ref[erows, cols] * _gelu_erf(act_ref[erows, cols] * scale_ref[:, cols]) * wscale_ref[:, cols]
            w_ref[cols, erows] = w.T.astype(w_ref.dtype)


def peer_gate(ht, u, layer, scale, wscale, tables, tt=PEER_TOKEN_TILE, te=PEER_EXPERT_TILE):
    D, T = ht.shape
    E = u.shape[1]
    H, NK, _ = tables[0].shape
    rows_per_tile = te // NK
    assert rows_per_tile % SUBLANES == 0
    k1spec = pl.BlockSpec((H, rows_per_tile, tt), lambda i, j: (0, j, i))
    k2spec = pl.BlockSpec((H, NK, tt), lambda i, j: (0, 0, i))
    return pl.pallas_call(
        functools.partial(_peer_gate_kernel, rows_per_tile=rows_per_tile),
        grid=(T // tt, E // te),
        in_specs=[pl.BlockSpec((D, tt), lambda i, j: (0, i)),
                  pl.BlockSpec((None, te, D), lambda i, j: (layer, j, 0)),
                  pl.BlockSpec((1, tt), lambda i, j: (0, i)),
                  pl.BlockSpec((1, tt), lambda i, j: (0, i)),
                  k1spec, k1spec, k2spec, k2spec],
        out_specs=pl.BlockSpec((tt, te), lambda i, j: (i, j)),
        out_shape=jax.ShapeDtypeStruct((T, E), FP8),
        scratch_shapes=[pltpu.VMEM((te, tt), F32), pltpu.VMEM((te, tt), F32)],
        name="peer_gate",
        compiler_params=_params("parallel", "arbitrary"),
    )(ht, u, scale, wscale, *tables)


def quantise_table(w):
    amax = jnp.maximum(jnp.max(jnp.abs(w), axis=(1, 2)), FP8_TINY)
    row_norm = jnp.sqrt(jnp.max(jnp.sum(w * w, axis=2), axis=1))
    return (w * (FP8_TARGET_MAX / amax)[:, None, None]).astype(FP8), amax * (1.0 / FP8_TARGET_MAX), row_norm


def peer_ffn(x, layer, norm_g, wq, subkeys, u_quant, v_quant):
    u_fp8, u_scale, u_norm = u_quant
    v_fp8, v_scale, _ = v_quant
    h, ht, t_scale, h_norm = rmsnorm(x, norm_g, BF16, with_fp8_transpose=True)
    q = matmul(h, wq, layer=layer, name="peer_query")
    tables = peer_topk(q, subkeys)
    bound = jnp.maximum(PEER_HEADS * h_norm * u_norm[layer], FP8_TINY)
    w = peer_gate(ht, u_fp8, layer, t_scale * u_scale[layer], FP8_TARGET_MAX / bound, tables)
    undo = (bound * (v_scale[layer] / FP8_TARGET_MAX)).reshape(-1, 1)
    return matmul(w, v_fp8, res=x, row_scale=undo, layer=layer, name="peer_out")


def _in_proj_layout(group_width):
    gw = group_width
    parts = ((('lru_x', 'lru_gate', 'gla_q', 'gla_k', 'gla_v'), (gw, gw, gw // 2, gw // 2, gw)),
             (('gla_og', 's5_u', 'moba_q', 'moba_k', 'moba_v'), (gw,) * 5))
    cols = {}
    for names, sizes in parts:
        c = 0
        for name, size in zip(names, sizes):
            cols[name] = c // LANES
            c += size
    return sum(parts[0][1]), cols


def kernel(x, norm1_g, w_in, lru_conv_w, lru_conv_b, lru_wa, lru_ba, lru_wx, lru_bx, lru_lambda, gla_wg2, gla_bg, gla_norm_g, s5_a_re, s5_a_im, s5_log_step, s5_b_re, s5_b_im, s5_c_re, s5_c_im, s5_d, s5_w_glu, s5_b_glu, w_out, norm2_g, peer_wq, peer_subkeys, peer_u, peer_v, final_norm_g):
    B, S, D = x.shape
    T = B * S
    depth = norm1_g.shape[0]
    gw = D // N_MIXERS
    x = x.reshape(T, D)
    lr0, col = _in_proj_layout(gw)
    w_in_t = jnp.transpose(w_in, (0, 2, 1)).astype(BF16)
    w_b = w_in_t[:, lr0 + GLA_GATE_RANK:]
    w_lr = jnp.zeros((depth, LANES, D), BF16).at[:, :GLA_GATE_RANK].set(w_in_t[:, lr0:lr0 + GLA_GATE_RANK])
    w_out_bf16 = w_out.astype(BF16)
    peer_wq_bf16 = peer_wq.astype(BF16)
    peer_u_quant = quantise_table(peer_u)
    peer_v_quant = quantise_table(peer_v)
    for l in range(depth):
        h = rmsnorm(x, norm1_g[l], BF16)
        pa = matmul(h, w_in_t, layer=l, n_out=lr0, w_transposed=True, name="in_proj_a")
        pb = matmul(h, w_b, layer=l, w_transposed=True, name="in_proj_b")
        plr = matmul(h, w_lr, layer=l, w_transposed=True, name="in_proj_lr")
        y_a = rglru(pa, B, S, col['lru_x'], col['lru_gate'], lru_conv_w[l], lru_conv_b[l], lru_wa[l], lru_ba[l],
                    lru_wx[l], lru_bx[l], lru_lambda[l])
        y_b = gla(pa, pb, plr, B, S, col['gla_q'], col['gla_k'], col['gla_v'], col['gla_og'],
                  gla_wg2[l], gla_bg[l], gla_norm_g[l])
        prep = s5_prepare(s5_a_re[l], s5_a_im[l], s5_log_step[l], s5_b_re[l], s5_b_im[l], s5_c_re[l], s5_c_im[l])
        z = s5_ssm(pb, B, S, col['s5_u'], prep, s5_d[l])
        y_c = glu(z, s5_w_glu[l], s5_b_glu[l])
        y_d = moba(pb, B, S, col['moba_q'], col['moba_k'], col['moba_v'])
        mixed = jnp.concatenate([y_a, y_b, y_c, y_d], axis=-1)
        x = matmul(mixed, w_out_bf16, res=x, layer=l, name="out_proj")
        x = peer_ffn(x, l, norm2_g[l], peer_wq_bf16, peer_subkeys[l], peer_u_quant, peer_v_quant)
    return rmsnorm(x, final_norm_g, F32).reshape(B, S, D)
```

```python
import functools
import math

import jax
import jax.numpy as jnp
from jax import lax
from jax.experimental import pallas as pl
from jax.experimental.pallas import tpu as pltpu

F32 = jnp.float32
BF16 = jnp.bfloat16
FP8 = jnp.float8_e4m3fn
FP8_TARGET_MAX = 256.0
FP8_TINY = 1e-30

LANES = 128
SUBLANES = 8
VMEM_LIMIT_BYTES = 56 * 2**20

RMS_EPS = 1e-6
N_MIXERS = 4

LRU_BLOCKS = 8
LRU_CONV = 4
LRU_C = 8.0
LRU_ROWS = 256

GLA_HEADS = 4
GLA_GATE_RANK = 16
GLA_TAU = 16.0
GLA_CHUNK = 64
GLA_TILE = 512
GLA_HEADS_PER_STEP = 2

S5_GROUP = 16
S5_STATE = 64
S5_SLAB_GROUPS = LANES // S5_GROUP
S5_SLAB_STATES = S5_SLAB_GROUPS * S5_STATE
S5_TILE = 256

MOBA_HEADS = 8
MOBA_BLOCK = 256
MOBA_TOPK = 3
MOBA_GROUP = 4
MOBA_HEADS_PER_STEP = 8
ROPE_THETA = 500000.0
MASK_VALUE = -1e30

PEER_HEADS = 8
PEER_NKEYS = 128
PEER_TOPK = 16
PEER_TOKEN_TILE = 1024
PEER_EXPERT_TILE = 1024
PEER_SLICE_ROWS = 2


def _params(*semantics):
    return pltpu.CompilerParams(dimension_semantics=semantics, vmem_limit_bytes=VMEM_LIMIT_BYTES)


def _rmsnorm_kernel(x_ref, g_ref, o_ref, *maybe_quantised):
    x = x_ref[...]
    y = x * lax.rsqrt(jnp.mean(x * x, axis=-1, keepdims=True) + RMS_EPS) * g_ref[...]
    o_ref[...] = y.astype(o_ref.dtype)
    if maybe_quantised:
        qt_ref, scale_ref, norm_ref = maybe_quantised
        yt = y.T
        amax = jnp.maximum(jnp.max(jnp.abs(yt), axis=0, keepdims=True), FP8_TINY)
        qt_ref[...] = (yt * (FP8_TARGET_MAX / amax)).astype(qt_ref.dtype)
        scale_ref[...] = amax * (1.0 / FP8_TARGET_MAX)
        norm_ref[...] = jnp.sqrt(jnp.sum(yt * yt, axis=0, keepdims=True))


def rmsnorm(x, g, out_dtype, with_fp8_transpose=False, tm=256):
    T, D = x.shape
    out_shape = [jax.ShapeDtypeStruct((T, D), out_dtype)]
    out_specs = [pl.BlockSpec((tm, D), lambda i: (i, 0))]
    with_transpose = with_fp8_transpose
    if with_transpose:
        row = pl.BlockSpec((1, tm), lambda i: (0, i))
        out_shape += [jax.ShapeDtypeStruct((D, T), FP8)] + [jax.ShapeDtypeStruct((1, T), F32)] * 2
        out_specs += [pl.BlockSpec((D, tm), lambda i: (0, i)), row, row]
    res = pl.pallas_call(
        _rmsnorm_kernel,
        grid=(T // tm,),
        in_specs=[pl.BlockSpec((tm, D), lambda i: (i, 0)), pl.BlockSpec((1, D), lambda i: (0, 0))],
        out_specs=out_specs,
        out_shape=out_shape,
        name="rmsnorm",
        compiler_params=_params("parallel"),
    )(x, g.reshape(1, D))
    return res if with_transpose else res[0]


def _mm_kernel(x_ref, w_ref, *rest, nk, has_res, has_scale, w_transposed):
    rest = list(rest)
    s_ref = rest.pop(0) if has_scale else None
    r_ref = rest.pop(0) if has_res else None
    o_ref = rest.pop(0)
    contract = (((1,), (1 if w_transposed else 0,)), ((), ()))
    part = lax.dot_general(x_ref[...], w_ref[...], contract, preferred_element_type=F32)

    def finish(acc):
        if has_scale:
            acc = acc * s_ref[...]
        if has_res:
            acc = acc + r_ref[...]
        o_ref[...] = acc.astype(o_ref.dtype)

    if nk == 1:
        finish(part)
        return
    acc_ref = rest.pop(0)
    k = pl.program_id(2)

    @pl.when(k == 0)
    def _():
        acc_ref[...] = part

    @pl.when(k > 0)
    def _():
        acc_ref[...] += part

    @pl.when(k == nk - 1)
    def _():
        finish(acc_ref[...])


def matmul(x, w, res=None, *, row_scale=None, layer=None, n_out=None, w_transposed=False, tm=1024, tn=1024,
           tk=4096, out_dtype=F32, name="matmul"):
    M, K = x.shape
    k_axis, n_axis = (-1, -2) if w_transposed else (-2, -1)
    N = w.shape[n_axis] if n_out is None else n_out
    tm, tn, tk = min(tm, M), min(tn, N), min(tk, K)
    assert M % tm == 0 and N % tn == 0 and K % tk == 0 and w.shape[k_axis] == K
    nk = K // tk
    wblock = (tn, tk) if w_transposed else (tk, tn)
    windex = (lambda i, j, k: (j, k)) if w_transposed else (lambda i, j, k: (k, j))
    if layer is None:
        wspec = pl.BlockSpec(wblock, windex)
    else:
        wspec = pl.BlockSpec((None,) + wblock, lambda i, j, k: (layer,) + windex(i, j, k))
    in_specs = [pl.BlockSpec((tm, tk), lambda i, j, k: (i, k)), wspec]
    args = [x, w]
    if row_scale is not None:
        in_specs.append(pl.BlockSpec((tm, 1), lambda i, j, k: (i, 0)))
        args.append(row_scale)
    if res is not None:
        in_specs.append(pl.BlockSpec((tm, tn), lambda i, j, k: (i, j)))
        args.append(res)
    return pl.pallas_call(
        functools.partial(_mm_kernel, nk=nk, has_res=res is not None, has_scale=row_scale is not None,
                          w_transposed=w_transposed),
        grid=(M // tm, N // tn, nk),
        in_specs=in_specs,
        out_specs=pl.BlockSpec((tm, tn), lambda i, j, k: (i, j)),
        out_shape=jax.ShapeDtypeStruct((M, N), out_dtype),
        scratch_shapes=[pltpu.VMEM((tm, tn), F32)] if nk > 1 else [],
        name=name,
        compiler_params=_params("parallel", "parallel", "arbitrary"),
    )(*args)


def _lru_kernel(x_ref, gate_ref, cw_ref, cb_ref, wa_ref, ba_ref, wx_ref, bx_ref, lam_ref, o_ref, *, seq):
    R = LRU_ROWS
    cw = cw_ref[...]
    cb = cb_ref[...]
    ba = ba_ref[...]
    bx = bx_ref[...]
    neg_c_softplus = -LRU_C * jax.nn.softplus(-lam_ref[...])
    wa = wa_ref[0]
    wx = wx_ref[0]
    row = lax.broadcasted_iota(jnp.int32, (R, LANES), 0)
    row8 = lax.broadcasted_iota(jnp.int32, (SUBLANES, LANES), 0)

    def body(c, h):
        r0 = pl.multiple_of(c * R, R)
        xt = x_ref[pl.ds(r0, R), :]
        p0 = pl.multiple_of(jnp.maximum(r0 - SUBLANES, 0), SUBLANES)
        prev = jnp.where(c > 0, x_ref[pl.ds(p0, SUBLANES), :], 0.0)
        xc = xt * cw[LRU_CONV - 1:LRU_CONV] + cb
        for d in range(1, LRU_CONV):
            rolled = pltpu.roll(xt, d, 0)
            head = jnp.where(row8 < d, pltpu.roll(prev, d, 0), rolled[:SUBLANES])
            shifted = jnp.concatenate([head, rolled[SUBLANES:]], axis=0)
            xc = xc + shifted * cw[LRU_CONV - 1 - d:LRU_CONV - d]
        xb = xc.astype(BF16)
        r = jax.nn.sigmoid(jnp.dot(xb, wa, preferred_element_type=F32) + ba)
        i = jax.nn.sigmoid(jnp.dot(xb, wx, preferred_element_type=F32) + bx)
        log_a = r * neg_c_softplus
        a = jnp.exp(log_a)
        b = jnp.sqrt(1.0 - a * a) * (i * xc)
        d = 1
        while d < R:
            keep = row >= d
            a_sh = jnp.where(keep, pltpu.roll(a, d, 0), 1.0)
            b_sh = jnp.where(keep, pltpu.roll(b, d, 0), 0.0)
            b = a * b_sh + b
            a = a * a_sh
            d *= 2
        hs = b + a * h
        o_ref[pl.ds(r0, R), :] = (hs * jax.nn.gelu(gate_ref[pl.ds(r0, R), :])).astype(o_ref.dtype)
        return hs[R - 1:R, :]

    lax.fori_loop(0, seq // R, body, jnp.zeros((1, LANES), F32))


def rglru(pm, batch, seq, x_col, gate_col, conv_w, conv_b, wa, ba, wx, bx, lam):
    T = batch * seq
    W = LRU_BLOCKS * LANES
    vec = lambda v: v.reshape(1, W)
    vspec = pl.BlockSpec((1, LANES), lambda b, n: (0, n))
    wspec = pl.BlockSpec((1, LANES, LANES), lambda b, n: (n, 0, 0))
    return pl.pallas_call(
        functools.partial(_lru_kernel, seq=seq),
        grid=(batch, LRU_BLOCKS),
        in_specs=[pl.BlockSpec((seq, LANES), lambda b, n: (b, x_col + n)),
                  pl.BlockSpec((seq, LANES), lambda b, n: (b, gate_col + n)),
                  pl.BlockSpec((LRU_CONV, LANES), lambda b, n: (0, n)),
                  vspec, wspec, vspec, wspec, vspec, vspec],
        out_specs=pl.BlockSpec((seq, LANES), lambda b, n: (b, n)),
        out_shape=jax.ShapeDtypeStruct((T, W), BF16),
        name="rglru",
        compiler_params=_params("parallel", "parallel"),
    )(pm, pm, conv_w, vec(conv_b), wa.astype(BF16), vec(ba), wx.astype(BF16), vec(bx), vec(lam))


def _gla_kernel(q_ref, k_ref, v_ref, og_ref, lr_ref, wg_ref, bg_ref, ng_ref, o_ref, st_ref):
    C = GLA_CHUNK
    NH = GLA_HEADS_PER_STEP
    dk = q_ref.shape[-1] // NH
    dv = v_ref.shape[-1] // NH

    @pl.when(pl.program_id(2) == 0)
    def _():
        st_ref[...] = jnp.zeros_like(st_ref)

    TT = GLA_TILE
    NC = TT // C
    ri = lax.broadcasted_iota(jnp.int32, (TT, TT), 0)
    ci = lax.broadcasted_iota(jnp.int32, (TT, TT), 1)
    causal = ((ri // C) == (ci // C)) & (ri >= ci)
    tril = jnp.where(causal, 1.0, 0.0).astype(BF16)
    scale = dk ** -0.5
    nt = (((1,), (1,)), ((), ()))
    tn = (((0,), (0,)), ((), ()))
    lr = lr_ref[...]

    for hh in range(NH):
        kcols = slice(hh * dk, (hh + 1) * dk)
        vcols = slice(hh * dv, (hh + 1) * dv)
        q = q_ref[:, kcols]
        k = k_ref[:, kcols]
        v = v_ref[:, vcols].astype(BF16)
        pre = jnp.dot(lr, wg_ref[hh], precision=lax.Precision.HIGHEST, preferred_element_type=F32) + bg_ref[hh]
        g = jax.nn.log_sigmoid(pre) / GLA_TAU
        g_hi = g.astype(BF16)
        g_lo = (g - g_hi.astype(F32)).astype(BF16)
        bc = (jnp.dot(tril, g_hi, preferred_element_type=F32) + jnp.dot(tril, g_lo, preferred_element_type=F32))
        b_last = jnp.broadcast_to(bc.reshape(NC, C, dk)[:, C - 1:C, :], (NC, C, dk)).reshape(TT, dk)
        qe = (q * scale * jnp.exp(bc)).astype(BF16)
        ke = (k * jnp.exp(-bc)).astype(BF16)
        kd = (k * jnp.exp(b_last - bc)).astype(BF16)
        att = lax.dot_general(qe, ke, nt, preferred_element_type=F32)
        att = jnp.where(causal, att, 0.0).astype(BF16)
        o = jnp.dot(att, v, preferred_element_type=F32)
        st = st_ref[hh]
        carried = []
        for c in range(NC):
            rows = slice(c * C, (c + 1) * C)
            carried.append(lax.dot_general(qe[rows], st.astype(BF16), nt, preferred_element_type=F32))
            decay = jnp.exp(b_last[c * C:c * C + 1, :])
            st = st * decay + lax.dot_general(v[rows], kd[rows], tn, preferred_element_type=F32)
        st_ref[hh] = st
        o = o + jnp.concatenate(carried, axis=0)
        o = o * lax.rsqrt(jnp.mean(o * o, axis=-1, keepdims=True) + RMS_EPS)
        o_ref[:, vcols] = (o * ng_ref[hh] * jax.nn.silu(og_ref[:, vcols])).astype(o_ref.dtype)


def gla(pm, pm_og, plr, batch, seq, q_col, k_col, v_col, og_col, wg2, bg, norm_g):
    T = batch * seq
    H = GLA_HEADS
    dk = wg2.shape[-1] // H
    dv = norm_g.shape[-1] // H
    assert dk == LANES and dv == 2 * LANES
    nt = seq // GLA_TILE
    wg = jnp.zeros((LANES, H * dk), F32).at[:GLA_GATE_RANK].set(wg2)
    wg = wg.reshape(LANES, H, dk).transpose(1, 0, 2)
    tok = lambda b, h, t: b * nt + t
    NH = GLA_HEADS_PER_STEP
    kb, vb = NH * dk // LANES, NH * dv // LANES
    assert H % NH == 0 and q_col % kb == 0 and k_col % kb == 0 and v_col % vb == 0 and og_col % vb == 0
    return pl.pallas_call(
        _gla_kernel,
        grid=(batch, H // NH, nt),
        in_specs=[pl.BlockSpec((GLA_TILE, NH * dk), lambda b, h, t: (tok(b, h, t), q_col // kb + h)),
                  pl.BlockSpec((GLA_TILE, NH * dk), lambda b, h, t: (tok(b, h, t), k_col // kb + h)),
                  pl.BlockSpec((GLA_TILE, NH * dv), lambda b, h, t: (tok(b, h, t), v_col // vb + h)),
                  pl.BlockSpec((GLA_TILE, NH * dv), lambda b, h, t: (tok(b, h, t), og_col // vb + h)),
                  pl.BlockSpec((GLA_TILE, LANES), lambda b, h, t: (tok(b, h, t), 0)),
                  pl.BlockSpec((NH, LANES, dk), lambda b, h, t: (h, 0, 0)),
                  pl.BlockSpec((NH, 1, dk), lambda b, h, t: (h, 0, 0)),
                  pl.BlockSpec((NH, 1, dv), lambda b, h, t: (h, 0, 0))],
        out_specs=pl.BlockSpec((GLA_TILE, NH * dv), lambda b, h, t: (tok(b, h, t), h)),
        out_shape=jax.ShapeDtypeStruct((T, H * dv), BF16),
        scratch_shapes=[pltpu.VMEM((NH, dv, dk), F32)],
        name="gla",
        compiler_params=_params("parallel", "parallel", "arbitrary"),
    )(pm, pm, pm, pm_og, plr, wg, bg.reshape(H, 1, dk), norm_g.reshape(H, 1, dv))


def _s5_kernel(u_ref, bcat_ref, ccat_ref, pr_ref, pi_ref, d_ref, z_ref, x_ref, carry_ref):
    TT = S5_TILE
    NS = 2 * S5_SLAB_STATES

    @pl.when(pl.program_id(2) == 0)
    def _():
        carry_ref[...] = jnp.zeros_like(carry_ref)

    u = u_ref[...]
    pr = pr_ref[0]
    pi = pi_ref[0]

    def swap(t):
        return pltpu.roll(t, S5_SLAB_STATES, 1)

    sub = lax.broadcasted_iota(jnp.int32, u.shape, 0) % SUBLANES
    lagged = [u.astype(BF16)]
    for d in range(1, SUBLANES):
        lagged.append(jnp.where(sub >= d, pltpu.roll(u, d, 0), 0.0).astype(BF16))
    x_ref[...] = jnp.dot(jnp.concatenate(lagged, axis=1), bcat_ref[0], preferred_element_type=F32)

    def carry_step(i, carry):
        rows = pl.ds(pl.multiple_of(i * SUBLANES, SUBLANES), SUBLANES)
        cb = jnp.broadcast_to(carry, (SUBLANES, NS))
        blk = x_ref[rows, :] + pr * cb + pi * swap(cb)
        x_ref[rows, :] = blk
        return blk[SUBLANES - 1:SUBLANES, :]

    carry_ref[...] = lax.fori_loop(0, TT // SUBLANES, carry_step, carry_ref[...], unroll=4)
    y = jnp.dot(x_ref[...].astype(BF16), ccat_ref[0], preferred_element_type=F32) + d_ref[...] * u
    z_ref[...] = jax.nn.gelu(y)


def s5_prepare(a_re, a_im, log_step, b_re, b_im, c_re, c_im):
    G, P = a_re.shape
    H = S5_GROUP
    ns = G // S5_SLAB_GROUPS
    step = jnp.exp(log_step)[:, None]
    mag = jnp.exp(a_re * step)
    ang = a_im * step
    abar_re, abar_im = mag * jnp.cos(ang), mag * jnp.sin(ang)
    den = a_re * a_re + a_im * a_im
    f_re = ((abar_re - 1.0) * a_re + abar_im * a_im) / den
    f_im = (abar_im * a_re - (abar_re - 1.0) * a_im) / den
    bb_re = f_re[..., None] * b_re - f_im[..., None] * b_im
    bb_im = f_re[..., None] * b_im + f_im[..., None] * b_re
    eye = jnp.eye(S5_SLAB_GROUPS, dtype=F32)

    def in_slab(bb):
        t = bb.reshape(ns, S5_SLAB_GROUPS, P, H)
        return jnp.einsum('sgph,gk->sghkp', t, eye).reshape(ns, LANES, S5_SLAB_STATES)

    def out_slab(cc):
        t = cc.reshape(ns, S5_SLAB_GROUPS, H, P)
        return jnp.einsum('sghp,gk->sgpkh', t, eye).reshape(ns, S5_SLAB_STATES, LANES)

    ccat = jnp.concatenate([out_slab(c_re), -out_slab(c_im)], axis=1).astype(BF16)
    pw_re, pw_im = [abar_re], [abar_im]
    for _ in range(SUBLANES - 1):
        r, i = pw_re[-1], pw_im[-1]
        pw_re.append(r * abar_re - i * abar_im)
        pw_im.append(r * abar_im + i * abar_re)
    slab = lambda t: jnp.stack(t, 0).reshape(SUBLANES, ns, S5_SLAB_STATES).transpose(1, 0, 2)
    pr, pi = slab(pw_re), slab(pw_im)
    bre, bim = in_slab(bb_re), in_slab(bb_im)
    lag_re = [bre] + [bre * pr[:, j:j + 1] - bim * pi[:, j:j + 1] for j in range(SUBLANES - 1)]
    lag_im = [bim] + [bre * pi[:, j:j + 1] + bim * pr[:, j:j + 1] for j in range(SUBLANES - 1)]
    bcat = jnp.concatenate([jnp.concatenate(lag_re, axis=1), jnp.concatenate(lag_im, axis=1)], axis=-1).astype(BF16)
    return bcat, ccat, jnp.concatenate([pr, pr], -1), jnp.concatenate([-pi, pi], -1)


def s5_ssm(pm, batch, seq, u_col, prep, d):
    T = batch * seq
    bcat, ccat, pr, pi = prep
    ns = bcat.shape[0]
    NS = 2 * S5_SLAB_STATES
    nt = seq // S5_TILE
    return pl.pallas_call(
        _s5_kernel,
        grid=(batch, ns, nt),
        in_specs=[pl.BlockSpec((S5_TILE, LANES), lambda b, s, t: (b * nt + t, u_col + s)),
                  pl.BlockSpec((1, SUBLANES * LANES, NS), lambda b, s, t: (s, 0, 0)),
                  pl.BlockSpec((1, NS, LANES), lambda b, s, t: (s, 0, 0)),
                  pl.BlockSpec((1, SUBLANES, NS), lambda b, s, t: (s, 0, 0)),
                  pl.BlockSpec((1, SUBLANES, NS), lambda b, s, t: (s, 0, 0)),
                  pl.BlockSpec((1, LANES), lambda b, s, t: (0, s))],
        out_specs=pl.BlockSpec((S5_TILE, LANES), lambda b, s, t: (b * nt + t, s)),
        out_shape=jax.ShapeDtypeStruct((T, ns * LANES), F32),
        scratch_shapes=[pltpu.VMEM((S5_TILE, NS), F32), pltpu.VMEM((1, NS), F32)],
        name="s5_ssm",
        compiler_params=_params("parallel", "parallel", "arbitrary"),
    )(pm, bcat, ccat, pr, pi, d.reshape(1, -1))


def _glu_kernel(z_ref, w_ref, b_ref, o_ref):
    z = z_ref[...]
    y = jnp.dot(z.astype(BF16), w_ref[...], preferred_element_type=F32) + b_ref[...]
    o_ref[...] = (z * jax.nn.sigmoid(y)).astype(o_ref.dtype)


def glu(z, w, b, tm=512):
    T, W = z.shape
    return pl.pallas_call(
        _glu_kernel,
        grid=(T // tm,),
        in_specs=[pl.BlockSpec((tm, W), lambda i: (i, 0)),
                  pl.BlockSpec((W, W), lambda i: (0, 0)),
                  pl.BlockSpec((1, W), lambda i: (0, 0))],
        out_specs=pl.BlockSpec((tm, W), lambda i: (i, 0)),
        out_shape=jax.ShapeDtypeStruct((T, W), BF16),
        name="s5_glu",
        compiler_params=_params("parallel"),
    )(z, w.astype(BF16), b.reshape(1, W))


def _rope(t, cos, sin_lo, sin_hi, rope_half):
    return (t * cos + pltpu.roll(t, LANES - rope_half, 1) * sin_lo + pltpu.roll(t, rope_half, 1) * sin_hi)


def rope_tables(seq, head_dim):
    rope_dims = head_dim // 4
    half = rope_dims // 2
    inv = jnp.power(ROPE_THETA, -jnp.arange(half, dtype=F32) / half)
    ang = jnp.arange(seq).astype(F32)[:, None] * inv[None, :]
    cos, sin = jnp.cos(ang), jnp.sin(ang)
    pad = head_dim - rope_dims
    zeros = jnp.zeros((seq, half), F32)
    cos_t = jnp.concatenate([cos, cos, jnp.ones((seq, pad), F32)], axis=-1)
    sin_lo = jnp.concatenate([-sin, zeros, jnp.zeros((seq, pad), F32)], axis=-1)
    sin_hi = jnp.concatenate([zeros, sin, jnp.zeros((seq, pad), F32)], axis=-1)
    return cos_t, sin_lo, sin_hi, half


def _moba_kv_kernel(k_ref, v_ref, cos_ref, slo_ref, shi_ref, kr_ref, km_ref, vt_ref, *, nb, rope_half):
    for n in range(nb):
        rows = pl.ds(n * MOBA_BLOCK, MOBA_BLOCK)
        kr = _rope(k_ref[rows, :], cos_ref[rows, :], slo_ref[rows, :], shi_ref[rows, :], rope_half)
        kr_ref[rows, :] = kr.astype(kr_ref.dtype)
        km_ref[0, 0, n:n + 1, :] = jnp.mean(kr, axis=0, keepdims=True)
        vt_ref[0, 0, n] = v_ref[rows, :].T.astype(vt_ref.dtype)


def _moba_attn_kernel(q_ref, cos_ref, slo_ref, shi_ref, kr_ref, vt_ref, km_ref, o_ref, sel_ref, acc_ref,
                      *, nb, rope_half):
    BLK = MOBA_BLOCK
    NH = MOBA_HEADS_PER_STEP
    i = pl.program_id(2)
    hd = q_ref.shape[-1] // NH
    log2_scale = hd ** -0.5 * math.log2(math.e)
    blk_id = lax.broadcasted_iota(jnp.int32, (nb, BLK), 0)
    key = lax.broadcasted_iota(jnp.int32, (BLK, BLK), 0)
    qry = lax.broadcasted_iota(jnp.int32, (BLK, BLK), 1)
    own = pl.ds(pl.multiple_of(i * BLK, BLK), BLK)
    cos, slo, shi = cos_ref[...], slo_ref[...], shi_ref[...]
    lanes = [slice(hh * hd, (hh + 1) * hd) for hh in range(NH)]

    qbs, stats = [], []
    for hh in range(NH):
        qt = _rope(q_ref[:, lanes[hh]], cos, slo, shi, rope_half).T
        gate = jnp.dot(km_ref[0, hh], qt, precision=lax.Precision.HIGHEST, preferred_element_type=F32)
        rank = jnp.zeros((nb, BLK), jnp.int32)
        for m in range(nb):
            gm = gate[m:m + 1, :]
            beats = (gm > gate) | ((gm == gate) & (m < blk_id))
            rank = rank + jnp.where(beats & (m < i), 1, 0)
        sel_ref[hh] = ((blk_id < i) & (rank < MOBA_TOPK)).astype(F32)
        qb = (qt * log2_scale).astype(BF16)
        s = jnp.dot(kr_ref[own, lanes[hh]], qb, preferred_element_type=F32)
        s = jnp.where(key <= qry, s, MASK_VALUE)
        m0 = jnp.max(s, axis=0, keepdims=True)
        p = jnp.exp2(s - m0)
        acc_ref[hh] = jnp.dot(vt_ref[0, hh, i], p.astype(BF16), preferred_element_type=F32)
        qbs.append(qb)
        stats += [m0, jnp.sum(p, axis=0, keepdims=True)]

    def body(g, carry):
        out = []
        for hh in range(NH):
            m_prev, l_prev = carry[2 * hh], carry[2 * hh + 1]
            scores = []
            for b in range(MOBA_GROUP):
                n = g * MOBA_GROUP + b
                rows = pl.ds(pl.multiple_of(n * BLK, BLK), BLK)
                sn = jnp.dot(kr_ref[rows, lanes[hh]], qbs[hh], preferred_element_type=F32)
                scores.append(jnp.where(sel_ref[hh, pl.ds(n, 1), :] > 0.0, sn, MASK_VALUE))
            m_new = m_prev
            for sn in scores:
                m_new = jnp.maximum(m_new, jnp.max(sn, axis=0, keepdims=True))
            alpha = jnp.exp2(m_prev - m_new)
            l_new = alpha * l_prev
            acc = alpha * acc_ref[hh]
            for b, sn in enumerate(scores):
                pn = jnp.exp2(sn - m_new)
                l_new = l_new + jnp.sum(pn, axis=0, keepdims=True)
                acc = acc + jnp.dot(vt_ref[0, hh, g * MOBA_GROUP + b], pn.astype(BF16), preferred_element_type=F32)
            acc_ref[hh] = acc
            out += [m_new, l_new]
        return tuple(out)

    groups = (i + MOBA_GROUP - 1) // MOBA_GROUP
    stats = lax.fori_loop(0, groups, body, tuple(stats))
    for hh in range(NH):
        o_ref[:, lanes[hh]] = (acc_ref[hh] / stats[2 * hh + 1]).T.astype(o_ref.dtype)


def moba(pm, batch, seq, q_col, k_col, v_col):
    T = batch * seq
    H, hd, BLK = MOBA_HEADS, LANES, MOBA_BLOCK
    nb = seq // BLK
    assert seq % BLK == 0 and nb % MOBA_GROUP == 0
    cos_t, sin_lo, sin_hi, half = rope_tables(seq, hd)
    full = pl.BlockSpec((seq, hd), lambda b, h: (0, 0))
    kr, kmean, vt = pl.pallas_call(
        functools.partial(_moba_kv_kernel, nb=nb, rope_half=half),
        grid=(batch, H),
        in_specs=[pl.BlockSpec((seq, hd), lambda b, h: (b, k_col + h)),
                  pl.BlockSpec((seq, hd), lambda b, h: (b, v_col + h)), full, full, full],
        out_specs=[pl.BlockSpec((seq, hd), lambda b, h: (b, h)),
                   pl.BlockSpec((1, 1, nb, hd), lambda b, h: (b, h, 0, 0)),
                   pl.BlockSpec((1, 1, nb, hd, BLK), lambda b, h: (b, h, 0, 0, 0))],
        out_shape=[jax.ShapeDtypeStruct((T, H * hd), BF16), jax.ShapeDtypeStruct((batch, H, nb, hd), F32),
                   jax.ShapeDtypeStruct((batch, H, nb, hd, BLK), BF16)],
        name="moba_kv",
        compiler_params=_params("parallel", "parallel"),
    )(pm, pm, cos_t, sin_lo, sin_hi)
    tab = pl.BlockSpec((BLK, hd), lambda b, h, i: (i, 0))
    NH = MOBA_HEADS_PER_STEP
    assert H % NH == 0 and q_col % NH == 0
    return pl.pallas_call(
        functools.partial(_moba_attn_kernel, nb=nb, rope_half=half),
        grid=(batch, H // NH, nb),
        in_specs=[pl.BlockSpec((BLK, NH * hd), lambda b, h, i: (b * nb + i, q_col // NH + h)), tab, tab, tab,
                  pl.BlockSpec((seq, NH * hd), lambda b, h, i: (b, h)),
                  pl.BlockSpec((1, NH, nb, hd, BLK), lambda b, h, i: (b, h, 0, 0, 0)),
                  pl.BlockSpec((1, NH, nb, hd), lambda b, h, i: (b, h, 0, 0))],
        out_specs=pl.BlockSpec((BLK, NH * hd), lambda b, h, i: (b * nb + i, h)),
        out_shape=jax.ShapeDtypeStruct((T, H * hd), BF16),
        scratch_shapes=[pltpu.VMEM((NH, nb, BLK), F32), pltpu.VMEM((NH, hd, BLK), F32)],
        name="moba_attn",
        compiler_params=_params("parallel", "parallel", "arbitrary"),
    )(pm, cos_t, sin_lo, sin_hi, kr, vt, kmean)


def _gelu_erf(x):
    return 0.5 * x * (1.0 + lax.erf(x * (2.0 ** -0.5)))


def _top_values(xs, count):
    vals = [[] for _ in xs]
    xs = list(xs)
    for _ in range(count):
        for j, x in enumerate(xs):
            m = jnp.max(x, axis=0, keepdims=True)
            vals[j].append(m)
            xs[j] = jnp.where(x >= m, -jnp.inf, x)
    return vals


def _peer_topk_kernel(q_ref, sk_ref, thr_ref, e1_ref, s2_ref, e2_ref):
    nt = (((1,), (1,)), ((), ()))
    K = PEER_TOPK
    q = q_ref[...]
    half = q.shape[-1] // 2
    s1 = lax.dot_general(sk_ref[0, 0], q[:, :half], nt, precision=lax.Precision.HIGHEST, preferred_element_type=F32)
    s2 = lax.dot_general(sk_ref[0, 1], q[:, half:], nt, precision=lax.Precision.HIGHEST, preferred_element_type=F32)
    v1, v2 = _top_values([s1, s2], K)
    v2_all = jnp.concatenate(v2, axis=0)
    cand = jnp.concatenate([v1[0] + v2_all] + [v1[a] + v2_all[:K // 2] for a in range(1, K)], axis=0)
    tau = _top_values([cand], K)[0][-1]
    top = v1[0] + v2[0]
    z = jnp.sum(jnp.where(cand >= tau, jnp.exp(cand - top), 0.0), axis=0, keepdims=True)
    thr = jnp.full_like(s1, jnp.inf)
    for b in range(K):
        thr = jnp.where(s1 + v2[b] >= tau, v2[b], thr)
    thr_ref[0] = jnp.where(s1 >= v1[K - 1], thr, jnp.inf)
    e1_ref[0] = jnp.exp(s1 - v1[0])
    s2_ref[0] = s2
    e2_ref[0] = jnp.exp(s2 - v2[0]) / z


def peer_topk(q, subkeys, tt=256):
    T = q.shape[0]
    H, _, NK, dh = subkeys.shape
    assert NK == PEER_NKEYS
    spec = pl.BlockSpec((1, NK, tt), lambda i, h: (h, 0, i))
    shape = jax.ShapeDtypeStruct((H, NK, T), F32)
    return pl.pallas_call(
        _peer_topk_kernel,
        grid=(T // tt, H),
        in_specs=[pl.BlockSpec((tt, 2 * dh), lambda i, h: (i, h)),
                  pl.BlockSpec((1, 2, NK, dh), lambda i, h: (h, 0, 0, 0))],
        out_specs=[spec] * 4,
        out_shape=[shape] * 4,
        name="peer_topk",
        compiler_params=_params("parallel", "parallel"),
    )(q, subkeys)


def _peer_gate_kernel(ht_ref, u_ref, scale_ref, wscale_ref, thr_ref, e1_ref, s2_ref, e2_ref, w_ref, act_ref, g_ref,
                      *, rows_per_tile):
    NK = PEER_NKEYS
    nchunks = ht_ref.shape[1] // LANES
    for s in range(rows_per_tile // PEER_SLICE_ROWS):
        rows = slice(s * PEER_SLICE_ROWS * NK, (s + 1) * PEER_SLICE_ROWS * NK)
        act_ref[rows, :] = jnp.dot(u_ref[rows, :], ht_ref[...], preferred_element_type=F32)
        for k1 in range(s * PEER_SLICE_ROWS, (s + 1) * PEER_SLICE_ROWS):
            for c in range(nchunks):
                cols = slice(c * LANES, (c + 1) * LANES)
                g = None
                for h in range(PEER_HEADS):
                    thr = thr_ref[h, k1:k1 + 1, cols]
                    e1 = e1_ref[h, k1:k1 + 1, cols]
                    term = jnp.where(s2_ref[h, :, cols] >= thr, e2_ref[h, :, cols], 0.0) * e1
                    g = term if g is None else g + term
                g_ref[k1 * NK:(k1 + 1) * NK, cols] = g
    for k1 in range(rows_per_tile):
        erows = slice(k1 * NK, (k1 + 1) * NK)
        for c in range(nchunks):
            cols = slice(c * LANES, (c + 1) * LANES)
            w = g_ref[erows, cols] * _gelu_erf(act_ref[erows, cols] * scale_ref[:, cols]) * wscale_ref[:, cols]
            w_ref[cols, erows] = w.T.astype(w_ref.dtype)


def peer_gate(ht, u, layer, scale, wscale, tables, tt=PEER_TOKEN_TILE, te=PEER_EXPERT_TILE):
    D, T = ht.shape
    E = u.shape[1]
    H, NK, _ = tables[0].shape
    tt = min(tt, T)
    assert T % tt == 0 and E % te == 0
    rows_per_tile = te // NK
    assert rows_per_tile % SUBLANES == 0
    k1spec = pl.BlockSpec((H, rows_per_tile, tt), lambda i, j: (0, j, i))
    k2spec = pl.BlockSpec((H, NK, tt), lambda i, j: (0, 0, i))
    return pl.pallas_call(
        functools.partial(_peer_gate_kernel, rows_per_tile=rows_per_tile),
        grid=(T // tt, E // te),
        in_specs=[pl.BlockSpec((D, tt), lambda i, j: (0, i)),
                  pl.BlockSpec((None, te, D), lambda i, j: (layer, j, 0)),
                  pl.BlockSpec((1, tt), lambda i, j: (0, i)),
                  pl.BlockSpec((1, tt), lambda i, j: (0, i)),
                  k1spec, k1spec, k2spec, k2spec],
        out_specs=pl.BlockSpec((tt, te), lambda i, j: (i, j)),
        out_shape=jax.ShapeDtypeStruct((T, E), FP8),
        scratch_shapes=[pltpu.VMEM((te, tt), F32), pltpu.VMEM((te, tt), F32)],
        name="peer_gate",
        compiler_params=_params("parallel", "arbitrary"),
    )(ht, u, scale, wscale, *tables)


def quantise_table(w):
    amax = jnp.maximum(jnp.max(jnp.abs(w), axis=(1, 2)), FP8_TINY)
    row_norm = jnp.sqrt(jnp.max(jnp.sum(w * w, axis=2), axis=1))
    return (w * (FP8_TARGET_MAX / amax)[:, None, None]).astype(FP8), amax * (1.0 / FP8_TARGET_MAX), row_norm


def peer_ffn(x, layer, norm_g, wq, subkeys, u_quant, v_quant):
    u_fp8, u_scale, u_norm = u_quant
    v_fp8, v_scale, _ = v_quant
    h, ht, t_scale, h_norm = rmsnorm(x, norm_g, BF16, with_fp8_transpose=True)
    q = matmul(h, wq, layer=layer, name="peer_query")
    tables = peer_topk(q, subkeys)
    bound = jnp.maximum(PEER_HEADS * h_norm * u_norm[layer], FP8_TINY)
    w = peer_gate(ht, u_fp8, layer, t_scale * u_scale[layer], FP8_TARGET_MAX / bound, tables)
    undo = (bound * (v_scale[layer] / FP8_TARGET_MAX)).reshape(-1, 1)
    return matmul(w, v_fp8, res=x, row_scale=undo, layer=layer, name="peer_out")


def _in_proj_layout(group_width):
    gw = group_width
    parts = ((('lru_x', 'lru_gate', 'gla_q', 'gla_k', 'gla_v'), (gw, gw, gw // 2, gw // 2, gw)),
             (('gla_og', 's5_u', 'moba_q', 'moba_k', 'moba_v'), (gw,) * 5))
    cols = {}
    for names, sizes in parts:
        c = 0
        for name, size in zip(names, sizes):
            cols[name] = c // LANES
            c += size
    return sum(parts[0][1]), cols


def kernel(x, norm1_g, w_in, lru_conv_w, lru_conv_b, lru_wa, lru_ba, lru_wx, lru_bx, lru_lambda, gla_wg2, gla_bg, gla_norm_g, s5_a_re, s5_a_im, s5_log_step, s5_b_re, s5_b_im, s5_c_re, s5_c_im, s5_d, s5_w_glu, s5_b_glu, w_out, norm2_g, peer_wq, peer_subkeys, peer_u, peer_v, final_norm_g):
    B, S, D = x.shape
    T = B * S
    depth = norm1_g.shape[0]
    gw = D // N_MIXERS
    x = x.reshape(T, D)
    lr0, col = _in_proj_layout(gw)
    w_in_t = jnp.transpose(w_in, (0, 2, 1)).astype(BF16)
    w_b = w_in_t[:, lr0 + GLA_GATE_RANK:]
    w_lr = jnp.zeros((depth, LANES, D), BF16).at[:, :GLA_GATE_RANK].set(w_in_t[:, lr0:lr0 + GLA_GATE_RANK])
    w_out_bf16 = w_out.astype(BF16)
    peer_wq_bf16 = peer_wq.astype(BF16)
    peer_u_quant = quantise_table(peer_u)
    peer_v_quant = quantise_table(peer_v)
    for l in range(depth):
        h = rmsnorm(x, norm1_g[l], BF16)
        pa = matmul(h, w_in_t, layer=l, n_out=lr0, w_transposed=True, name="in_proj_a")
        pb = matmul(h, w_b, layer=l, w_transposed=True, name="in_proj_b")
        plr = matmul(h, w_lr, layer=l, w_transposed=True, name="in_proj_lr")
        y_a = rglru(pa, B, S, col['lru_x'], col['lru_gate'], lru_conv_w[l], lru_conv_b[l], lru_wa[l], lru_ba[l],
                    lru_wx[l], lru_bx[l], lru_lambda[l])
        y_b = gla(pa, pb, plr, B, S, col['gla_q'], col['gla_k'], col['gla_v'], col['gla_og'],
                  gla_wg2[l], gla_bg[l], gla_norm_g[l])
        prep = s5_prepare(s5_a_re[l], s5_a_im[l], s5_log_step[l], s5_b_re[l], s5_b_im[l], s5_c_re[l], s5_c_im[l])
        z = s5_ssm(pb, B, S, col['s5_u'], prep, s5_d[l])
        y_c = glu(z, s5_w_glu[l], s5_b_glu[l])
        y_d = moba(pb, B, S, col['moba_q'], col['moba_k'], col['moba_v'])
        mixed = jnp.concatenate([y_a, y_b, y_c, y_d], axis=-1)
        x = matmul(mixed, w_out_bf16, res=x, layer=l, name="out_proj")
        x = peer_ffn(x, l, norm2_g[l], peer_wq_bf16, peer_subkeys[l], peer_u_quant, peer_v_quant)
    return rmsnorm(x, final_norm_g, F32).reshape(B, S, D)
```

```python
import functools
import math

import jax
import jax.numpy as jnp
from jax import lax
from jax.experimental import pallas as pl
from jax.experimental.pallas import tpu as pltpu

F32 = jnp.float32
BF16 = jnp.bfloat16
FP8 = jnp.float8_e4m3fn
FP8_TARGET_MAX = 256.0
FP8_TINY = 1e-30

LANES = 128
SUBLANES = 8
VMEM_LIMIT_BYTES = 56 * 2**20

RMS_EPS = 1e-6
N_MIXERS = 4

LRU_BLOCKS = 8
LRU_CONV = 4
LRU_C = 8.0
LRU_ROWS = 256

GLA_HEADS = 4
GLA_GATE_RANK = 16
GLA_TAU = 16.0
GLA_CHUNK = 64
GLA_TILE = 512
GLA_HEADS_PER_STEP = 2

S5_GROUP = 16
S5_STATE = 64
S5_SLAB_GROUPS = LANES // S5_GROUP
S5_SLAB_STATES = S5_SLAB_GROUPS * S5_STATE
S5_TILE = 256

MOBA_HEADS = 8
MOBA_BLOCK = 256
MOBA_TOPK = 3
MOBA_GROUP = 4
MOBA_HEADS_PER_STEP = 8
ROPE_THETA = 500000.0
MASK_VALUE = -1e30

PEER_HEADS = 8
PEER_NKEYS = 128
PEER_TOPK = 16
PEER_TOKEN_TILE = 1024
PEER_EXPERT_TILE = 1024
PEER_SLICE_ROWS = 2


def _params(*semantics):
    return pltpu.CompilerParams(dimension_semantics=semantics, vmem_limit_bytes=VMEM_LIMIT_BYTES)


def _rmsnorm_kernel(x_ref, g_ref, o_ref, *maybe_quantised):
    x = x_ref[...]
    y = x * lax.rsqrt(jnp.mean(x * x, axis=-1, keepdims=True) + RMS_EPS) * g_ref[...]
    o_ref[...] = y.astype(o_ref.dtype)
    if maybe_quantised:
        qt_ref, scale_ref, norm_ref = maybe_quantised
        yt = y.T
        amax = jnp.maximum(jnp.max(jnp.abs(yt), axis=0, keepdims=True), FP8_TINY)
        qt_ref[...] = (yt * (FP8_TARGET_MAX / amax)).astype(qt_ref.dtype)
        scale_ref[...] = amax * (1.0 / FP8_TARGET_MAX)
        norm_ref[...] = jnp.sqrt(jnp.sum(yt * yt, axis=0, keepdims=True))


def rmsnorm(x, g, out_dtype, with_fp8_transpose=False, tm=256):
    T, D = x.shape
    out_shape = [jax.ShapeDtypeStruct((T, D), out_dtype)]
    out_specs = [pl.BlockSpec((tm, D), lambda i: (i, 0))]
    with_transpose = with_fp8_transpose
    if with_transpose:
        row = pl.BlockSpec((1, tm), lambda i: (0, i))
        out_shape += [jax.ShapeDtypeStruct((D, T), FP8)] + [jax.ShapeDtypeStruct((1, T), F32)] * 2
        out_specs += [pl.BlockSpec((D, tm), lambda i: (0, i)), row, row]
    res = pl.pallas_call(
        _rmsnorm_kernel,
        grid=(T // tm,),
        in_specs=[pl.BlockSpec((tm, D), lambda i: (i, 0)), pl.BlockSpec((1, D), lambda i: (0, 0))],
        out_specs=out_specs,
        out_shape=out_shape,
        name="rmsnorm",
        compiler_params=_params("parallel"),
    )(x, g.reshape(1, D))
    return res if with_transpose else res[0]


def _mm_kernel(x_ref, w_ref, *rest, nk, has_res, has_scale, w_transposed):
    rest = list(rest)
    s_ref = rest.pop(0) if has_scale else None
    r_ref = rest.pop(0) if has_res else None
    o_ref = rest.pop(0)
    contract = (((1,), (1 if w_transposed else 0,)), ((), ()))
    part = lax.dot_general(x_ref[...], w_ref[...], contract, preferred_element_type=F32)

    def finish(acc):
        if has_scale:
            acc = acc * s_ref[...]
        if has_res:
            acc = acc + r_ref[...]
        o_ref[...] = acc.astype(o_ref.dtype)

    if nk == 1:
        finish(part)
        return
    acc_ref = rest.pop(0)
    k = pl.program_id(2)

    @pl.when(k == 0)
    def _():
        acc_ref[...] = part

    @pl.when(k > 0)
    def _():
        acc_ref[...] += part

    @pl.when(k == nk - 1)
    def _():
        finish(acc_ref[...])


def matmul(x, w, res=None, *, row_scale=None, layer=None, n_out=None, w_transposed=False, tm=1024, tn=1024,
           tk=4096, out_dtype=F32, name="matmul"):
    M, K = x.shape
    k_axis, n_axis = (-1, -2) if w_transposed else (-2, -1)
    N = w.shape[n_axis] if n_out is None else n_out
    tm, tn, tk = min(tm, M), min(tn, N), min(tk, K)
    assert M % tm == 0 and N % tn == 0 and K % tk == 0 and w.shape[k_axis] == K
    nk = K // tk
    wblock = (tn, tk) if w_transposed else (tk, tn)
    windex = (lambda i, j, k: (j, k)) if w_transposed else (lambda i, j, k: (k, j))
    if layer is None:
        wspec = pl.BlockSpec(wblock, windex)
    else:
        wspec = pl.BlockSpec((None,) + wblock, lambda i, j, k: (layer,) + windex(i, j, k))
    in_specs = [pl.BlockSpec((tm, tk), lambda i, j, k: (i, k)), wspec]
    args = [x, w]
    if row_scale is not None:
        in_specs.append(pl.BlockSpec((tm, 1), lambda i, j, k: (i, 0)))
        args.append(row_scale)
    if res is not None:
        in_specs.append(pl.BlockSpec((tm, tn), lambda i, j, k: (i, j)))
        args.append(res)
    return pl.pallas_call(
        functools.partial(_mm_kernel, nk=nk, has_res=res is not None, has_scale=row_scale is not None,
                          w_transposed=w_transposed),
        grid=(M // tm, N // tn, nk),
        in_specs=in_specs,
        out_specs=pl.BlockSpec((tm, tn), lambda i, j, k: (i, j)),
        out_shape=jax.ShapeDtypeStruct((M, N), out_dtype),
        scratch_shapes=[pltpu.VMEM((tm, tn), F32)] if nk > 1 else [],
        name=name,
        compiler_params=_params("parallel", "parallel", "arbitrary"),
    )(*args)


def _lru_kernel(x_ref, gate_ref, cw_ref, cb_ref, wa_ref, ba_ref, wx_ref, bx_ref, lam_ref, o_ref, *, seq):
    R = LRU_ROWS
    cw = cw_ref[...]
    cb = cb_ref[...]
    ba = ba_ref[...]
    bx = bx_ref[...]
    neg_c_softplus = -LRU_C * jax.nn.softplus(-lam_ref[...])
    wa = wa_ref[0]
    wx = wx_ref[0]
    row = lax.broadcasted_iota(jnp.int32, (R, LANES), 0)
    row8 = lax.broadcasted_iota(jnp.int32, (SUBLANES, LANES), 0)

    def body(c, h):
        r0 = pl.multiple_of(c * R, R)
        xt = x_ref[pl.ds(r0, R), :]
        p0 = pl.multiple_of(jnp.maximum(r0 - SUBLANES, 0), SUBLANES)
        prev = jnp.where(c > 0, x_ref[pl.ds(p0, SUBLANES), :], 0.0)
        xc = xt * cw[LRU_CONV - 1:LRU_CONV] + cb
        for d in range(1, LRU_CONV):
            rolled = pltpu.roll(xt, d, 0)
            head = jnp.where(row8 < d, pltpu.roll(prev, d, 0), rolled[:SUBLANES])
            shifted = jnp.concatenate([head, rolled[SUBLANES:]], axis=0)
            xc = xc + shifted * cw[LRU_CONV - 1 - d:LRU_CONV - d]
        xb = xc.astype(BF16)
        r = jax.nn.sigmoid(jnp.dot(xb, wa, preferred_element_type=F32) + ba)
        i = jax.nn.sigmoid(jnp.dot(xb, wx, preferred_element_type=F32) + bx)
        log_a = r * neg_c_softplus
        a = jnp.exp(log_a)
        b = jnp.sqrt(1.0 - a * a) * (i * xc)
        d = 1
        while d < R:
            keep = row >= d
            a_sh = jnp.where(keep, pltpu.roll(a, d, 0), 1.0)
            b_sh = jnp.where(keep, pltpu.roll(b, d, 0), 0.0)
            b = a * b_sh + b
            a = a * a_sh
            d *= 2
        hs = b + a * h
        o_ref[pl.ds(r0, R), :] = (hs * jax.nn.gelu(gate_ref[pl.ds(r0, R), :])).astype(o_ref.dtype)
        return hs[R - 1:R, :]

    lax.fori_loop(0, seq // R, body, jnp.zeros((1, LANES), F32))


def rglru(pm, batch, seq, x_col, gate_col, conv_w, conv_b, wa, ba, wx, bx, lam):
    T = batch * seq
    W = LRU_BLOCKS * LANES
    vec = lambda v: v.reshape(1, W)
    vspec = pl.BlockSpec((1, LANES), lambda b, n: (0, n))
    wspec = pl.BlockSpec((1, LANES, LANES), lambda b, n: (n, 0, 0))
    return pl.pallas_call(
        functools.partial(_lru_kernel, seq=seq),
        grid=(batch, LRU_BLOCKS),
        in_specs=[pl.BlockSpec((seq, LANES), lambda b, n: (b, x_col + n)),
                  pl.BlockSpec((seq, LANES), lambda b, n: (b, gate_col + n)),
                  pl.BlockSpec((LRU_CONV, LANES), lambda b, n: (0, n)),
                  vspec, wspec, vspec, wspec, vspec, vspec],
        out_specs=pl.BlockSpec((seq, LANES), lambda b, n: (b, n)),
        out_shape=jax.ShapeDtypeStruct((T, W), BF16),
        name="rglru",
        compiler_params=_params("parallel", "parallel"),
    )(pm, pm, conv_w, vec(conv_b), wa.astype(BF16), vec(ba), wx.astype(BF16), vec(bx), vec(lam))


def _gla_kernel(q_ref, k_ref, v_ref, og_ref, lr_ref, wg_ref, bg_ref, ng_ref, o_ref, st_ref):
    C = GLA_CHUNK
    NH = GLA_HEADS_PER_STEP
    dk = q_ref.shape[-1] // NH
    dv = v_ref.shape[-1] // NH

    @pl.when(pl.program_id(2) == 0)
    def _():
        st_ref[...] = jnp.zeros_like(st_ref)

    TT = GLA_TILE
    NC = TT // C
    ri = lax.broadcasted_iota(jnp.int32, (TT, TT), 0)
    ci = lax.broadcasted_iota(jnp.int32, (TT, TT), 1)
    causal = ((ri // C) == (ci // C)) & (ri >= ci)
    tril = jnp.where(causal, 1.0, 0.0).astype(BF16)
    scale = dk ** -0.5
    nt = (((1,), (1,)), ((), ()))
    tn = (((0,), (0,)), ((), ()))
    lr = lr_ref[...]

    for hh in range(NH):
        kcols = slice(hh * dk, (hh + 1) * dk)
        vcols = slice(hh * dv, (hh + 1) * dv)
        q = q_ref[:, kcols]
        k = k_ref[:, kcols]
        v = v_ref[:, vcols].astype(BF16)
        pre = jnp.dot(lr, wg_ref[hh], precision=lax.Precision.HIGHEST, preferred_element_type=F32) + bg_ref[hh]
        g = jax.nn.log_sigmoid(pre) / GLA_TAU
        g_hi = g.astype(BF16)
        g_lo = (g - g_hi.astype(F32)).astype(BF16)
        bc = (jnp.dot(tril, g_hi, preferred_element_type=F32) + jnp.dot(tril, g_lo, preferred_element_type=F32))
        b_last = jnp.broadcast_to(bc.reshape(NC, C, dk)[:, C - 1:C, :], (NC, C, dk)).reshape(TT, dk)
        qe = (q * scale * jnp.exp(bc)).astype(BF16)
        ke = (k * jnp.exp(-bc)).astype(BF16)
        kd = (k * jnp.exp(b_last - bc)).astype(BF16)
        att = lax.dot_general(qe, ke, nt, preferred_element_type=F32)
        att = jnp.where(causal, att, 0.0).astype(BF16)
        o = jnp.dot(att, v, preferred_element_type=F32)
        st = st_ref[hh]
        carried = []
        for c in range(NC):
            rows = slice(c * C, (c + 1) * C)
            carried.append(lax.dot_general(qe[rows], st.astype(BF16), nt, preferred_element_type=F32))
            decay = jnp.exp(b_last[c * C:c * C + 1, :])
            st = st * decay + lax.dot_general(v[rows], kd[rows], tn, preferred_element_type=F32)
        st_ref[hh] = st
        o = o + jnp.concatenate(carried, axis=0)
        o = o * lax.rsqrt(jnp.mean(o * o, axis=-1, keepdims=True) + RMS_EPS)
        o_ref[:, vcols] = (o * ng_ref[hh] * jax.nn.silu(og_ref[:, vcols])).astype(o_ref.dtype)


def gla(pm, pm_og, plr, batch, seq, q_col, k_col, v_col, og_col, wg2, bg, norm_g):
    T = batch * seq
    H = GLA_HEADS
    dk = wg2.shape[-1] // H
    dv = norm_g.shape[-1] // H
    assert dk == LANES and dv == 2 * LANES
    nt = seq // GLA_TILE
    wg = jnp.zeros((LANES, H * dk), F32).at[:GLA_GATE_RANK].set(wg2)
    wg = wg.reshape(LANES, H, dk).transpose(1, 0, 2)
    tok = lambda b, h, t: b * nt + t
    NH = GLA_HEADS_PER_STEP
    kb, vb = NH * dk // LANES, NH * dv // LANES
    assert H % NH == 0 and q_col % kb == 0 and k_col % kb == 0 and v_col % vb == 0 and og_col % vb == 0
    return pl.pallas_call(
        _gla_kernel,
        grid=(batch, H // NH, nt),
        in_specs=[pl.BlockSpec((GLA_TILE, NH * dk), lambda b, h, t: (tok(b, h, t), q_col // kb + h)),
                  pl.BlockSpec((GLA_TILE, NH * dk), lambda b, h, t: (tok(b, h, t), k_col // kb + h)),
                  pl.BlockSpec((GLA_TILE, NH * dv), lambda b, h, t: (tok(b, h, t), v_col // vb + h)),
                  pl.BlockSpec((GLA_TILE, NH * dv), lambda b, h, t: (tok(b, h, t), og_col // vb + h)),
                  pl.BlockSpec((GLA_TILE, LANES), lambda b, h, t: (tok(b, h, t), 0)),
                  pl.BlockSpec((NH, LANES, dk), lambda b, h, t: (h, 0, 0)),
                  pl.BlockSpec((NH, 1, dk), lambda b, h, t: (h, 0, 0)),
                  pl.BlockSpec((NH, 1, dv), lambda b, h, t: (h, 0, 0))],
        out_specs=pl.BlockSpec((GLA_TILE, NH * dv), lambda b, h, t: (tok(b, h, t), h)),
        out_shape=jax.ShapeDtypeStruct((T, H * dv), BF16),
        scratch_shapes=[pltpu.VMEM((NH, dv, dk), F32)],
        name="gla",
        compiler_params=_params("parallel", "parallel", "arbitrary"),
    )(pm, pm, pm, pm_og, plr, wg, bg.reshape(H, 1, dk), norm_g.reshape(H, 1, dv))


def _s5_kernel(u_ref, bcat_ref, ccat_ref, pr_ref, pi_ref, d_ref, z_ref, x_ref, carry_ref):
    TT = S5_TILE
    NS = 2 * S5_SLAB_STATES

    @pl.when(pl.program_id(2) == 0)
    def _():
        carry_ref[...] = jnp.zeros_like(carry_ref)

    u = u_ref[...]
    pr = pr_ref[0]
    pi = pi_ref[0]

    def swap(t):
        return pltpu.roll(t, S5_SLAB_STATES, 1)

    sub = lax.broadcasted_iota(jnp.int32, u.shape, 0) % SUBLANES
    lagged = [u.astype(BF16)]
    for d in range(1, SUBLANES):
        lagged.append(jnp.where(sub >= d, pltpu.roll(u, d, 0), 0.0).astype(BF16))
    x_ref[...] = jnp.dot(jnp.concatenate(lagged, axis=1), bcat_ref[0], preferred_element_type=F32)

    def carry_step(i, carry):
        rows = pl.ds(pl.multiple_of(i * SUBLANES, SUBLANES), SUBLANES)
        cb = jnp.broadcast_to(carry, (SUBLANES, NS))
        blk = x_ref[rows, :] + pr * cb + pi * swap(cb)
        x_ref[rows, :] = blk
        return blk[SUBLANES - 1:SUBLANES, :]

    carry_ref[...] = lax.fori_loop(0, TT // SUBLANES, carry_step, carry_ref[...], unroll=4)
    y = jnp.dot(x_ref[...].astype(BF16), ccat_ref[0], preferred_element_type=F32) + d_ref[...] * u
    z_ref[...] = jax.nn.gelu(y)


def s5_prepare(a_re, a_im, log_step, b_re, b_im, c_re, c_im):
    G, P = a_re.shape
    H = S5_GROUP
    ns = G // S5_SLAB_GROUPS
    step = jnp.exp(log_step)[:, None]
    mag = jnp.exp(a_re * step)
    ang = a_im * step
    abar_re, abar_im = mag * jnp.cos(ang), mag * jnp.sin(ang)
    den = a_re * a_re + a_im * a_im
    f_re = ((abar_re - 1.0) * a_re + abar_im * a_im) / den
    f_im = (abar_im * a_re - (abar_re - 1.0) * a_im) / den
    bb_re = f_re[..., None] * b_re - f_im[..., None] * b_im
    bb_im = f_re[..., None] * b_im + f_im[..., None] * b_re
    eye = jnp.eye(S5_SLAB_GROUPS, dtype=F32)

    def in_slab(bb):
        t = bb.reshape(ns, S5_SLAB_GROUPS, P, H)
        return jnp.einsum('sgph,gk->sghkp', t, eye).reshape(ns, LANES, S5_SLAB_STATES)

    def out_slab(cc):
        t = cc.reshape(ns, S5_SLAB_GROUPS, H, P)
        return jnp.einsum('sghp,gk->sgpkh', t, eye).reshape(ns, S5_SLAB_STATES, LANES)

    ccat = jnp.concatenate([out_slab(c_re), -out_slab(c_im)], axis=1).astype(BF16)
    pw_re, pw_im = [abar_re], [abar_im]
    for _ in range(SUBLANES - 1):
        r, i = pw_re[-1], pw_im[-1]
        pw_re.append(r * abar_re - i * abar_im)
        pw_im.append(r * abar_im + i * abar_re)
    slab = lambda t: jnp.stack(t, 0).reshape(SUBLANES, ns, S5_SLAB_STATES).transpose(1, 0, 2)
    pr, pi = slab(pw_re), slab(pw_im)
    bre, bim = in_slab(bb_re), in_slab(bb_im)
    lag_re = [bre] + [bre * pr[:, j:j + 1] - bim * pi[:, j:j + 1] for j in range(SUBLANES - 1)]
    lag_im = [bim] + [bre * pi[:, j:j + 1] + bim * pr[:, j:j + 1] for j in range(SUBLANES - 1)]
    bcat = jnp.concatenate([jnp.concatenate(lag_re, axis=1), jnp.concatenate(lag_im, axis=1)], axis=-1).astype(BF16)
    return bcat, ccat, jnp.concatenate([pr, pr], -1), jnp.concatenate([-pi, pi], -1)


def s5_ssm(pm, batch, seq, u_col, prep, d):
    T = batch * seq
    bcat, ccat, pr, pi = prep
    ns = bcat.shape[0]
    NS = 2 * S5_SLAB_STATES
    nt = seq // S5_TILE
    return pl.pallas_call(
        _s5_kernel,
        grid=(batch, ns, nt),
        in_specs=[pl.BlockSpec((S5_TILE, LANES), lambda b, s, t: (b * nt + t, u_col + s)),
                  pl.BlockSpec((1, SUBLANES * LANES, NS), lambda b, s, t: (s, 0, 0)),
                  pl.BlockSpec((1, NS, LANES), lambda b, s, t: (s, 0, 0)),
                  pl.BlockSpec((1, SUBLANES, NS), lambda b, s, t: (s, 0, 0)),
                  pl.BlockSpec((1, SUBLANES, NS), lambda b, s, t: (s, 0, 0)),
                  pl.BlockSpec((1, LANES), lambda b, s, t: (0, s))],
        out_specs=pl.BlockSpec((S5_TILE, LANES), lambda b, s, t: (b * nt + t, s)),
        out_shape=jax.ShapeDtypeStruct((T, ns * LANES), F32),
        scratch_shapes=[pltpu.VMEM((S5_TILE, NS), F32), pltpu.VMEM((1, NS), F32)],
        name="s5_ssm",
        compiler_params=_params("parallel", "parallel", "arbitrary"),
    )(pm, bcat, ccat, pr, pi, d.reshape(1, -1))


def _glu_kernel(z_ref, w_ref, b_ref, o_ref):
    z = z_ref[...]
    y = jnp.dot(z.astype(BF16), w_ref[...], preferred_element_type=F32) + b_ref[...]
    o_ref[...] = (z * jax.nn.sigmoid(y)).astype(o_ref.dtype)


def glu(z, w, b, tm=512):
    T, W = z.shape
    return pl.pallas_call(
        _glu_kernel,
        grid=(T // tm,),
        in_specs=[pl.BlockSpec((tm, W), lambda i: (i, 0)),
                  pl.BlockSpec((W, W), lambda i: (0, 0)),
                  pl.BlockSpec((1, W), lambda i: (0, 0))],
        out_specs=pl.BlockSpec((tm, W), lambda i: (i, 0)),
        out_shape=jax.ShapeDtypeStruct((T, W), BF16),
        name="s5_glu",
        compiler_params=_params("parallel"),
    )(z, w.astype(BF16), b.reshape(1, W))


def _rope(t, cos, sin_lo, sin_hi, rope_half):
    return (t * cos + pltpu.roll(t, LANES - rope_half, 1) * sin_lo + pltpu.roll(t, rope_half, 1) * sin_hi)


def rope_tables(seq, head_dim):
    rope_dims = head_dim // 4
    half = rope_dims // 2
    inv = jnp.power(ROPE_THETA, -jnp.arange(half, dtype=F32) / half)
    ang = jnp.arange(seq).astype(F32)[:, None] * inv[None, :]
    cos, sin = jnp.cos(ang), jnp.sin(ang)
    pad = head_dim - rope_dims
    zeros = jnp.zeros((seq, half), F32)
    cos_t = jnp.concatenate([cos, cos, jnp.ones((seq, pad), F32)], axis=-1)
    sin_lo = jnp.concatenate([-sin, zeros, jnp.zeros((seq, pad), F32)], axis=-1)
    sin_hi = jnp.concatenate([zeros, sin, jnp.zeros((seq, pad), F32)], axis=-1)
    return cos_t, sin_lo, sin_hi, half


def _moba_kv_kernel(k_ref, v_ref, cos_ref, slo_ref, shi_ref, kr_ref, km_ref, vt_ref, *, nb, rope_half):
    for n in range(nb):
        rows = pl.ds(n * MOBA_BLOCK, MOBA_BLOCK)
        kr = _rope(k_ref[rows, :], cos_ref[rows, :], slo_ref[rows, :], shi_ref[rows, :], rope_half)
        kr_ref[rows, :] = kr.astype(kr_ref.dtype)
        km_ref[0, 0, n:n + 1, :] = jnp.mean(kr, axis=0, keepdims=True)
        vt_ref[0, 0, n] = v_ref[rows, :].T.astype(vt_ref.dtype)


def _moba_attn_kernel(q_ref, cos_ref, slo_ref, shi_ref, kr_ref, vt_ref, km_ref, o_ref, sel_ref, acc_ref,
                      *, nb, rope_half):
    BLK = MOBA_BLOCK
    NH = MOBA_HEADS_PER_STEP
    i = pl.program_id(2)
    hd = q_ref.shape[-1] // NH
    log2_scale = hd ** -0.5 * math.log2(math.e)
    blk_id = lax.broadcasted_iota(jnp.int32, (nb, BLK), 0)
    key = lax.broadcasted_iota(jnp.int32, (BLK, BLK), 0)
    qry = lax.broadcasted_iota(jnp.int32, (BLK, BLK), 1)
    own = pl.ds(pl.multiple_of(i * BLK, BLK), BLK)
    cos, slo, shi = cos_ref[...], slo_ref[...], shi_ref[...]
    lanes = [slice(hh * hd, (hh + 1) * hd) for hh in range(NH)]

    qbs, stats = [], []
    for hh in range(NH):
        qt = _rope(q_ref[:, lanes[hh]], cos, slo, shi, rope_half).T
        gate = jnp.dot(km_ref[0, hh], qt, precision=lax.Precision.HIGHEST, preferred_element_type=F32)
        rank = jnp.zeros((nb, BLK), jnp.int32)
        for m in range(nb):
            gm = gate[m:m + 1, :]
            beats = (gm > gate) | ((gm == gate) & (m < blk_id))
            rank = rank + jnp.where(beats & (m < i), 1, 0)
        sel_ref[hh] = ((blk_id < i) & (rank < MOBA_TOPK)).astype(F32)
        qb = (qt * log2_scale).astype(BF16)
        s = jnp.dot(kr_ref[own, lanes[hh]], qb, preferred_element_type=F32)
        s = jnp.where(key <= qry, s, MASK_VALUE)
        m0 = jnp.max(s, axis=0, keepdims=True)
        p = jnp.exp2(s - m0)
        acc_ref[hh] = jnp.dot(vt_ref[0, hh, i], p.astype(BF16), preferred_element_type=F32)
        qbs.append(qb)
        stats += [m0, jnp.sum(p, axis=0, keepdims=True)]

    def body(g, carry):
        out = []
        for hh in range(NH):
            m_prev, l_prev = carry[2 * hh], carry[2 * hh + 1]
            scores = []
            for b in range(MOBA_GROUP):
                n = g * MOBA_GROUP + b
                rows = pl.ds(pl.multiple_of(n * BLK, BLK), BLK)
                sn = jnp.dot(kr_ref[rows, lanes[hh]], qbs[hh], preferred_element_type=F32)
                scores.append(jnp.where(sel_ref[hh, pl.ds(n, 1), :] > 0.0, sn, MASK_VALUE))
            m_new = m_prev
            for sn in scores:
                m_new = jnp.maximum(m_new, jnp.max(sn, axis=0, keepdims=True))
            alpha = jnp.exp2(m_prev - m_new)
            l_new = alpha * l_prev
            acc = alpha * acc_ref[hh]
            for b, sn in enumerate(scores):
                pn = jnp.exp2(sn - m_new)
                l_new = l_new + jnp.sum(pn, axis=0, keepdims=True)
                acc = acc + jnp.dot(vt_ref[0, hh, g * MOBA_GROUP + b], pn.astype(BF16), preferred_element_type=F32)
            acc_ref[hh] = acc
            out += [m_new, l_new]
        return tuple(out)

    groups = (i + MOBA_GROUP - 1) // MOBA_GROUP
    stats = lax.fori_loop(0, groups, body, tuple(stats))
    for hh in range(NH):
        o_ref[:, lanes[hh]] = (acc_ref[hh] / stats[2 * hh + 1]).T.astype(o_ref.dtype)


def moba(pm, batch, seq, q_col, k_col, v_col):
    T = batch * seq
    H, hd, BLK = MOBA_HEADS, LANES, MOBA_BLOCK
    nb = seq // BLK
    assert seq % BLK == 0 and nb % MOBA_GROUP == 0
    cos_t, sin_lo, sin_hi, half = rope_tables(seq, hd)
    full = pl.BlockSpec((seq, hd), lambda b, h: (0, 0))
    kr, kmean, vt = pl.pallas_call(
        functools.partial(_moba_kv_kernel, nb=nb, rope_half=half),
        grid=(batch, H),
        in_specs=[pl.BlockSpec((seq, hd), lambda b, h: (b, k_col + h)),
                  pl.BlockSpec((seq, hd), lambda b, h: (b, v_col + h)), full, full, full],
        out_specs=[pl.BlockSpec((seq, hd), lambda b, h: (b, h)),
                   pl.BlockSpec((1, 1, nb, hd), lambda b, h: (b, h, 0, 0)),
                   pl.BlockSpec((1, 1, nb, hd, BLK), lambda b, h: (b, h, 0, 0, 0))],
        out_shape=[jax.ShapeDtypeStruct((T, H * hd), BF16), jax.ShapeDtypeStruct((batch, H, nb, hd), F32),
                   jax.ShapeDtypeStruct((batch, H, nb, hd, BLK), BF16)],
        name="moba_kv",
        compiler_params=_params("parallel", "parallel"),
    )(pm, pm, cos_t, sin_lo, sin_hi)
    tab = pl.BlockSpec((BLK, hd), lambda b, h, i: (i, 0))
    NH = MOBA_HEADS_PER_STEP
    assert H % NH == 0 and q_col % NH == 0
    return pl.pallas_call(
        functools.partial(_moba_attn_kernel, nb=nb, rope_half=half),
        grid=(batch, H // NH, nb),
        in_specs=[pl.BlockSpec((BLK, NH * hd), lambda b, h, i: (b * nb + i, q_col // NH + h)), tab, tab, tab,
                  pl.BlockSpec((seq, NH * hd), lambda b, h, i: (b, h)),
                  pl.BlockSpec((1, NH, nb, hd, BLK), lambda b, h, i: (b, h, 0, 0, 0)),
                  pl.BlockSpec((1, NH, nb, hd), lambda b, h, i: (b, h, 0, 0))],
        out_specs=pl.BlockSpec((BLK, NH * hd), lambda b, h, i: (b * nb + i, h)),
        out_shape=jax.ShapeDtypeStruct((T, H * hd), BF16),
        scratch_shapes=[pltpu.VMEM((NH, nb, BLK), F32), pltpu.VMEM((NH, hd, BLK), F32)],
        name="moba_attn",
        compiler_params=_params("parallel", "parallel", "arbitrary"),
    )(pm, cos_t, sin_lo, sin_hi, kr, vt, kmean)


def _gelu_erf(x):
    return 0.5 * x * (1.0 + lax.erf(x * (2.0 ** -0.5)))


def _top_values(xs, count):
    vals = [[] for _ in xs]
    ranks = [jnp.full_like(x, float(count)) for x in xs]
    xs = list(xs)
    for b in range(count):
        for j, x in enumerate(xs):
            m = jnp.max(x, axis=0, keepdims=True)
            vals[j].append(m)
            hit = x >= m
            ranks[j] = jnp.where(hit, float(b), ranks[j])
            xs[j] = jnp.where(hit, -jnp.inf, x)
    return vals, ranks


def _peer_topk_kernel(q_ref, sk_ref, n1_ref, e1_ref, r2_ref, e2_ref):
    nt = (((1,), (1,)), ((), ()))
    K = PEER_TOPK
    q = q_ref[...]
    half = q.shape[-1] // 2
    s1 = lax.dot_general(sk_ref[0, 0], q[:, :half], nt, precision=lax.Precision.HIGHEST, preferred_element_type=F32)
    s2 = lax.dot_general(sk_ref[0, 1], q[:, half:], nt, precision=lax.Precision.HIGHEST, preferred_element_type=F32)
    (v1, v2), (_, rank2) = _top_values([s1, s2], K)
    v2_all = jnp.concatenate(v2, axis=0)
    cand = jnp.concatenate([v1[0] + v2_all] + [v1[a] + v2_all[:K // 2] for a in range(1, K)], axis=0)
    tau = _top_values([cand], K)[0][0][-1]
    top = v1[0] + v2[0]
    z = jnp.sum(jnp.where(cand >= tau, jnp.exp(cand - top), 0.0), axis=0, keepdims=True)
    count = jnp.zeros_like(s1)
    for b in range(K):
        count = count + jnp.where(s1 + v2[b] >= tau, 1.0, 0.0)
    n1_ref[0] = jnp.where(s1 >= v1[K - 1], count, 0.0)
    e1_ref[0] = jnp.exp(s1 - v1[0])
    r2_ref[0] = rank2.astype(BF16)
    e2_ref[0] = (jnp.exp(s2 - v2[0]) / z).astype(BF16)


def peer_topk(q, subkeys, tt=512):
    T = q.shape[0]
    H, _, NK, dh = subkeys.shape
    assert NK == PEER_NKEYS
    spec = pl.BlockSpec((1, NK, tt), lambda i, h: (h, 0, i))
    shape = jax.ShapeDtypeStruct((H, NK, T), F32)
    shape2 = jax.ShapeDtypeStruct((H, NK, T), BF16)
    return pl.pallas_call(
        _peer_topk_kernel,
        grid=(T // tt, H),
        in_specs=[pl.BlockSpec((tt, 2 * dh), lambda i, h: (i, h)),
                  pl.BlockSpec((1, 2, NK, dh), lambda i, h: (h, 0, 0, 0))],
        out_specs=[spec] * 4,
        out_shape=[shape, shape, shape2, shape2],
        name="peer_topk",
        compiler_params=_params("parallel", "parallel"),
    )(q, subkeys)


def _peer_gate_kernel(ht_ref, u_ref, scale_ref, wscale_ref, n1_ref, e1_ref, r2_ref, e2_ref, w_ref, act_ref, g_ref,
                      *, rows_per_tile):
    NK = PEER_NKEYS
    nchunks = ht_ref.shape[1] // LANES
    zero = jnp.zeros((NK, LANES), BF16)
    packed_rows = 2 * SUBLANES

    def key1_rows(ref, h, k1, cols):
        r8 = jnp.broadcast_to(ref[h, k1:k1 + 1, cols], (SUBLANES, LANES)).astype(BF16)
        r16 = jnp.concatenate([r8, r8], axis=0)
        return jnp.broadcast_to(r16[None], (NK // packed_rows, packed_rows, LANES)).reshape(NK, LANES)

    for s in range(rows_per_tile // PEER_SLICE_ROWS):
        rows = slice(s * PEER_SLICE_ROWS * NK, (s + 1) * PEER_SLICE_ROWS * NK)
        act_ref[rows, :] = jnp.dot(u_ref[rows, :], ht_ref[...], preferred_element_type=F32)
        for k1 in range(s * PEER_SLICE_ROWS, (s + 1) * PEER_SLICE_ROWS):
            for c in range(nchunks):
                cols = slice(c * LANES, (c + 1) * LANES)
                g = None
                for h in range(PEER_HEADS):
                    selected = r2_ref[h, :, cols] < key1_rows(n1_ref, h, k1, cols)
                    term = jnp.where(selected, e2_ref[h, :, cols], zero) * key1_rows(e1_ref, h, k1, cols)
                    g = term if g is None else g + term
                g_ref[k1 * NK:(k1 + 1) * NK, cols] = g.astype(F32)
    for k1 in range(rows_per_tile):
        erows = slice(k1 * NK, (k1 + 1) * NK)
        for c in range(nchunks):
            cols = slice(c * LANES, (c + 1) * LANES)
            w = g_ref[erows, cols] * _gelu_erf(act_ref[erows, cols] * scale_ref[:, cols]) * wscale_ref[:, cols]
            w_ref[cols, erows] = w.T.astype(w_ref.dtype)


def peer_gate(ht, u, layer, scale, wscale, tables, tt=PEER_TOKEN_TILE, te=PEER_EXPERT_TILE):
    D, T = ht.shape
    E = u.shape[1]
    H, NK, _ = tables[0].shape
    tt = min(tt, T)
    assert T % tt == 0 and E % te == 0
    rows_per_tile = te // NK
    assert rows_per_tile % SUBLANES == 0
    k1spec = pl.BlockSpec((H, rows_per_tile, tt), lambda i, j: (0, j, i))
    k2spec = pl.BlockSpec((H, NK, tt), lambda i, j: (0, 0, i))
    return pl.pallas_call(
        functools.partial(_peer_gate_kernel, rows_per_tile=rows_per_tile),
        grid=(T // tt, E // te),
        in_specs=[pl.BlockSpec((D, tt), lambda i, j: (0, i)),
                  pl.BlockSpec((None, te, D), lambda i, j: (layer, j, 0)),
                  pl.BlockSpec((1, tt), lambda i, j: (0, i)),
                  pl.BlockSpec((1, tt), lambda i, j: (0, i)),
                  k1spec, k1spec, k2spec, k2spec],
        out_specs=pl.BlockSpec((tt, te), lambda i, j: (i, j)),
        out_shape=jax.ShapeDtypeStruct((T, E), FP8),
        scratch_shapes=[pltpu.VMEM((te, tt), F32), pltpu.VMEM((te, tt), F32)],
        name="peer_gate",
        compiler_params=_params("parallel", "arbitrary"),
    )(ht, u, scale, wscale, *tables)


def quantise_table(w):
    amax = jnp.maximum(jnp.max(jnp.abs(w), axis=(1, 2)), FP8_TINY)
    row_norm = jnp.sqrt(jnp.max(jnp.sum(w * w, axis=2), axis=1))
    return (w * (FP8_TARGET_MAX / amax)[:, None, None]).astype(FP8), amax * (1.0 / FP8_TARGET_MAX), row_norm


def peer_ffn(x, layer, norm_g, wq, subkeys, u_quant, v_quant):
    u_fp8, u_scale, u_norm = u_quant
    v_fp8, v_scale, _ = v_quant
    h, ht, t_scale, h_norm = rmsnorm(x, norm_g, BF16, with_fp8_transpose=True)
    q = matmul(h, wq, layer=layer, name="peer_query")
    tables = peer_topk(q, subkeys)
    bound = jnp.maximum(PEER_HEADS * h_norm * u_norm[layer], FP8_TINY)
    w = peer_gate(ht, u_fp8, layer, t_scale * u_scale[layer], FP8_TARGET_MAX / bound, tables)
    undo = (bound * (v_scale[layer] / FP8_TARGET_MAX)).reshape(-1, 1)
    return matmul(w, v_fp8, res=x, row_scale=undo, layer=layer, name="peer_out")


def _in_proj_layout(group_width):
    gw = group_width
    parts = ((('lru_x', 'lru_gate', 'gla_q', 'gla_k', 'gla_v'), (gw, gw, gw // 2, gw // 2, gw)),
             (('gla_og', 's5_u', 'moba_q', 'moba_k', 'moba_v'), (gw,) * 5))
    cols = {}
    for names, sizes in parts:
        c = 0
        for name, size in zip(names, sizes):
            cols[name] = c // LANES
            c += size
    return sum(parts[0][1]), cols


def kernel(x, norm1_g, w_in, lru_conv_w, lru_conv_b, lru_wa, lru_ba, lru_wx, lru_bx, lru_lambda, gla_wg2, gla_bg, gla_norm_g, s5_a_re, s5_a_im, s5_log_step, s5_b_re, s5_b_im, s5_c_re, s5_c_im, s5_d, s5_w_glu, s5_b_glu, w_out, norm2_g, peer_wq, peer_subkeys, peer_u, peer_v, final_norm_g):
    B, S, D = x.shape
    T = B * S
    depth = norm1_g.shape[0]
    gw = D // N_MIXERS
    x = x.reshape(T, D)
    lr0, col = _in_proj_layout(gw)
    w_in_t = jnp.transpose(w_in, (0, 2, 1)).astype(BF16)
    w_b = w_in_t[:, lr0 + GLA_GATE_RANK:]
    w_lr = jnp.zeros((depth, LANES, D), BF16).at[:, :GLA_GATE_RANK].set(w_in_t[:, lr0:lr0 + GLA_GATE_RANK])
    w_out_bf16 = w_out.astype(BF16)
    peer_wq_bf16 = peer_wq.astype(BF16)
    peer_u_quant = quantise_table(peer_u)
    peer_v_quant = quantise_table(peer_v)
    for l in range(depth):
        h = rmsnorm(x, norm1_g[l], BF16)
        pa = matmul(h, w_in_t, layer=l, n_out=lr0, w_transposed=True, name="in_proj_a")
        pb = matmul(h, w_b, layer=l, w_transposed=True, name="in_proj_b")
        plr = matmul(h, w_lr, layer=l, w_transposed=True, name="in_proj_lr")
        y_a = rglru(pa, B, S, col['lru_x'], col['lru_gate'], lru_conv_w[l], lru_conv_b[l], lru_wa[l], lru_ba[l],
                    lru_wx[l], lru_bx[l], lru_lambda[l])
        y_b = gla(pa, pb, plr, B, S, col['gla_q'], col['gla_k'], col['gla_v'], col['gla_og'],
                  gla_wg2[l], gla_bg[l], gla_norm_g[l])
        prep = s5_prepare(s5_a_re[l], s5_a_im[l], s5_log_step[l], s5_b_re[l], s5_b_im[l], s5_c_re[l], s5_c_im[l])
        z = s5_ssm(pb, B, S, col['s5_u'], prep, s5_d[l])
        y_c = glu(z, s5_w_glu[l], s5_b_glu[l])
        y_d = moba(pb, B, S, col['moba_q'], col['moba_k'], col['moba_v'])
        mixed = jnp.concatenate([y_a, y_b, y_c, y_d], axis=-1)
        x = matmul(mixed, w_out_bf16, res=x, layer=l, name="out_proj")
        x = peer_ffn(x, l, norm2_g[l], peer_wq_bf16, peer_subkeys[l], peer_u_quant, peer_v_quant)
    return rmsnorm(x, final_norm_g, F32).reshape(B, S, D)
```

```python
import functools
import math

import jax
import jax.numpy as jnp
from jax import lax
from jax.experimental import pallas as pl
from jax.experimental.pallas import tpu as pltpu

F32 = jnp.float32
BF16 = jnp.bfloat16
FP8 = jnp.float8_e4m3fn
FP8_TARGET_MAX = 256.0
FP8_TINY = 1e-30

LANES = 128
SUBLANES = 8
VMEM_LIMIT_BYTES = 56 * 2**20

RMS_EPS = 1e-6
N_MIXERS = 4

LRU_BLOCKS = 8
LRU_CONV = 4
LRU_C = 8.0
LRU_ROWS = 256

GLA_HEADS = 4
GLA_GATE_RANK = 16
GLA_TAU = 16.0
GLA_CHUNK = 64
GLA_TILE = 512
GLA_HEADS_PER_STEP = 2

S5_GROUP = 16
S5_STATE = 64
S5_SLAB_GROUPS = LANES // S5_GROUP
S5_SLAB_STATES = S5_SLAB_GROUPS * S5_STATE
S5_TILE = 256

MOBA_HEADS = 8
MOBA_BLOCK = 256
MOBA_TOPK = 3
MOBA_GROUP = 4
MOBA_HEADS_PER_STEP = 8
ROPE_THETA = 500000.0
MASK_VALUE = -1e30

PEER_HEADS = 8
PEER_NKEYS = 128
PEER_TOPK = 16
PEER_TOKEN_TILE = 1024
PEER_EXPERT_TILE = 1024
PEER_SLICE_ROWS = 2


def _params(*semantics, allow_input_fusion=None):
    return pltpu.CompilerParams(dimension_semantics=semantics, vmem_limit_bytes=VMEM_LIMIT_BYTES,
                                allow_input_fusion=allow_input_fusion)


def _rmsnorm_kernel(x_ref, g_ref, o_ref, *maybe_quantised):
    x = x_ref[...]
    y = x * lax.rsqrt(jnp.mean(x * x, axis=-1, keepdims=True) + RMS_EPS) * g_ref[...]
    o_ref[...] = y.astype(o_ref.dtype)
    if maybe_quantised:
        qt_ref, scale_ref, norm_ref = maybe_quantised
        yt = y.T
        amax = jnp.maximum(jnp.max(jnp.abs(yt), axis=0, keepdims=True), FP8_TINY)
        qt_ref[...] = (yt * (FP8_TARGET_MAX / amax)).astype(qt_ref.dtype)
        scale_ref[...] = amax * (1.0 / FP8_TARGET_MAX)
        norm_ref[...] = jnp.sqrt(jnp.sum(yt * yt, axis=0, keepdims=True))


def rmsnorm(x, g, out_dtype, with_fp8_transpose=False, tm=256):
    T, D = x.shape
    out_shape = [jax.ShapeDtypeStruct((T, D), out_dtype)]
    out_specs = [pl.BlockSpec((tm, D), lambda i: (i, 0))]
    with_transpose = with_fp8_transpose
    if with_transpose:
        row = pl.BlockSpec((1, tm), lambda i: (0, i))
        out_shape += [jax.ShapeDtypeStruct((D, T), FP8)] + [jax.ShapeDtypeStruct((1, T), F32)] * 2
        out_specs += [pl.BlockSpec((D, tm), lambda i: (0, i)), row, row]
    res = pl.pallas_call(
        _rmsnorm_kernel,
        grid=(T // tm,),
        in_specs=[pl.BlockSpec((tm, D), lambda i: (i, 0)), pl.BlockSpec((1, D), lambda i: (0, 0))],
        out_specs=out_specs,
        out_shape=out_shape,
        name="rmsnorm",
        compiler_params=_params("parallel"),
    )(x, g.reshape(1, D))
    return res if with_transpose else res[0]


def _mm_kernel(x_ref, w_ref, *rest, nk, has_res, has_scale, w_transposed):
    rest = list(rest)
    s_ref = rest.pop(0) if has_scale else None
    r_ref = rest.pop(0) if has_res else None
    o_ref = rest.pop(0)
    contract = (((1,), (1 if w_transposed else 0,)), ((), ()))
    part = lax.dot_general(x_ref[...], w_ref[...], contract, preferred_element_type=F32)

    def finish(acc):
        if has_scale:
            acc = acc * s_ref[...]
        if has_res:
            acc = acc + r_ref[...]
        o_ref[...] = acc.astype(o_ref.dtype)

    if nk == 1:
        finish(part)
        return
    acc_ref = rest.pop(0)
    k = pl.program_id(2)

    @pl.when(k == 0)
    def _():
        acc_ref[...] = part

    @pl.when(k > 0)
    def _():
        acc_ref[...] += part

    @pl.when(k == nk - 1)
    def _():
        finish(acc_ref[...])


def matmul(x, w, res=None, *, row_scale=None, layer=None, n_out=None, w_transposed=False, tm=1024, tn=1024,
           tk=4096, out_dtype=F32, fuse_x_producer=False, name="matmul"):
    M, K = x.shape
    k_axis, n_axis = (-1, -2) if w_transposed else (-2, -1)
    N = w.shape[n_axis] if n_out is None else n_out
    tm, tn, tk = min(tm, M), min(tn, N), min(tk, K)
    assert M % tm == 0 and N % tn == 0 and K % tk == 0 and w.shape[k_axis] == K
    nk = K // tk
    wblock = (tn, tk) if w_transposed else (tk, tn)
    windex = (lambda i, j, k: (j, k)) if w_transposed else (lambda i, j, k: (k, j))
    if layer is None:
        wspec = pl.BlockSpec(wblock, windex)
    else:
        wspec = pl.BlockSpec((None,) + wblock, lambda i, j, k: (layer,) + windex(i, j, k))
    in_specs = [pl.BlockSpec((tm, tk), lambda i, j, k: (i, k)), wspec]
    args = [x, w]
    if row_scale is not None:
        in_specs.append(pl.BlockSpec((tm, 1), lambda i, j, k: (i, 0)))
        args.append(row_scale)
    if res is not None:
        in_specs.append(pl.BlockSpec((tm, tn), lambda i, j, k: (i, j)))
        args.append(res)
    return pl.pallas_call(
        functools.partial(_mm_kernel, nk=nk, has_res=res is not None, has_scale=row_scale is not None,
                          w_transposed=w_transposed),
        grid=(M // tm, N // tn, nk),
        in_specs=in_specs,
        out_specs=pl.BlockSpec((tm, tn), lambda i, j, k: (i, j)),
        out_shape=jax.ShapeDtypeStruct((M, N), out_dtype),
        scratch_shapes=[pltpu.VMEM((tm, tn), F32)] if nk > 1 else [],
        name=name,
        compiler_params=_params("parallel", "parallel", "arbitrary",
                                allow_input_fusion=[fuse_x_producer] + [False] * (len(args) - 1)),
    )(*args)


def _lru_kernel(x_ref, gate_ref, cw_ref, cb_ref, wa_ref, ba_ref, wx_ref, bx_ref, lam_ref, o_ref, *, seq):
    R = LRU_ROWS
    cw = cw_ref[...]
    cb = cb_ref[...]
    ba = ba_ref[...]
    bx = bx_ref[...]
    neg_c_softplus = -LRU_C * jax.nn.softplus(-lam_ref[...])
    wa = wa_ref[0]
    wx = wx_ref[0]
    row = lax.broadcasted_iota(jnp.int32, (R, LANES), 0)
    row8 = lax.broadcasted_iota(jnp.int32, (SUBLANES, LANES), 0)

    def body(c, h):
        r0 = pl.multiple_of(c * R, R)
        xt = x_ref[pl.ds(r0, R), :]
        p0 = pl.multiple_of(jnp.maximum(r0 - SUBLANES, 0), SUBLANES)
        prev = jnp.where(c > 0, x_ref[pl.ds(p0, SUBLANES), :], 0.0)
        xc = xt * cw[LRU_CONV - 1:LRU_CONV] + cb
        for d in range(1, LRU_CONV):
            rolled = pltpu.roll(xt, d, 0)
            head = jnp.where(row8 < d, pltpu.roll(prev, d, 0), rolled[:SUBLANES])
            shifted = jnp.concatenate([head, rolled[SUBLANES:]], axis=0)
            xc = xc + shifted * cw[LRU_CONV - 1 - d:LRU_CONV - d]
        xb = xc.astype(BF16)
        r = jax.nn.sigmoid(jnp.dot(xb, wa, preferred_element_type=F32) + ba)
        i = jax.nn.sigmoid(jnp.dot(xb, wx, preferred_element_type=F32) + bx)
        log_a = r * neg_c_softplus
        a = jnp.exp(log_a)
        b = jnp.sqrt(1.0 - a * a) * (i * xc)
        d = 1
        while d < R:
            keep = row >= d
            a_sh = jnp.where(keep, pltpu.roll(a, d, 0), 1.0)
            b_sh = jnp.where(keep, pltpu.roll(b, d, 0), 0.0)
            b = a * b_sh + b
            a = a * a_sh
            d *= 2
        hs = b + a * h
        o_ref[pl.ds(r0, R), :] = (hs * jax.nn.gelu(gate_ref[pl.ds(r0, R), :])).astype(o_ref.dtype)
        return hs[R - 1:R, :]

    lax.fori_loop(0, seq // R, body, jnp.zeros((1, LANES), F32))


def rglru(pm, batch, seq, x_col, gate_col, conv_w, conv_b, wa, ba, wx, bx, lam):
    T = batch * seq
    W = LRU_BLOCKS * LANES
    vec = lambda v: v.reshape(1, W)
    vspec = pl.BlockSpec((1, LANES), lambda b, n: (0, n))
    wspec = pl.BlockSpec((1, LANES, LANES), lambda b, n: (n, 0, 0))
    return pl.pallas_call(
        functools.partial(_lru_kernel, seq=seq),
        grid=(batch, LRU_BLOCKS),
        in_specs=[pl.BlockSpec((seq, LANES), lambda b, n: (b, x_col + n)),
                  pl.BlockSpec((seq, LANES), lambda b, n: (b, gate_col + n)),
                  pl.BlockSpec((LRU_CONV, LANES), lambda b, n: (0, n)),
                  vspec, wspec, vspec, wspec, vspec, vspec],
        out_specs=pl.BlockSpec((seq, LANES), lambda b, n: (b, n)),
        out_shape=jax.ShapeDtypeStruct((T, W), BF16),
        name="rglru",
        compiler_params=_params("parallel", "parallel"),
    )(pm, pm, conv_w, vec(conv_b), wa.astype(BF16), vec(ba), wx.astype(BF16), vec(bx), vec(lam))


def _gla_kernel(q_ref, k_ref, v_ref, og_ref, lr_ref, wg_ref, bg_ref, ng_ref, o_ref, st_ref):
    C = GLA_CHUNK
    NH = GLA_HEADS_PER_STEP
    dk = q_ref.shape[-1] // NH
    dv = v_ref.shape[-1] // NH

    @pl.when(pl.program_id(2) == 0)
    def _():
        st_ref[...] = jnp.zeros_like(st_ref)

    TT = GLA_TILE
    NC = TT // C
    ri = lax.broadcasted_iota(jnp.int32, (TT, TT), 0)
    ci = lax.broadcasted_iota(jnp.int32, (TT, TT), 1)
    causal = ((ri // C) == (ci // C)) & (ri >= ci)
    tril = jnp.where(causal, 1.0, 0.0).astype(BF16)
    scale = dk ** -0.5
    nt = (((1,), (1,)), ((), ()))
    tn = (((0,), (0,)), ((), ()))
    lr = lr_ref[...]

    for hh in range(NH):
        kcols = slice(hh * dk, (hh + 1) * dk)
        vcols = slice(hh * dv, (hh + 1) * dv)
        q = q_ref[:, kcols]
        k = k_ref[:, kcols]
        v = v_ref[:, vcols].astype(BF16)
        pre = jnp.dot(lr, wg_ref[hh], precision=lax.Precision.HIGHEST, preferred_element_type=F32) + bg_ref[hh]
        g = jax.nn.log_sigmoid(pre) / GLA_TAU
        g_hi = g.astype(BF16)
        g_lo = (g - g_hi.astype(F32)).astype(BF16)
        bc = (jnp.dot(tril, g_hi, preferred_element_type=F32) + jnp.dot(tril, g_lo, preferred_element_type=F32))
        b_last = jnp.broadcast_to(bc.reshape(NC, C, dk)[:, C - 1:C, :], (NC, C, dk)).reshape(TT, dk)
        qe = (q * scale * jnp.exp(bc)).astype(BF16)
        ke = (k * jnp.exp(-bc)).astype(BF16)
        kd = (k * jnp.exp(b_last - bc)).astype(BF16)
        att = lax.dot_general(qe, ke, nt, preferred_element_type=F32)
        att = jnp.where(causal, att, 0.0).astype(BF16)
        o = jnp.dot(att, v, preferred_element_type=F32)
        st = st_ref[hh]
        carried = []
        for c in range(NC):
            rows = slice(c * C, (c + 1) * C)
            carried.append(lax.dot_general(qe[rows], st.astype(BF16), nt, preferred_element_type=F32))
            decay = jnp.exp(b_last[c * C:c * C + 1, :])
            st = st * decay + lax.dot_general(v[rows], kd[rows], tn, preferred_element_type=F32)
        st_ref[hh] = st
        o = o + jnp.concatenate(carried, axis=0)
        o = o * lax.rsqrt(jnp.mean(o * o, axis=-1, keepdims=True) + RMS_EPS)
        o_ref[:, vcols] = (o * ng_ref[hh] * jax.nn.silu(og_ref[:, vcols])).astype(o_ref.dtype)


def gla(pm, pm_og, plr, batch, seq, q_col, k_col, v_col, og_col, wg2, bg, norm_g):
    T = batch * seq
    H = GLA_HEADS
    dk = wg2.shape[-1] // H
    dv = norm_g.shape[-1] // H
    assert dk == LANES and dv == 2 * LANES
    nt = seq // GLA_TILE
    wg = jnp.zeros((LANES, H * dk), F32).at[:GLA_GATE_RANK].set(wg2)
    wg = wg.reshape(LANES, H, dk).transpose(1, 0, 2)
    tok = lambda b, h, t: b * nt + t
    NH = GLA_HEADS_PER_STEP
    kb, vb = NH * dk // LANES, NH * dv // LANES
    assert H % NH == 0 and q_col % kb == 0 and k_col % kb == 0 and v_col % vb == 0 and og_col % vb == 0
    return pl.pallas_call(
        _gla_kernel,
        grid=(batch, H // NH, nt),
        in_specs=[pl.BlockSpec((GLA_TILE, NH * dk), lambda b, h, t: (tok(b, h, t), q_col // kb + h)),
                  pl.BlockSpec((GLA_TILE, NH * dk), lambda b, h, t: (tok(b, h, t), k_col // kb + h)),
                  pl.BlockSpec((GLA_TILE, NH * dv), lambda b, h, t: (tok(b, h, t), v_col // vb + h)),
                  pl.BlockSpec((GLA_TILE, NH * dv), lambda b, h, t: (tok(b, h, t), og_col // vb + h)),
                  pl.BlockSpec((GLA_TILE, LANES), lambda b, h, t: (tok(b, h, t), 0)),
                  pl.BlockSpec((NH, LANES, dk), lambda b, h, t: (h, 0, 0)),
                  pl.BlockSpec((NH, 1, dk), lambda b, h, t: (h, 0, 0)),
                  pl.BlockSpec((NH, 1, dv), lambda b, h, t: (h, 0, 0))],
        out_specs=pl.BlockSpec((GLA_TILE, NH * dv), lambda b, h, t: (tok(b, h, t), h)),
        out_shape=jax.ShapeDtypeStruct((T, H * dv), BF16),
        scratch_shapes=[pltpu.VMEM((NH, dv, dk), F32)],
        name="gla",
        compiler_params=_params("parallel", "parallel", "arbitrary"),
    )(pm, pm, pm, pm_og, plr, wg, bg.reshape(H, 1, dk), norm_g.reshape(H, 1, dv))


def _s5_kernel(u_ref, bcat_ref, ccat_ref, pr_ref, pi_ref, d_ref, z_ref, x_ref, carry_ref):
    TT = S5_TILE
    NS = 2 * S5_SLAB_STATES

    @pl.when(pl.program_id(2) == 0)
    def _():
        carry_ref[...] = jnp.zeros_like(carry_ref)

    u = u_ref[...]
    pr = pr_ref[0]
    pi = pi_ref[0]

    def swap(t):
        return pltpu.roll(t, S5_SLAB_STATES, 1)

    sub = lax.broadcasted_iota(jnp.int32, u.shape, 0) % SUBLANES
    lagged = [u.astype(BF16)]
    for d in range(1, SUBLANES):
        lagged.append(jnp.where(sub >= d, pltpu.roll(u, d, 0), 0.0).astype(BF16))
    x_ref[...] = jnp.dot(jnp.concatenate(lagged, axis=1), bcat_ref[0], preferred_element_type=F32)

    def carry_step(i, carry):
        rows = pl.ds(pl.multiple_of(i * SUBLANES, SUBLANES), SUBLANES)
        cb = jnp.broadcast_to(carry, (SUBLANES, NS))
        blk = x_ref[rows, :] + pr * cb + pi * swap(cb)
        x_ref[rows, :] = blk
        return blk[SUBLANES - 1:SUBLANES, :]

    carry_ref[...] = lax.fori_loop(0, TT // SUBLANES, carry_step, carry_ref[...], unroll=4)
    y = jnp.dot(x_ref[...].astype(BF16), ccat_ref[0], preferred_element_type=F32) + d_ref[...] * u
    z_ref[...] = jax.nn.gelu(y)


def s5_prepare(a_re, a_im, log_step, b_re, b_im, c_re, c_im):
    G, P = a_re.shape
    H = S5_GROUP
    ns = G // S5_SLAB_GROUPS
    step = jnp.exp(log_step)[:, None]
    mag = jnp.exp(a_re * step)
    ang = a_im * step
    abar_re, abar_im = mag * jnp.cos(ang), mag * jnp.sin(ang)
    den = a_re * a_re + a_im * a_im
    f_re = ((abar_re - 1.0) * a_re + abar_im * a_im) / den
    f_im = (abar_im * a_re - (abar_re - 1.0) * a_im) / den
    bb_re = f_re[..., None] * b_re - f_im[..., None] * b_im
    bb_im = f_re[..., None] * b_im + f_im[..., None] * b_re
    eye = jnp.eye(S5_SLAB_GROUPS, dtype=F32)

    def in_slab(bb):
        t = bb.reshape(ns, S5_SLAB_GROUPS, P, H)
        return jnp.einsum('sgph,gk->sghkp', t, eye).reshape(ns, LANES, S5_SLAB_STATES)

    def out_slab(cc):
        t = cc.reshape(ns, S5_SLAB_GROUPS, H, P)
        return jnp.einsum('sghp,gk->sgpkh', t, eye).reshape(ns, S5_SLAB_STATES, LANES)

    ccat = jnp.concatenate([out_slab(c_re), -out_slab(c_im)], axis=1).astype(BF16)
    pw_re, pw_im = [abar_re], [abar_im]
    for _ in range(SUBLANES - 1):
        r, i = pw_re[-1], pw_im[-1]
        pw_re.append(r * abar_re - i * abar_im)
        pw_im.append(r * abar_im + i * abar_re)
    slab = lambda t: jnp.stack(t, 0).reshape(SUBLANES, ns, S5_SLAB_STATES).transpose(1, 0, 2)
    pr, pi = slab(pw_re), slab(pw_im)
    bre, bim = in_slab(bb_re), in_slab(bb_im)
    lag_re = [bre] + [bre * pr[:, j:j + 1] - bim * pi[:, j:j + 1] for j in range(SUBLANES - 1)]
    lag_im = [bim] + [bre * pi[:, j:j + 1] + bim * pr[:, j:j + 1] for j in range(SUBLANES - 1)]
    bcat = jnp.concatenate([jnp.concatenate(lag_re, axis=1), jnp.concatenate(lag_im, axis=1)], axis=-1).astype(BF16)
    return bcat, ccat, jnp.concatenate([pr, pr], -1), jnp.concatenate([-pi, pi], -1)


def s5_ssm(pm, batch, seq, u_col, prep, d):
    T = batch * seq
    bcat, ccat, pr, pi = prep
    ns = bcat.shape[0]
    NS = 2 * S5_SLAB_STATES
    nt = seq // S5_TILE
    return pl.pallas_call(
        _s5_kernel,
        grid=(batch, ns, nt),
        in_specs=[pl.BlockSpec((S5_TILE, LANES), lambda b, s, t: (b * nt + t, u_col + s)),
                  pl.BlockSpec((1, SUBLANES * LANES, NS), lambda b, s, t: (s, 0, 0)),
                  pl.BlockSpec((1, NS, LANES), lambda b, s, t: (s, 0, 0)),
                  pl.BlockSpec((1, SUBLANES, NS), lambda b, s, t: (s, 0, 0)),
                  pl.BlockSpec((1, SUBLANES, NS), lambda b, s, t: (s, 0, 0)),
                  pl.BlockSpec((1, LANES), lambda b, s, t: (0, s))],
        out_specs=pl.BlockSpec((S5_TILE, LANES), lambda b, s, t: (b * nt + t, s)),
        out_shape=jax.ShapeDtypeStruct((T, ns * LANES), F32),
        scratch_shapes=[pltpu.VMEM((S5_TILE, NS), F32), pltpu.VMEM((1, NS), F32)],
        name="s5_ssm",
        compiler_params=_params("parallel", "parallel", "arbitrary"),
    )(pm, bcat, ccat, pr, pi, d.reshape(1, -1))


def _glu_kernel(z_ref, w_ref, b_ref, o_ref):
    z = z_ref[...]
    y = jnp.dot(z.astype(BF16), w_ref[...], preferred_element_type=F32) + b_ref[...]
    o_ref[...] = (z * jax.nn.sigmoid(y)).astype(o_ref.dtype)


def glu(z, w, b, tm=512):
    T, W = z.shape
    return pl.pallas_call(
        _glu_kernel,
        grid=(T // tm,),
        in_specs=[pl.BlockSpec((tm, W), lambda i: (i, 0)),
                  pl.BlockSpec((W, W), lambda i: (0, 0)),
                  pl.BlockSpec((1, W), lambda i: (0, 0))],
        out_specs=pl.BlockSpec((tm, W), lambda i: (i, 0)),
        out_shape=jax.ShapeDtypeStruct((T, W), BF16),
        name="s5_glu",
        compiler_params=_params("parallel"),
    )(z, w.astype(BF16), b.reshape(1, W))


def _rope(t, cos, sin_lo, sin_hi, rope_half):
    return (t * cos + pltpu.roll(t, LANES - rope_half, 1) * sin_lo + pltpu.roll(t, rope_half, 1) * sin_hi)


def rope_tables(seq, head_dim):
    rope_dims = head_dim // 4
    half = rope_dims // 2
    inv = jnp.power(ROPE_THETA, -jnp.arange(half, dtype=F32) / half)
    ang = jnp.arange(seq).astype(F32)[:, None] * inv[None, :]
    cos, sin = jnp.cos(ang), jnp.sin(ang)
    pad = head_dim - rope_dims
    zeros = jnp.zeros((seq, half), F32)
    cos_t = jnp.concatenate([cos, cos, jnp.ones((seq, pad), F32)], axis=-1)
    sin_lo = jnp.concatenate([-sin, zeros, jnp.zeros((seq, pad), F32)], axis=-1)
    sin_hi = jnp.concatenate([zeros, sin, jnp.zeros((seq, pad), F32)], axis=-1)
    return cos_t, sin_lo, sin_hi, half


def _moba_kv_kernel(k_ref, v_ref, cos_ref, slo_ref, shi_ref, kr_ref, km_ref, vt_ref, *, nb, rope_half):
    for n in range(nb):
        rows = pl.ds(n * MOBA_BLOCK, MOBA_BLOCK)
        kr = _rope(k_ref[rows, :], cos_ref[rows, :], slo_ref[rows, :], shi_ref[rows, :], rope_half)
        kr_ref[rows, :] = kr.astype(kr_ref.dtype)
        km_ref[0, 0, n:n + 1, :] = jnp.mean(kr, axis=0, keepdims=True)
        vt_ref[0, 0, n] = v_ref[rows, :].T.astype(vt_ref.dtype)


def _moba_attn_kernel(q_ref, cos_ref, slo_ref, shi_ref, kr_ref, vt_ref, km_ref, o_ref, sel_ref, acc_ref,
                      *, nb, rope_half):
    BLK = MOBA_BLOCK
    NH = MOBA_HEADS_PER_STEP
    i = pl.program_id(2)
    hd = q_ref.shape[-1] // NH
    log2_scale = hd ** -0.5 * math.log2(math.e)
    blk_id = lax.broadcasted_iota(jnp.int32, (nb, BLK), 0)
    key = lax.broadcasted_iota(jnp.int32, (BLK, BLK), 0)
    qry = lax.broadcasted_iota(jnp.int32, (BLK, BLK), 1)
    own = pl.ds(pl.multiple_of(i * BLK, BLK), BLK)
    cos, slo, shi = cos_ref[...], slo_ref[...], shi_ref[...]
    lanes = [slice(hh * hd, (hh + 1) * hd) for hh in range(NH)]

    qbs, stats = [], []
    for hh in range(NH):
        qt = _rope(q_ref[:, lanes[hh]], cos, slo, shi, rope_half).T
        gate = jnp.dot(km_ref[0, hh], qt, precision=lax.Precision.HIGHEST, preferred_element_type=F32)
        rank = jnp.zeros((nb, BLK), jnp.int32)
        for m in range(nb):
            gm = gate[m:m + 1, :]
            beats = (gm > gate) | ((gm == gate) & (m < blk_id))
            rank = rank + jnp.where(beats & (m < i), 1, 0)
        sel_ref[hh] = ((blk_id < i) & (rank < MOBA_TOPK)).astype(F32)
        qb = (qt * log2_scale).astype(BF16)
        s = jnp.dot(kr_ref[own, lanes[hh]], qb, preferred_element_type=F32)
        s = jnp.where(key <= qry, s, MASK_VALUE)
        m0 = jnp.max(s, axis=0, keepdims=True)
        p = jnp.exp2(s - m0)
        acc_ref[hh] = jnp.dot(vt_ref[0, hh, i], p.astype(BF16), preferred_element_type=F32)
        qbs.append(qb)
        stats += [m0, jnp.sum(p, axis=0, keepdims=True)]

    def body(g, carry):
        out = []
        for hh in range(NH):
            m_prev, l_prev = carry[2 * hh], carry[2 * hh + 1]
            scores = []
            for b in range(MOBA_GROUP):
                n = g * MOBA_GROUP + b
                rows = pl.ds(pl.multiple_of(n * BLK, BLK), BLK)
                sn = jnp.dot(kr_ref[rows, lanes[hh]], qbs[hh], preferred_element_type=F32)
                scores.append(jnp.where(sel_ref[hh, pl.ds(n, 1), :] > 0.0, sn, MASK_VALUE))
            m_new = m_prev
            for sn in scores:
                m_new = jnp.maximum(m_new, jnp.max(sn, axis=0, keepdims=True))
            alpha = jnp.exp2(m_prev - m_new)
            l_new = alpha * l_prev
            acc = alpha * acc_ref[hh]
            for b, sn in enumerate(scores):
                pn = jnp.exp2(sn - m_new)
                l_new = l_new + jnp.sum(pn, axis=0, keepdims=True)
                acc = acc + jnp.dot(vt_ref[0, hh, g * MOBA_GROUP + b], pn.astype(BF16), preferred_element_type=F32)
            acc_ref[hh] = acc
            out += [m_new, l_new]
        return tuple(out)

    groups = (i + MOBA_GROUP - 1) // MOBA_GROUP
    stats = lax.fori_loop(0, groups, body, tuple(stats))
    for hh in range(NH):
        o_ref[:, lanes[hh]] = (acc_ref[hh] / stats[2 * hh + 1]).T.astype(o_ref.dtype)


def moba(pm, batch, seq, q_col, k_col, v_col):
    T = batch * seq
    H, hd, BLK = MOBA_HEADS, LANES, MOBA_BLOCK
    nb = seq // BLK
    assert seq % BLK == 0 and nb % MOBA_GROUP == 0
    cos_t, sin_lo, sin_hi, half = rope_tables(seq, hd)
    full = pl.BlockSpec((seq, hd), lambda b, h: (0, 0))
    kr, kmean, vt = pl.pallas_call(
        functools.partial(_moba_kv_kernel, nb=nb, rope_half=half),
        grid=(batch, H),
        in_specs=[pl.BlockSpec((seq, hd), lambda b, h: (b, k_col + h)),
                  pl.BlockSpec((seq, hd), lambda b, h: (b, v_col + h)), full, full, full],
        out_specs=[pl.BlockSpec((seq, hd), lambda b, h: (b, h)),
                   pl.BlockSpec((1, 1, nb, hd), lambda b, h: (b, h, 0, 0)),
                   pl.BlockSpec((1, 1, nb, hd, BLK), lambda b, h: (b, h, 0, 0, 0))],
        out_shape=[jax.ShapeDtypeStruct((T, H * hd), BF16), jax.ShapeDtypeStruct((batch, H, nb, hd), F32),
                   jax.ShapeDtypeStruct((batch, H, nb, hd, BLK), BF16)],
        name="moba_kv",
        compiler_params=_params("parallel", "parallel"),
    )(pm, pm, cos_t, sin_lo, sin_hi)
    tab = pl.BlockSpec((BLK, hd), lambda b, h, i: (i, 0))
    NH = MOBA_HEADS_PER_STEP
    assert H % NH == 0 and q_col % NH == 0
    return pl.pallas_call(
        functools.partial(_moba_attn_kernel, nb=nb, rope_half=half),
        grid=(batch, H // NH, nb),
        in_specs=[pl.BlockSpec((BLK, NH * hd), lambda b, h, i: (b * nb + i, q_col // NH + h)), tab, tab, tab,
                  pl.BlockSpec((seq, NH * hd), lambda b, h, i: (b, h)),
                  pl.BlockSpec((1, NH, nb, hd, BLK), lambda b, h, i: (b, h, 0, 0, 0)),
                  pl.BlockSpec((1, NH, nb, hd), lambda b, h, i: (b, h, 0, 0))],
        out_specs=pl.BlockSpec((BLK, NH * hd), lambda b, h, i: (b * nb + i, h)),
        out_shape=jax.ShapeDtypeStruct((T, H * hd), BF16),
        scratch_shapes=[pltpu.VMEM((NH, nb, BLK), F32), pltpu.VMEM((NH, hd, BLK), F32)],
        name="moba_attn",
        compiler_params=_params("parallel", "parallel", "arbitrary"),
    )(pm, cos_t, sin_lo, sin_hi, kr, vt, kmean)


def _gelu_erf(x):
    return 0.5 * x * (1.0 + lax.erf(x * (2.0 ** -0.5)))


def _top_values(xs, count):
    vals = [[] for _ in xs]
    ranks = [jnp.full_like(x, float(count)) for x in xs]
    xs = list(xs)
    for b in range(count):
        for j, x in enumerate(xs):
            m = jnp.max(x, axis=0, keepdims=True)
            vals[j].append(m)
            hit = x >= m
            ranks[j] = jnp.where(hit, float(b), ranks[j])
            xs[j] = jnp.where(hit, -jnp.inf, x)
    return vals, ranks


def _peer_topk_kernel(q_ref, sk_ref, n1_ref, e1_ref, r2_ref, e2_ref):
    nt = (((1,), (1,)), ((), ()))
    K = PEER_TOPK
    q = q_ref[...]
    half = q.shape[-1] // 2
    s1 = lax.dot_general(sk_ref[0, 0], q[:, :half], nt, precision=lax.Precision.HIGHEST, preferred_element_type=F32)
    s2 = lax.dot_general(sk_ref[0, 1], q[:, half:], nt, precision=lax.Precision.HIGHEST, preferred_element_type=F32)
    (v1, v2), (_, rank2) = _top_values([s1, s2], K)
    v2_all = jnp.concatenate(v2, axis=0)
    cand = jnp.concatenate([v1[0] + v2_all] + [v1[a] + v2_all[:K // 2] for a in range(1, K)], axis=0)
    tau = _top_values([cand], K)[0][0][-1]
    top = v1[0] + v2[0]
    z = jnp.sum(jnp.where(cand >= tau, jnp.exp(cand - top), 0.0), axis=0, keepdims=True)
    count = jnp.zeros_like(s1)
    for b in range(K):
        count = count + jnp.where(s1 + v2[b] >= tau, 1.0, 0.0)
    n1_ref[0] = jnp.where(s1 >= v1[K - 1], count, 0.0)
    e1_ref[0] = jnp.exp(s1 - v1[0])
    r2_ref[0] = rank2.astype(BF16)
    e2_ref[0] = (jnp.exp(s2 - v2[0]) / z).astype(BF16)


def peer_topk(q, subkeys, tt=512):
    T = q.shape[0]
    H, _, NK, dh = subkeys.shape
    assert NK == PEER_NKEYS
    spec = pl.BlockSpec((1, NK, tt), lambda i, h: (h, 0, i))
    shape = jax.ShapeDtypeStruct((H, NK, T), F32)
    shape2 = jax.ShapeDtypeStruct((H, NK, T), BF16)
    return pl.pallas_call(
        _peer_topk_kernel,
        grid=(T // tt, H),
        in_specs=[pl.BlockSpec((tt, 2 * dh), lambda i, h: (i, h)),
                  pl.BlockSpec((1, 2, NK, dh), lambda i, h: (h, 0, 0, 0))],
        out_specs=[spec] * 4,
        out_shape=[shape, shape, shape2, shape2],
        name="peer_topk",
        compiler_params=_params("parallel", "parallel"),
    )(q, subkeys)


def _peer_gate_kernel(ht_ref, u_ref, scale_ref, wscale_ref, n1_ref, e1_ref, r2_ref, e2_ref, w_ref, act_ref, g_ref,
                      *, rows_per_tile):
    NK = PEER_NKEYS
    nchunks = ht_ref.shape[1] // LANES
    zero = jnp.zeros((NK, LANES), BF16)
    packed_rows = 2 * SUBLANES

    def key1_rows(ref, h, k1, cols):
        r8 = jnp.broadcast_to(ref[h, k1:k1 + 1, cols], (SUBLANES, LANES)).astype(BF16)
        r16 = jnp.concatenate([r8, r8], axis=0)
        return jnp.broadcast_to(r16[None], (NK // packed_rows, packed_rows, LANES)).reshape(NK, LANES)

    for s in range(rows_per_tile // PEER_SLICE_ROWS):
        rows = slice(s * PEER_SLICE_ROWS * NK, (s + 1) * PEER_SLICE_ROWS * NK)
        act_ref[rows, :] = jnp.dot(u_ref[rows, :], ht_ref[...], preferred_element_type=F32)
        for k1 in range(s * PEER_SLICE_ROWS, (s + 1) * PEER_SLICE_ROWS):
            for c in range(nchunks):
                cols = slice(c * LANES, (c + 1) * LANES)
                g = None
                for h in range(PEER_HEADS):
                    selected = r2_ref[h, :, cols] < key1_rows(n1_ref, h, k1, cols)
                    term = jnp.where(selected, e2_ref[h, :, cols], zero) * key1_rows(e1_ref, h, k1, cols)
                    g = term if g is None else g + term
                g_ref[k1 * NK:(k1 + 1) * NK, cols] = g.astype(F32)
    for k1 in range(rows_per_tile):
        erows = slice(k1 * NK, (k1 + 1) * NK)
        for c in range(nchunks):
            cols = slice(c * LANES, (c + 1) * LANES)
            w = g_ref[erows, cols] * _gelu_erf(act_ref[erows, cols] * scale_ref[:, cols]) * wscale_ref[:, cols]
            w_ref[cols, erows] = w.T.astype(w_ref.dtype)


def peer_gate(ht, u, layer, scale, wscale, tables, tt=PEER_TOKEN_TILE, te=PEER_EXPERT_TILE):
    D, T = ht.shape
    E = u.shape[1]
    H, NK, _ = tables[0].shape
    tt = min(tt, T)
    assert T % tt == 0 and E % te == 0
    rows_per_tile = te // NK
    assert rows_per_tile % SUBLANES == 0
    k1spec = pl.BlockSpec((H, rows_per_tile, tt), lambda i, j: (0, j, i))
    k2spec = pl.BlockSpec((H, NK, tt), lambda i, j: (0, 0, i))
    return pl.pallas_call(
        functools.partial(_peer_gate_kernel, rows_per_tile=rows_per_tile),
        grid=(T // tt, E // te),
        in_specs=[pl.BlockSpec((D, tt), lambda i, j: (0, i)),
                  pl.BlockSpec((None, te, D), lambda i, j: (layer, j, 0)),
                  pl.BlockSpec((1, tt), lambda i, j: (0, i)),
                  pl.BlockSpec((1, tt), lambda i, j: (0, i)),
                  k1spec, k1spec, k2spec, k2spec],
        out_specs=pl.BlockSpec((tt, te), lambda i, j: (i, j)),
        out_shape=jax.ShapeDtypeStruct((T, E), FP8),
        scratch_shapes=[pltpu.VMEM((te, tt), F32), pltpu.VMEM((te, tt), F32)],
        name="peer_gate",
        compiler_params=_params("parallel", "arbitrary"),
    )(ht, u, scale, wscale, *tables)


def quantise_table(w):
    amax = jnp.maximum(jnp.max(jnp.abs(w), axis=(1, 2)), FP8_TINY)
    row_norm = jnp.sqrt(jnp.max(jnp.sum(w * w, axis=2), axis=1))
    return (w * (FP8_TARGET_MAX / amax)[:, None, None]).astype(FP8), amax * (1.0 / FP8_TARGET_MAX), row_norm


def peer_ffn(x, layer, norm_g, wq, subkeys, u_quant, v_quant):
    u_fp8, u_scale, u_norm = u_quant
    v_fp8, v_scale, _ = v_quant
    h, ht, t_scale, h_norm = rmsnorm(x, norm_g, BF16, with_fp8_transpose=True)
    q = matmul(h, wq, layer=layer, name="peer_query")
    tables = peer_topk(q, subkeys)
    bound = jnp.maximum(PEER_HEADS * h_norm * u_norm[layer], FP8_TINY)
    w = peer_gate(ht, u_fp8, layer, t_scale * u_scale[layer], FP8_TARGET_MAX / bound, tables)
    undo = (bound * (v_scale[layer] / FP8_TARGET_MAX)).reshape(-1, 1)
    return matmul(w, v_fp8, res=x, row_scale=undo, layer=layer, name="peer_out")


def _in_proj_layout(group_width):
    gw = group_width
    parts = ((('lru_x', 'lru_gate', 'gla_q', 'gla_k', 'gla_v'), (gw, gw, gw // 2, gw // 2, gw)),
             (('gla_og', 's5_u', 'moba_q', 'moba_k', 'moba_v'), (gw,) * 5))
    cols = {}
    for names, sizes in parts:
        c = 0
        for name, size in zip(names, sizes):
            cols[name] = c // LANES
            c += size
    return sum(parts[0][1]), cols


def kernel(x, norm1_g, w_in, lru_conv_w, lru_conv_b, lru_wa, lru_ba, lru_wx, lru_bx, lru_lambda, gla_wg2, gla_bg, gla_norm_g, s5_a_re, s5_a_im, s5_log_step, s5_b_re, s5_b_im, s5_c_re, s5_c_im, s5_d, s5_w_glu, s5_b_glu, w_out, norm2_g, peer_wq, peer_subkeys, peer_u, peer_v, final_norm_g):
    B, S, D = x.shape
    T = B * S
    depth = norm1_g.shape[0]
    gw = D // N_MIXERS
    x = x.reshape(T, D)
    lr0, col = _in_proj_layout(gw)
    w_in_t = jnp.transpose(w_in, (0, 2, 1)).astype(BF16)
    w_b = w_in_t[:, lr0 + GLA_GATE_RANK:]
    w_lr = jnp.zeros((depth, LANES, D), BF16).at[:, :GLA_GATE_RANK].set(w_in_t[:, lr0:lr0 + GLA_GATE_RANK])
    w_out_bf16 = w_out.astype(BF16)
    peer_wq_bf16 = peer_wq.astype(BF16)
    peer_u_quant = quantise_table(peer_u)
    peer_v_quant = quantise_table(peer_v)
    for l in range(depth):
        h = rmsnorm(x, norm1_g[l], BF16)
        pa = matmul(h, w_in_t, layer=l, n_out=lr0, w_transposed=True, name="in_proj_a")
        pb = matmul(h, w_b, layer=l, w_transposed=True, name="in_proj_b")
        plr = matmul(h, w_lr, layer=l, w_transposed=True, name="in_proj_lr")
        y_a = rglru(pa, B, S, col['lru_x'], col['lru_gate'], lru_conv_w[l], lru_conv_b[l], lru_wa[l], lru_ba[l],
                    lru_wx[l], lru_bx[l], lru_lambda[l])
        y_b = gla(pa, pb, plr, B, S, col['gla_q'], col['gla_k'], col['gla_v'], col['gla_og'],
                  gla_wg2[l], gla_bg[l], gla_norm_g[l])
        prep = s5_prepare(s5_a_re[l], s5_a_im[l], s5_log_step[l], s5_b_re[l], s5_b_im[l], s5_c_re[l], s5_c_im[l])
        z = s5_ssm(pb, B, S, col['s5_u'], prep, s5_d[l])
        y_c = glu(z, s5_w_glu[l], s5_b_glu[l])
        y_d = moba(pb, B, S, col['moba_q'], col['moba_k'], col['moba_v'])
        mixed = jnp.concatenate([y_a, y_b, y_c, y_d], axis=-1)
        x = matmul(mixed, w_out_bf16, res=x, layer=l, fuse_x_producer=True, name="out_proj")
        x = peer_ffn(x, l, norm2_g[l], peer_wq_bf16, peer_subkeys[l], peer_u_quant, peer_v_quant)
    return rmsnorm(x, final_norm_g, F32).reshape(B, S, D)
```
